```python
import jax
import jax.numpy as jnp
from jax import lax
import numpy as np

D_MODEL = 2048
BATCH = 8
SEQ = 4096
DEPTH = 1

HEAD_DIM = 128
M_WIDTH = D_MODEL // 4
A_WIDTH = (D_MODEL - M_WIDTH) // 2
B_WIDTH = D_MODEL - M_WIDTH - A_WIDTH
M_HEADS = M_WIDTH // HEAD_DIM
A_GROUPS = A_WIDTH // HEAD_DIM
B_HEADS = B_WIDTH // HEAD_DIM
D_PROJ = 2 * A_WIDTH + 3 * B_WIDTH + M_WIDTH
CHUNK = 128
DILATED_PATTERNS = ((128, 1), (512, 4), (2048, 16))
ATT_BLOCK = 128
N_MEM = 256
N_EXPERTS = 32
TOP_K = 4
D_FF = D_MODEL
SWIGLU_LIMIT = 7.0
SWIGLU_ALPHA = 1.702
MOE_BLOCK = 512
LN_EPS = 1e-5
NEG_INF = -1e30
DN_ALPHA = (2.0 * DEPTH) ** 0.25
DN_BETA = (8.0 * DEPTH) ** -0.25

kernel_name = "hymba_gmlp_longnet_mem_moe_deepnorm"


def layer_norm(x, g, b):
    xf = x.astype(jnp.float32)
    mu = jnp.mean(xf, axis=-1, keepdims=True)
    var = jnp.mean(jnp.square(xf - mu), axis=-1, keepdims=True)
    return ((xf - mu) * lax.rsqrt(var + LN_EPS)).astype(x.dtype) * g + b


def rms_norm(x, g):
    xf = x.astype(jnp.float32)
    return (xf * lax.rsqrt(jnp.mean(jnp.square(xf), axis=-1, keepdims=True) + LN_EPS)).astype(x.dtype) * g


def chunked_spatial_gating(a, w_spatial, b_spatial, ln_g, ln_b):
    bsz, seq, _ = a.shape
    u, v = jnp.split(a, 2, axis=-1)
    v = layer_norm(v, ln_g, ln_b)
    v = v.reshape(bsz, seq // CHUNK, CHUNK, A_GROUPS, HEAD_DIM)
    causal = jnp.tril(jnp.ones((CHUNK, CHUNK), dtype=bool))
    w = jnp.where(causal[None], w_spatial, 0)
    mixed = jnp.einsum('gts,bnsgc->bntgc', w, v) + b_spatial.T[None, None, :, :, None]
    return u * mixed.reshape(bsz, seq, A_WIDTH)


def dilated_window_attention(q, k, v, window, dilation):
    bsz, seq, nh, dh = q.shape
    span = window // dilation
    sub_len = seq // dilation
    nb = -(-sub_len // ATT_BLOCK)
    lp = nb * ATT_BLOCK

    def to_sub(t):
        t = t.reshape(bsz, sub_len, dilation, nh, dh).transpose(0, 2, 1, 3, 4)
        return jnp.pad(t, ((0, 0), (0, 0), (0, lp - sub_len), (0, 0), (0, 0)))

    def windowed(t):
        prev = jnp.pad(t, ((0, 0), (0, 0), (ATT_BLOCK, 0), (0, 0), (0, 0)))[:, :, :lp]
        prev = prev.reshape(bsz, dilation, nb, ATT_BLOCK, nh, dh)
        cur = t.reshape(bsz, dilation, nb, ATT_BLOCK, nh, dh)
        return jnp.concatenate([prev, cur], axis=3)

    qb = to_sub(q).reshape(bsz, dilation, nb, ATT_BLOCK, nh, dh)
    kw = windowed(to_sub(k))
    vw = windowed(to_sub(v))
    s = jnp.einsum('brnqhc,brnkhc->brnhqk', qb, kw,
                   preferred_element_type=jnp.float32) * (HEAD_DIM ** -0.5)
    qi = jnp.arange(ATT_BLOCK)[:, None]
    kj = jnp.arange(2 * ATT_BLOCK)[None, :]
    dist = qi + ATT_BLOCK - kj
    key_pos = jnp.arange(nb)[:, None, None] * ATT_BLOCK - ATT_BLOCK + kj[None]
    mask = (dist >= 0) & (dist <= span) & (key_pos >= 0)
    s = jnp.where(mask[None, None, :, None], s, NEG_INF)
    lse = jax.nn.logsumexp(s, axis=-1)
    p = jnp.exp(s - lse[..., None])
    o = jnp.einsum('brnhqk,brnkhc->brnqhc', p.astype(v.dtype), vw)

    def from_sub(t):
        t = jnp.moveaxis(t[:, :, :sub_len], 1, 2)
        return t.reshape(bsz, seq, *t.shape[3:])

    o = from_sub(o.reshape(bsz, dilation, lp, nh, dh))
    lse = from_sub(lse.transpose(0, 1, 2, 4, 3).reshape(bsz, dilation, lp, nh))
    return o, lse


def dilated_mixture_attention(q, k, v):
    outs, lses = [], []
    for window, dilation in DILATED_PATTERNS:
        o, lse = dilated_window_attention(q, k, v, window, dilation)
        outs.append(o)
        lses.append(lse)
    wts = jax.nn.softmax(jnp.stack(lses, axis=0), axis=0)
    o = jnp.einsum('pbsh,pbshc->bshc', wts, jnp.stack(outs, axis=0).astype(jnp.float32))
    return o.astype(q.dtype)


def memory_cross_attention(q, mem, w_mem_kv):
    bsz, n_mem, _ = mem.shape
    kv = (mem @ w_mem_kv).reshape(bsz, n_mem, 2, M_HEADS, HEAD_DIM)
    k, v = kv[:, :, 0], kv[:, :, 1]
    s = jnp.einsum('bshc,bmhc->bhsm', q, k, preferred_element_type=jnp.float32) * (HEAD_DIM ** -0.5)
    p = jax.nn.softmax(s, axis=-1)
    return jnp.einsum('bhsm,bmhc->bshc', p.astype(v.dtype), v)


def mixing_sublayer(x, mem, w_in, w_spatial, b_spatial, a_ln_g, a_ln_b, w_mem_kv,
                    norm_a_g, norm_b_g, norm_m_g, w_out):
    bsz, seq, _ = x.shape
    z = x @ w_in
    o1 = 2 * A_WIDTH
    za, zq, zk, zv, zm = jnp.split(z, [o1, o1 + B_WIDTH, o1 + 2 * B_WIDTH, o1 + 3 * B_WIDTH], axis=-1)
    y_a = chunked_spatial_gating(jax.nn.gelu(za), w_spatial, b_spatial, a_ln_g, a_ln_b)
    heads = lambda t, h: t.reshape(bsz, seq, h, HEAD_DIM)
    y_b = dilated_mixture_attention(heads(zq, B_HEADS), heads(zk, B_HEADS), heads(zv, B_HEADS))
    y_m = memory_cross_attention(heads(zm, M_HEADS), mem, w_mem_kv)
    y = jnp.concatenate([rms_norm(y_a, norm_a_g),
                         rms_norm(y_b.reshape(bsz, seq, B_WIDTH), norm_b_g),
                         rms_norm(y_m.reshape(bsz, seq, M_WIDTH), norm_m_g)], axis=-1)
    return y @ w_out


def moe_ffn(h, w_router, b_router, w_gate_up, b_gate_up, w_down, b_down):
    bsz, seq, d = h.shape
    n_tok = bsz * seq
    hf = h.reshape(n_tok, d)
    logits = (hf @ w_router + b_router).astype(jnp.float32)
    top_val, top_idx = lax.top_k(logits, TOP_K)
    gates = jax.nn.softmax(top_val, axis=-1).astype(h.dtype)
    n_assign = n_tok * TOP_K
    flat_e = top_idx.reshape(-1)
    flat_tok = jnp.arange(n_assign, dtype=jnp.int32) // TOP_K
    flat_g = gates.reshape(-1)
    order = jnp.argsort(flat_e)
    sorted_e = flat_e[order]
    counts = jnp.bincount(flat_e, length=N_EXPERTS)
    padded = (counts + MOE_BLOCK - 1) // MOE_BLOCK * MOE_BLOCK
    start = jnp.cumsum(counts) - counts
    pad_end = jnp.cumsum(padded)
    pad_start = pad_end - padded
    rank = jnp.arange(n_assign, dtype=jnp.int32) - start[sorted_e]
    dest = pad_start[sorted_e] + rank
    n_blocks = (n_assign + N_EXPERTS * (MOE_BLOCK - 1)) // MOE_BLOCK + 1
    n_rows = n_blocks * MOE_BLOCK
    row_tok = jnp.zeros((n_rows,), jnp.int32).at[dest].set(flat_tok[order])
    row_gate = jnp.zeros((n_rows,), h.dtype).at[dest].set(flat_g[order])
    block_expert = jnp.minimum(
        jnp.searchsorted(pad_end, jnp.arange(n_blocks, dtype=jnp.int32) * MOE_BLOCK, side='right'),
        N_EXPERTS - 1)
    xs = hf[row_tok].reshape(n_blocks, MOE_BLOCK, d)

    def expert_block(args):
        xb, e = args
        gu = xb @ w_gate_up[e] + b_gate_up[e]
        gate, up = jnp.split(gu, 2, axis=-1)
        gate = jnp.minimum(gate, SWIGLU_LIMIT)
        up = jnp.clip(up, -SWIGLU_LIMIT, SWIGLU_LIMIT)
        act = (up + 1.0) * (gate * jax.nn.sigmoid(SWIGLU_ALPHA * gate))
        return act @ w_down[e] + b_down[e]

    ys = lax.map(expert_block, (xs, block_expert)).reshape(n_rows, d) * row_gate[:, None]
    out = jax.ops.segment_sum(ys, row_tok, num_segments=n_tok)
    return out.reshape(bsz, seq, d)


def setup_inputs(seed: int = 0) -> dict:
    key = jax.random.key(seed)
    ks = jax.random.split(key, 22)
    nrm = lambda k, shape, scale: jax.random.normal(k, shape, jnp.float32) * scale
    L = DEPTH
    return {
        "x": nrm(ks[0], (BATCH, SEQ, D_MODEL), 1.0),
        "mem": nrm(ks[1], (BATCH, N_MEM, D_MODEL), 1.0),
        "w_in": nrm(ks[2], (L, D_MODEL, D_PROJ), D_MODEL ** -0.5),
        "w_spatial": nrm(ks[3], (L, A_GROUPS, CHUNK, CHUNK), CHUNK ** -0.5),
        "b_spatial": 1.0 + nrm(ks[4], (L, A_GROUPS, CHUNK), 0.1),
        "a_ln_g": 1.0 + nrm(ks[5], (L, A_WIDTH), 0.02),
        "a_ln_b": nrm(ks[6], (L, A_WIDTH), 0.02),
        "w_mem_kv": nrm(ks[7], (L, D_MODEL, 2 * M_WIDTH), D_MODEL ** -0.5),
        "norm_a_g": 1.0 + nrm(ks[8], (L, A_WIDTH), 0.02),
        "norm_b_g": 1.0 + nrm(ks[9], (L, B_WIDTH), 0.02),
        "norm_m_g": 1.0 + nrm(ks[10], (L, M_WIDTH), 0.02),
        "w_out": nrm(ks[11], (L, D_MODEL, D_MODEL), D_MODEL ** -0.5 * DN_BETA),
        "ln1_g": 1.0 + nrm(ks[12], (L, D_MODEL), 0.02),
        "ln1_b": nrm(ks[13], (L, D_MODEL), 0.02),
        "w_router": nrm(ks[14], (L, D_MODEL, N_EXPERTS), D_MODEL ** -0.5),
        "b_router": nrm(ks[15], (L, N_EXPERTS), 0.01),
        "w_gate_up": nrm(ks[16], (L, N_EXPERTS, D_MODEL, 2 * D_FF), D_MODEL ** -0.5),
        "b_gate_up": nrm(ks[17], (L, N_EXPERTS, 2 * D_FF), 0.01),
        "w_down": nrm(ks[18], (L, N_EXPERTS, D_FF, D_MODEL), D_FF ** -0.5 * DN_BETA),
        "b_down": nrm(ks[19], (L, N_EXPERTS, D_MODEL), 0.01),
        "ln2_g": 1.0 + nrm(ks[20], (L, D_MODEL), 0.02),
        "ln2_b": nrm(ks[21], (L, D_MODEL), 0.02),
    }


def reference(x, mem, w_in, w_spatial, b_spatial, a_ln_g, a_ln_b, w_mem_kv,
              norm_a_g, norm_b_g, norm_m_g, w_out, ln1_g, ln1_b,
              w_router, b_router, w_gate_up, b_gate_up, w_down, b_down, ln2_g, ln2_b):
    for l in range(DEPTH):
        mix = mixing_sublayer(x, mem, w_in[l], w_spatial[l], b_spatial[l], a_ln_g[l], a_ln_b[l],
                              w_mem_kv[l], norm_a_g[l], norm_b_g[l], norm_m_g[l], w_out[l])
        x = layer_norm(DN_ALPHA * x + mix, ln1_g[l], ln1_b[l])
        ffn = moe_ffn(x, w_router[l], b_router[l], w_gate_up[l], b_gate_up[l], w_down[l], b_down[l])
        x = layer_norm(DN_ALPHA * x + ffn, ln2_g[l], ln2_b[l])
    return x
```

```python
import functools

import jax
import jax.numpy as jnp
from jax import lax
from jax.experimental import pallas as pl
from jax.experimental.pallas import tpu as pltpu

F32 = jnp.float32
BF16 = jnp.bfloat16
I32 = jnp.int32
U32 = jnp.uint32

HEAD_DIM = 128
CHUNK = 128
ATT_BLOCK = 128
DILATIONS = (1, 4, 16)
TOP_K = 4
SWIGLU_LIMIT = 7.0
SWIGLU_ALPHA = 1.702
LN_EPS = 1e-5
NEG_INF = -1e30

V7X_VMEM_LIMIT_BYTES = 56 * 1024 * 1024

IN_PROJ_ROWS = 256
LOCAL_ROWS = 512
OUT_ROWS = 256
DISPATCH_ROWS = 256
COMBINE_ROWS = 256
EXPERT_ROWS = 1024
EXPERT_FF_TILE = 256
EXPERT_OUT_TILE = 512


def _params(semantics):
    return pltpu.CompilerParams(dimension_semantics=semantics,
                                vmem_limit_bytes=V7X_VMEM_LIMIT_BYTES)


def _full_vmem():
    return pl.BlockSpec(memory_space=pltpu.VMEM)


def _layer_norm(h, g, b):
    mu = jnp.mean(h, axis=-1, keepdims=True)
    hc = h - mu
    var = jnp.mean(hc * hc, axis=-1, keepdims=True)
    return hc * lax.rsqrt(var + LN_EPS) * g + b


def _rms_norm(y, g):
    return y * lax.rsqrt(jnp.mean(y * y, axis=-1, keepdims=True) + LN_EPS) * g


def _gelu_tanh(x):
    c = 0.7978845608028654
    return x * (0.5 * (1.0 + jnp.tanh(c * (x + 0.044715 * (x * x * x)))))


def _dot_nt(a, b):
    return lax.dot_general(a, b, (((1,), (1,)), ((), ())), preferred_element_type=F32)


def _in_proj_kernel(bounds, x_ref, w_ref, *out_refs):
    xb = x_ref[...].astype(BF16)
    for o_ref, (lo, hi) in zip(out_refs, bounds):
        o_ref[...] = jnp.dot(xb, w_ref[:, lo:hi], preferred_element_type=F32)


def _in_proj(x2, w_in_bf, widths):
    t, d = x2.shape
    tm = IN_PROJ_ROWS
    bounds, lo = [], 0
    for w in widths:
        bounds.append((lo, lo + w))
        lo += w
    return pl.pallas_call(
        functools.partial(_in_proj_kernel, tuple(bounds)),
        grid=(t // tm,),
        in_specs=[pl.BlockSpec((tm, d), lambda i: (i, 0)), _full_vmem()],
        out_specs=[pl.BlockSpec((tm, w), lambda i: (i, 0)) for w in widths],
        out_shape=[jax.ShapeDtypeStruct((t, w), F32) for w in widths],
        compiler_params=_params(("arbitrary",)),
        name="in_proj",
    )(x2, w_in_bf)


def _kv_proj_kernel(mem_ref, w_ref, kv_ref):
    kv_ref[...] = jnp.dot(mem_ref[...].astype(BF16), w_ref[...],
                          preferred_element_type=F32).astype(BF16)


def _kv_proj(mem, w_kv_bf):
    b, n_mem, d = mem.shape
    n = w_kv_bf.shape[1]
    return pl.pallas_call(
        _kv_proj_kernel,
        grid=(b,),
        in_specs=[pl.BlockSpec((None, n_mem, d), lambda i: (i, 0, 0)), _full_vmem()],
        out_specs=pl.BlockSpec((None, n_mem, n), lambda i: (i, 0, 0)),
        out_shape=jax.ShapeDtypeStruct((b, n_mem, n), BF16),
        compiler_params=_params(("arbitrary",)),
        name="kv_proj",
    )(mem, w_kv_bf)


def _local_mix_kernel(za_ref, zm_ref, kv_ref, wsp_ref, bsp_ref, lng_ref, lnb_ref, nag_ref, nmg_ref,
                      ya_ref, ym_ref, ya_acc, ym_acc):
    rows = za_ref.shape[0]
    a_width = za_ref.shape[1] // 2
    n_groups = a_width // HEAD_DIM
    n_chunks = rows // CHUNK
    m_width = zm_ref.shape[1]
    m_heads = m_width // HEAD_DIM

    act = _gelu_tanh(za_ref[...])
    u = act[:, :a_width]
    v = _layer_norm(act[:, a_width:], lng_ref[...], lnb_ref[...])
    t_idx = lax.broadcasted_iota(I32, (CHUNK, CHUNK), 0)
    s_idx = lax.broadcasted_iota(I32, (CHUNK, CHUNK), 1)
    causal = s_idx <= t_idx
    bsp = bsp_ref[...]
    for g in range(n_groups):
        cols = slice(g * HEAD_DIM, (g + 1) * HEAD_DIM)
        w_g = jnp.where(causal, wsp_ref[g], 0.0).astype(BF16)
        v_g = jnp.concatenate([v[c * CHUNK:(c + 1) * CHUNK, cols] for c in range(n_chunks)], axis=1)
        mixed = jnp.dot(w_g, v_g.astype(BF16), preferred_element_type=F32) + bsp[:, g:g + 1]
        for c in range(n_chunks):
            rs = slice(c * CHUNK, (c + 1) * CHUNK)
            ya_acc[rs, cols] = u[rs, cols] * mixed[:, c * HEAD_DIM:(c + 1) * HEAD_DIM]
    ya_ref[...] = _rms_norm(ya_acc[...], nag_ref[...]).astype(BF16)

    scale = HEAD_DIM ** -0.5
    q = zm_ref[...]
    for h in range(m_heads):
        cols = slice(h * HEAD_DIM, (h + 1) * HEAD_DIM)
        k_h = kv_ref[:, h * HEAD_DIM:(h + 1) * HEAD_DIM]
        v_h = kv_ref[:, m_width + h * HEAD_DIM:m_width + (h + 1) * HEAD_DIM]
        s = _dot_nt(q[:, cols].astype(BF16), k_h) * scale
        p = jnp.exp(s - jnp.max(s, axis=-1, keepdims=True))
        l = jnp.sum(p, axis=-1, keepdims=True)
        ym_acc[:, cols] = jnp.dot(p.astype(BF16), v_h, preferred_element_type=F32) / l
    ym_ref[...] = _rms_norm(ym_acc[...], nmg_ref[...]).astype(BF16)


def _local_mix(za, zm, kv, w_spatial, b_spatial_t, a_ln_g, a_ln_b, norm_a_g, norm_m_g, batch, seq):
    rows = LOCAL_ROWS
    a2 = za.shape[1]
    a_width = a2 // 2
    m_width = zm.shape[1]
    n_mem, kv_w = kv.shape[1], kv.shape[2]
    tiles = seq // rows
    return pl.pallas_call(
        _local_mix_kernel,
        grid=(batch, tiles),
        in_specs=[
            pl.BlockSpec((rows, a2), lambda b, i: (b * tiles + i, 0)),
            pl.BlockSpec((rows, m_width), lambda b, i: (b * tiles + i, 0)),
            pl.BlockSpec((None, n_mem, kv_w), lambda b, i: (b, 0, 0)),
            _full_vmem(), _full_vmem(), _full_vmem(), _full_vmem(), _full_vmem(), _full_vmem(),
        ],
        out_specs=[pl.BlockSpec((rows, a_width), lambda b, i: (b * tiles + i, 0)),
                   pl.BlockSpec((rows, m_width), lambda b, i: (b * tiles + i, 0))],
        out_shape=[jax.ShapeDtypeStruct((batch * seq, a_width), BF16),
                   jax.ShapeDtypeStruct((batch * seq, m_width), BF16)],
        scratch_shapes=[pltpu.VMEM((rows, a_width), F32), pltpu.VMEM((rows, m_width), F32)],
        compiler_params=_params(("arbitrary", "arbitrary")),
        name="local_mix",
    )(za, zm, kv, w_spatial, b_spatial_t, a_ln_g, a_ln_b, norm_a_g, norm_m_g)


def _dilated_kernel(q_ref, k_ref, v_ref, o_ref, m_s, l_s, a_s):
    seq = q_ref.shape[0]
    scale = HEAD_DIM ** -0.5
    blk = ATT_BLOCK
    qi = lax.broadcasted_iota(I32, (blk, 2 * blk), 0)
    kj = lax.broadcasted_iota(I32, (blk, 2 * blk), 1)
    band = (kj >= qi) & (kj <= qi + blk)
    qi1 = lax.broadcasted_iota(I32, (blk, blk), 0)
    kj1 = lax.broadcasted_iota(I32, (blk, blk), 1)
    first = kj1 <= qi1

    def scores(q, k, mask):
        s = _dot_nt(q.astype(BF16), k.astype(BF16)) * scale
        s = jnp.where(mask, s, NEG_INF)
        m = jnp.max(s, axis=-1, keepdims=True)
        p = jnp.exp(s - m)
        return m, p

    def block(q_rows, k_rows, mask, merge):
        m, p = scores(q_ref[q_rows, :], k_ref[k_rows, :], mask)
        l = jnp.sum(p, axis=-1, keepdims=True)
        acc = jnp.dot(p.astype(BF16), v_ref[k_rows, :].astype(BF16), preferred_element_type=F32)
        if merge:
            m_old = m_s[q_rows, :]
            m_new = jnp.maximum(m_old, m)
            c_old = jnp.exp(m_old - m_new)
            c_new = jnp.exp(m - m_new)
            m_s[q_rows, :] = m_new
            l_s[q_rows, :] = c_old * l_s[q_rows, :] + c_new * l
            a_s[q_rows, :] = c_old * a_s[q_rows, :] + c_new * acc
        else:
            m_s[q_rows, :] = jnp.broadcast_to(m, (blk, HEAD_DIM))
            l_s[q_rows, :] = jnp.broadcast_to(l, (blk, HEAD_DIM))
            a_s[q_rows, :] = acc

    for d in DILATIONS:
        n_blocks = seq // (blk * d)
        merge = d != 1

        def residue(r, carry, d=d, n_blocks=n_blocks, merge=merge):
            def rows(start, n):
                if d == 1:
                    return pl.ds(start, n)
                return pl.ds(start, n, stride=d)

            block(rows(r, blk), rows(r, blk), first, merge)

            def later(n, c):
                q0 = n * (blk * d) + r
                block(rows(q0, blk), rows(q0 - blk * d, 2 * blk), band, merge)
                return c

            lax.fori_loop(1, n_blocks, later, 0)
            return carry

        lax.fori_loop(0, d, residue, 0)

    o_ref[...] = a_s[...] / l_s[...]


def _dilated_attention(zq, zk, zv, batch, seq):
    width = zq.shape[1]
    heads = width // HEAD_DIM
    spec = pl.BlockSpec((seq, HEAD_DIM), lambda b, h: (b, h))
    return pl.pallas_call(
        _dilated_kernel,
        grid=(batch, heads),
        in_specs=[spec, spec, spec],
        out_specs=spec,
        out_shape=jax.ShapeDtypeStruct((batch * seq, width), F32),
        scratch_shapes=[pltpu.VMEM((seq, HEAD_DIM), F32)] * 3,
        compiler_params=_params(("arbitrary", "arbitrary")),
        name="dilated",
    )(zq, zk, zv)


def _out_router_kernel(alpha, x_ref, ya_ref, yb_ref, ym_ref, wo_ref, nbg_ref, g_ref, b_ref,
                       wrh_ref, wrl_ref, br_ref,
                       x1_ref, idx_ref, gate_ref, rank_ref, cnt_ref, base):
    step = pl.program_id(0)
    tm = x_ref.shape[0]
    a_w = ya_ref.shape[1]
    b_w = yb_ref.shape[1]
    n_exp = br_ref.shape[1]

    @pl.when(step == 0)
    def _():
        base[...] = jnp.zeros_like(base)

    yb = _rms_norm(yb_ref[...], nbg_ref[...]).astype(BF16)
    mix = jnp.dot(ya_ref[...], wo_ref[0:a_w, :], preferred_element_type=F32)
    mix += jnp.dot(yb, wo_ref[a_w:a_w + b_w, :], preferred_element_type=F32)
    mix += jnp.dot(ym_ref[...], wo_ref[a_w + b_w:, :], preferred_element_type=F32)
    x1 = _layer_norm(alpha * x_ref[...] + mix, g_ref[...], b_ref[...])
    x1_ref[...] = x1

    x_hi = x1.astype(BF16)
    x_lo = (x1 - x_hi.astype(F32)).astype(BF16)
    logits = (jnp.dot(x_hi, wrh_ref[...], preferred_element_type=F32)
              + jnp.dot(x_lo, wrh_ref[...], preferred_element_type=F32)
              + jnp.dot(x_hi, wrl_ref[...], preferred_element_type=F32)) + br_ref[...]

    lane = lax.broadcasted_iota(I32, (tm, n_exp), 1).astype(F32)
    work = logits
    vals, sels = [], []
    for _ in range(TOP_K):
        m = jnp.max(work, axis=-1, keepdims=True)
        sel = jnp.min(jnp.where(work == m, lane, float(n_exp)), axis=-1, keepdims=True)
        vals.append(m)
        sels.append(sel)
        work = jnp.where(lane == sel, -jnp.inf, work)
    exps = [jnp.exp(v - vals[0]) for v in vals]
    denom = exps[0] + exps[1] + exps[2] + exps[3]

    hot = [lane == s for s in sels]
    multi = sum(h.astype(F32) for h in hot)
    t_i = lax.broadcasted_iota(I32, (tm, tm), 0)
    t_j = lax.broadcasted_iota(I32, (tm, tm), 1)
    earlier = (t_j < t_i).astype(BF16)
    before = jnp.dot(earlier, multi.astype(BF16), preferred_element_type=F32) + base[...]
    k_lane = lax.broadcasted_iota(I32, (tm, TOP_K), 1)
    idx_out = jnp.zeros((tm, TOP_K), I32)
    gate_out = jnp.zeros((tm, TOP_K), F32)
    rank_out = jnp.zeros((tm, TOP_K), I32)
    for k in range(TOP_K):
        rank_k = jnp.sum(jnp.where(hot[k], before, 0.0), axis=-1, keepdims=True).astype(I32)
        idx_out = jnp.where(k_lane == k, sels[k].astype(I32), idx_out)
        gate_out = jnp.where(k_lane == k, exps[k] / denom, gate_out)
        rank_out = jnp.where(k_lane == k, rank_k, rank_out)
    idx_ref[...] = idx_out
    gate_ref[...] = gate_out
    rank_ref[...] = rank_out
    base[...] += jnp.sum(multi, axis=0, keepdims=True)
    cnt_ref[...] = base[...].astype(I32)


def _out_router(alpha, x2, ya, yb, ym, w_out_bf, norm_b_g, ln_g, ln_b, wr_hi, wr_lo, b_router):
    t, d = x2.shape
    tm = OUT_ROWS
    n_exp = b_router.shape[1]
    row = lambda w: pl.BlockSpec((tm, w), lambda i: (i, 0))
    return pl.pallas_call(
        functools.partial(_out_router_kernel, alpha),
        grid=(t // tm,),
        in_specs=[row(d), row(ya.shape[1]), row(yb.shape[1]), row(ym.shape[1])] + [_full_vmem()] * 7,
        out_specs=[row(d), row(TOP_K), row(TOP_K), row(TOP_K),
                   pl.BlockSpec((1, n_exp), lambda i: (0, 0))],
        out_shape=[jax.ShapeDtypeStruct((t, d), F32),
                   jax.ShapeDtypeStruct((t, TOP_K), I32),
                   jax.ShapeDtypeStruct((t, TOP_K), F32),
                   jax.ShapeDtypeStruct((t, TOP_K), I32),
                   jax.ShapeDtypeStruct((1, n_exp), I32)],
        scratch_shapes=[pltpu.VMEM((1, n_exp), F32)],
        compiler_params=_params(("arbitrary",)),
        name="out_router",
    )(x2, ya, yb, ym, w_out_bf, norm_b_g, ln_g, ln_b, wr_hi, wr_lo, b_router)


def _dispatch_kernel(bv_ref, dest_ref, x_ref, xs_hbm, packed, zeros, sem, zsem):
    tm, d = x_ref.shape
    half = d // 2
    block_rows = zeros.shape[0]
    n_blocks = xs_hbm.shape[0] // block_rows

    @pl.when(pl.program_id(0) == 0)
    def _():
        zeros[...] = jnp.zeros_like(zeros)

        def fill(i):
            return pltpu.make_async_copy(zeros, xs_hbm.at[pl.ds(i * block_rows, block_rows), :], zsem)

        def start(i, c):
            @pl.when(bv_ref[i] < block_rows)
            def _():
                fill(i).start()
            return c

        def wait(i, c):
            @pl.when(bv_ref[i] < block_rows)
            def _():
                fill(i).wait()
            return c

        lax.fori_loop(0, n_blocks, start, 0)
        lax.fori_loop(0, n_blocks, wait, 0)

    bits = pltpu.bitcast(x_ref[...].astype(BF16).astype(F32), U32)
    packed[...] = (bits[:, :half] >> 16) | (bits[:, half:] & jnp.uint32(0xFFFF0000))

    def row_copy(t, slot):
        return pltpu.make_async_copy(packed.at[pl.ds(t, 1), :], xs_hbm.at[pl.ds(slot, 1), :], sem)

    def issue(t, c):
        for k in range(TOP_K):
            row_copy(t, dest_ref[t * TOP_K + k]).start()
        return c

    lax.fori_loop(0, tm, issue, 0, unroll=8)
    for _ in range(TOP_K):
        pltpu.make_async_copy(packed, xs_hbm.at[pl.ds(0, tm), :], sem).wait()


def _dispatch(x1, dest_flat, block_valid):
    t, d = x1.shape
    tm = DISPATCH_ROWS
    n_rows = block_valid.shape[0] * EXPERT_ROWS
    grid_spec = pltpu.PrefetchScalarGridSpec(
        num_scalar_prefetch=1,
        grid=(t // tm,),
        in_specs=[pl.BlockSpec((tm * TOP_K,), lambda i, bv: (i,), memory_space=pltpu.SMEM),
                  pl.BlockSpec((tm, d), lambda i, bv: (i, 0))],
        out_specs=pl.BlockSpec(memory_space=pl.ANY),
        scratch_shapes=[pltpu.VMEM((tm, d // 2), U32), pltpu.VMEM((EXPERT_ROWS, d // 2), U32),
                        pltpu.SemaphoreType.DMA, pltpu.SemaphoreType.DMA],
    )
    return pl.pallas_call(
        _dispatch_kernel,
        grid_spec=grid_spec,
        out_shape=jax.ShapeDtypeStruct((n_rows, d // 2), U32),
        compiler_params=_params(("arbitrary",)),
        name="dispatch",
    )(block_valid, dest_flat, x1)


def _experts_kernel(n_ff_steps, be_ref, na_ref, x_ref, wg_ref, wu_ref, bg_ref, bu_ref,
                    wd_ref, bd_ref, o_ref, xb, act):
    i = pl.program_id(0)
    s = pl.program_id(1)
    half = x_ref.shape[1]

    @pl.when(i >= na_ref[0])
    def _():
        o_ref[...] = jnp.zeros_like(o_ref)

    @pl.when(i < na_ref[0])
    def _():
        @pl.when(s == 0)
        def _():
            w = x_ref[...]
            xb[:, :half] = pltpu.bitcast(w << 16, F32).astype(BF16)
            xb[:, half:] = pltpu.bitcast(w & jnp.uint32(0xFFFF0000), F32).astype(BF16)

        @pl.when(s < n_ff_steps)
        def _():
            x = xb[...]
            gate = jnp.dot(x, wg_ref[...].astype(BF16), preferred_element_type=F32) + bg_ref[...]
            up = jnp.dot(x, wu_ref[...].astype(BF16), preferred_element_type=F32) + bu_ref[...]
            gate = jnp.minimum(gate, SWIGLU_LIMIT)
            up = jnp.clip(up, -SWIGLU_LIMIT, SWIGLU_LIMIT)
            a = (up + 1.0) * (gate * jax.nn.sigmoid(SWIGLU_ALPHA * gate))
            act[s] = a.astype(BF16)

        @pl.when(s >= n_ff_steps)
        def _():
            a = jnp.concatenate([act[j] for j in range(n_ff_steps)], axis=1)
            o_ref[...] = jnp.dot(a, wd_ref[...].astype(BF16),
                                 preferred_element_type=F32) + bd_ref[...]


def _experts(xs, block_expert, n_active, w_gate_up, b_gate_up, w_down, b_down):
    n_rows, half = xs.shape
    d = 2 * half
    n_exp, _, two_ff = w_gate_up.shape
    d_ff = two_ff // 2
    tm, tf, tn = EXPERT_ROWS, EXPERT_FF_TILE, EXPERT_OUT_TILE
    n_blocks = n_rows // tm
    nf, nn = d_ff // tf, d // tn
    bgu = b_gate_up.reshape(n_exp, 1, two_ff)
    bdn = b_down.reshape(n_exp, 1, d)

    def blk(i, na):
        return jnp.minimum(i, na[0] - 1)

    def ff_step(i, s, na):
        return jnp.where(i < na[0], jnp.minimum(s, nf - 1), nf - 1)

    def down_step(i, s, na):
        return jnp.where(i < na[0], jnp.maximum(s - nf, 0), nn - 1)

    in_specs = [
        pl.BlockSpec((tm, half), lambda i, s, be, na: (blk(i, na), 0)),
        pl.BlockSpec((None, d, tf), lambda i, s, be, na: (be[blk(i, na)], 0, ff_step(i, s, na))),
        pl.BlockSpec((None, d, tf), lambda i, s, be, na: (be[blk(i, na)], 0, nf + ff_step(i, s, na))),
        pl.BlockSpec((None, 1, tf), lambda i, s, be, na: (be[blk(i, na)], 0, ff_step(i, s, na))),
        pl.BlockSpec((None, 1, tf), lambda i, s, be, na: (be[blk(i, na)], 0, nf + ff_step(i, s, na))),
        pl.BlockSpec((None, d_ff, tn), lambda i, s, be, na: (be[blk(i, na)], 0, down_step(i, s, na))),
        pl.BlockSpec((None, 1, tn), lambda i, s, be, na: (be[blk(i, na)], 0, down_step(i, s, na))),
    ]
    grid_spec = pltpu.PrefetchScalarGridSpec(
        num_scalar_prefetch=2,
        grid=(n_blocks, nf + nn),
        in_specs=in_specs,
        out_specs=pl.BlockSpec((tm, tn), lambda i, s, be, na: (i, jnp.maximum(s - nf, 0))),
        scratch_shapes=[pltpu.VMEM((tm, d), BF16), pltpu.VMEM((nf, tm, tf), BF16)],
    )
    return pl.pallas_call(
        functools.partial(_experts_kernel, nf),
        grid_spec=grid_spec,
        out_shape=jax.ShapeDtypeStruct((n_rows, d), F32),
        compiler_params=_params(("arbitrary", "arbitrary")),
        name="experts",
    )(block_expert, n_active, xs, w_gate_up, w_gate_up, bgu, bgu, w_down, bdn)


def _combine_kernel(alpha, dest_ref, x1_ref, gate_ref, g_ref, b_ref, ys_hbm, o_ref, rows, sem):
    tm, d = x1_ref.shape

    def row_copy(k, t, slot):
        return pltpu.make_async_copy(ys_hbm.at[pl.ds(slot, 1), :], rows.at[k, pl.ds(t, 1), :], sem)

    def issue(t, c):
        for k in range(TOP_K):
            row_copy(k, t, dest_ref[t * TOP_K + k]).start()
        return c

    lax.fori_loop(0, tm, issue, 0, unroll=8)
    for k in range(TOP_K):
        pltpu.make_async_copy(ys_hbm.at[pl.ds(0, tm), :], rows.at[k], sem).wait()

    gates = gate_ref[...]
    ffn = gates[:, 0:1] * rows[0]
    for k in range(1, TOP_K):
        ffn += gates[:, k:k + 1] * rows[k]
    o_ref[...] = _layer_norm(alpha * x1_ref[...] + ffn, g_ref[...], b_ref[...])


def _combine(alpha, dest_flat, x1, gates, ln_g, ln_b, ys):
    t, d = x1.shape
    tm = COMBINE_ROWS
    return pl.pallas_call(
        functools.partial(_combine_kernel, alpha),
        grid=(t // tm,),
        in_specs=[pl.BlockSpec((tm * TOP_K,), lambda i: (i,), memory_space=pltpu.SMEM),
                  pl.BlockSpec((tm, d), lambda i: (i, 0)),
                  pl.BlockSpec((tm, TOP_K), lambda i: (i, 0)),
                  _full_vmem(), _full_vmem(),
                  pl.BlockSpec(memory_space=pl.ANY)],
        out_specs=pl.BlockSpec((tm, d), lambda i: (i, 0)),
        out_shape=jax.ShapeDtypeStruct((t, d), F32),
        scratch_shapes=[pltpu.VMEM((TOP_K, tm, d), F32), pltpu.SemaphoreType.DMA],
        compiler_params=_params(("arbitrary",)),
        name="combine",
    )(dest_flat, x1, gates, ln_g, ln_b, ys)


def _routing_tables(idx, rank, counts, n_blocks):
    tm = EXPERT_ROWS
    n_exp = counts.shape[0]
    padded = (counts + tm - 1) // tm * tm
    pad_end = jnp.cumsum(padded)
    pad_start = pad_end - padded
    dest = (pad_start[idx] + rank).reshape(-1).astype(I32)
    n_active = (pad_end[-1] // tm).astype(I32).reshape(1)
    first_row = jnp.arange(n_blocks, dtype=I32) * tm
    block_expert = jnp.minimum(jnp.searchsorted(pad_end, first_row, side="right"), n_exp - 1).astype(I32)
    block_valid = jnp.clip(counts[block_expert] - (first_row - pad_start[block_expert]), 0, tm).astype(I32)
    return dest, block_expert, block_valid, n_active


def _layer(alpha, x, mem, w_in, w_spatial, b_spatial, a_ln_g, a_ln_b, w_mem_kv, norm_a_g, norm_b_g,
           norm_m_g, w_out, ln1_g, ln1_b, w_router, b_router, w_gate_up, b_gate_up, w_down, b_down,
           ln2_g, ln2_b):
    batch, seq, d = x.shape
    t = batch * seq
    m_width = w_mem_kv.shape[1] // 2
    a_width = a_ln_g.shape[0]
    b_width = norm_b_g.shape[0]
    n_exp = w_router.shape[1]
    row = lambda v: v.reshape(1, -1)

    x2 = x.reshape(t, d)
    za, zq, zk, zv, zm = _in_proj(x2, w_in.astype(BF16), (2 * a_width, b_width, b_width, b_width, m_width))
    kv = _kv_proj(mem, w_mem_kv.astype(BF16))
    ya, ym = _local_mix(za, zm, kv, w_spatial, b_spatial.T, row(a_ln_g), row(a_ln_b),
                        row(norm_a_g), row(norm_m_g), batch, seq)
    yb = _dilated_attention(zq, zk, zv, batch, seq)

    wr_hi = w_router.astype(BF16)
    wr_lo = (w_router - wr_hi.astype(F32)).astype(BF16)
    x1, idx, gates, rank, counts = _out_router(
        alpha, x2, ya, yb, ym, w_out.astype(BF16), row(norm_b_g), row(ln1_g), row(ln1_b),
        wr_hi, wr_lo, row(b_router))

    tm = EXPERT_ROWS
    n_blocks = (t * TOP_K + n_exp * (tm - 1)) // tm
    dest, block_expert, block_valid, n_active = _routing_tables(idx, rank, counts.reshape(n_exp), n_blocks)
    xs = _dispatch(x1, dest, block_valid)
    ys = _experts(xs, block_expert, n_active, w_gate_up, b_gate_up, w_down, b_down)
    out = _combine(alpha, dest, x1, gates, row(ln2_g), row(ln2_b), ys)
    return out.reshape(batch, seq, d)


def kernel(x, mem, w_in, w_spatial, b_spatial, a_ln_g, a_ln_b, w_mem_kv, norm_a_g, norm_b_g, norm_m_g, w_out, ln1_g, ln1_b, w_router, b_router, w_gate_up, b_gate_up, w_down, b_down, ln2_g, ln2_b):
    depth = w_in.shape[0]
    alpha = (2.0 * depth) ** 0.25
    for l in range(depth):
        x = _layer(alpha, x, mem, w_in[l], w_spatial[l], b_spatial[l], a_ln_g[l], a_ln_b[l], w_mem_kv[l],
                   norm_a_g[l], norm_b_g[l], norm_m_g[l], w_out[l], ln1_g[l], ln1_b[l], w_router[l],
                   b_router[l], w_gate_up[l], b_gate_up[l], w_down[l], b_down[l], ln2_g[l], ln2_b[l])
    return x
```

```python
import functools

import jax
import jax.numpy as jnp
from jax import lax
from jax.experimental import pallas as pl
from jax.experimental.pallas import tpu as pltpu

F32 = jnp.float32
BF16 = jnp.bfloat16
I32 = jnp.int32
U32 = jnp.uint32

HEAD_DIM = 128
CHUNK = 128
ATT_BLOCK = 128
DILATIONS = (1, 4, 16)
TOP_K = 4
SWIGLU_LIMIT = 7.0
SWIGLU_ALPHA = 1.702
LN_EPS = 1e-5
NEG_INF = -1e30

V7X_VMEM_LIMIT_BYTES = 56 * 1024 * 1024

IN_PROJ_ROWS = 256
LOCAL_ROWS = 512
OUT_ROWS = 256
DISPATCH_ROWS = 256
COMBINE_ROWS = 256
DILATED_FILL_ROWS = 256
DILATED_GROUP = 16
EXPERT_ROWS = 1024
EXPERT_FF_TILE = 256
EXPERT_OUT_TILE = 512


def _params(semantics):
    return pltpu.CompilerParams(dimension_semantics=semantics,
                                vmem_limit_bytes=V7X_VMEM_LIMIT_BYTES)


def _full_vmem():
    return pl.BlockSpec(memory_space=pltpu.VMEM)


def _layer_norm(h, g, b):
    mu = jnp.mean(h, axis=-1, keepdims=True)
    hc = h - mu
    var = jnp.mean(hc * hc, axis=-1, keepdims=True)
    return hc * lax.rsqrt(var + LN_EPS) * g + b


def _rms_norm(y, g):
    return y * lax.rsqrt(jnp.mean(y * y, axis=-1, keepdims=True) + LN_EPS) * g


def _gelu_tanh(x):
    c = 0.7978845608028654
    return x * (0.5 * (1.0 + jnp.tanh(c * (x + 0.044715 * (x * x * x)))))


def _dot_nt(a, b):
    return lax.dot_general(a, b, (((1,), (1,)), ((), ())), preferred_element_type=F32)


def _in_proj_kernel(bounds, x_ref, w_ref, *out_refs):
    xb = x_ref[...].astype(BF16)
    for o_ref, (lo, hi) in zip(out_refs, bounds):
        o_ref[...] = jnp.dot(xb, w_ref[:, lo:hi], preferred_element_type=F32)


def _in_proj(x2, w_in_bf, widths):
    t, d = x2.shape
    tm = IN_PROJ_ROWS
    bounds, lo = [], 0
    for w in widths:
        bounds.append((lo, lo + w))
        lo += w
    return pl.pallas_call(
        functools.partial(_in_proj_kernel, tuple(bounds)),
        grid=(t // tm,),
        in_specs=[pl.BlockSpec((tm, d), lambda i: (i, 0)), _full_vmem()],
        out_specs=[pl.BlockSpec((tm, w), lambda i: (i, 0)) for w in widths],
        out_shape=[jax.ShapeDtypeStruct((t, w), F32) for w in widths],
        compiler_params=_params(("arbitrary",)),
        name="in_proj",
    )(x2, w_in_bf)


def _kv_proj_kernel(mem_ref, w_ref, kv_ref):
    kv_ref[...] = jnp.dot(mem_ref[...].astype(BF16), w_ref[...],
                          preferred_element_type=F32).astype(BF16)


def _kv_proj(mem, w_kv_bf):
    b, n_mem, d = mem.shape
    n = w_kv_bf.shape[1]
    return pl.pallas_call(
        _kv_proj_kernel,
        grid=(b,),
        in_specs=[pl.BlockSpec((None, n_mem, d), lambda i: (i, 0, 0)), _full_vmem()],
        out_specs=pl.BlockSpec((None, n_mem, n), lambda i: (i, 0, 0)),
        out_shape=jax.ShapeDtypeStruct((b, n_mem, n), BF16),
        compiler_params=_params(("arbitrary",)),
        name="kv_proj",
    )(mem, w_kv_bf)


def _local_mix_kernel(za_ref, zm_ref, kv_ref, wsp_ref, bsp_ref, lng_ref, lnb_ref, nag_ref, nmg_ref,
                      ya_ref, ym_ref, ya_acc, ym_acc):
    rows = za_ref.shape[0]
    a_width = za_ref.shape[1] // 2
    n_groups = a_width // HEAD_DIM
    n_chunks = rows // CHUNK
    m_width = zm_ref.shape[1]
    m_heads = m_width // HEAD_DIM

    act = _gelu_tanh(za_ref[...])
    u = act[:, :a_width]
    v = _layer_norm(act[:, a_width:], lng_ref[...], lnb_ref[...])
    t_idx = lax.broadcasted_iota(I32, (CHUNK, CHUNK), 0)
    s_idx = lax.broadcasted_iota(I32, (CHUNK, CHUNK), 1)
    causal = s_idx <= t_idx
    bsp = bsp_ref[...]
    for g in range(n_groups):
        cols = slice(g * HEAD_DIM, (g + 1) * HEAD_DIM)
        w_g = jnp.where(causal, wsp_ref[g], 0.0).astype(BF16)
        v_g = jnp.concatenate([v[c * CHUNK:(c + 1) * CHUNK, cols] for c in range(n_chunks)], axis=1)
        mixed = jnp.dot(w_g, v_g.astype(BF16), preferred_element_type=F32) + bsp[:, g:g + 1]
        for c in range(n_chunks):
            rs = slice(c * CHUNK, (c + 1) * CHUNK)
            ya_acc[rs, cols] = u[rs, cols] * mixed[:, c * HEAD_DIM:(c + 1) * HEAD_DIM]
    ya_ref[...] = _rms_norm(ya_acc[...], nag_ref[...]).astype(BF16)

    scale = HEAD_DIM ** -0.5
    q = zm_ref[...]
    for h in range(m_heads):
        cols = slice(h * HEAD_DIM, (h + 1) * HEAD_DIM)
        k_h = kv_ref[:, h * HEAD_DIM:(h + 1) * HEAD_DIM]
        v_h = kv_ref[:, m_width + h * HEAD_DIM:m_width + (h + 1) * HEAD_DIM]
        s = _dot_nt(q[:, cols].astype(BF16), k_h) * scale
        p = jnp.exp(s - jnp.max(s, axis=-1, keepdims=True))
        l = jnp.sum(p, axis=-1, keepdims=True)
        ym_acc[:, cols] = jnp.dot(p.astype(BF16), v_h, preferred_element_type=F32) / l
    ym_ref[...] = _rms_norm(ym_acc[...], nmg_ref[...]).astype(BF16)


def _local_mix(za, zm, kv, w_spatial, b_spatial_t, a_ln_g, a_ln_b, norm_a_g, norm_m_g, batch, seq):
    rows = LOCAL_ROWS
    a2 = za.shape[1]
    a_width = a2 // 2
    m_width = zm.shape[1]
    n_mem, kv_w = kv.shape[1], kv.shape[2]
    tiles = seq // rows
    return pl.pallas_call(
        _local_mix_kernel,
        grid=(batch, tiles),
        in_specs=[
            pl.BlockSpec((rows, a2), lambda b, i: (b * tiles + i, 0)),
            pl.BlockSpec((rows, m_width), lambda b, i: (b * tiles + i, 0)),
            pl.BlockSpec((None, n_mem, kv_w), lambda b, i: (b, 0, 0)),
            _full_vmem(), _full_vmem(), _full_vmem(), _full_vmem(), _full_vmem(), _full_vmem(),
        ],
        out_specs=[pl.BlockSpec((rows, a_width), lambda b, i: (b * tiles + i, 0)),
                   pl.BlockSpec((rows, m_width), lambda b, i: (b * tiles + i, 0))],
        out_shape=[jax.ShapeDtypeStruct((batch * seq, a_width), BF16),
                   jax.ShapeDtypeStruct((batch * seq, m_width), BF16)],
        scratch_shapes=[pltpu.VMEM((rows, a_width), F32), pltpu.VMEM((rows, m_width), F32)],
        compiler_params=_params(("arbitrary", "arbitrary")),
        name="local_mix",
    )(za, zm, kv, w_spatial, b_spatial_t, a_ln_g, a_ln_b, norm_a_g, norm_m_g)


def _dilated_kernel(q_ref, k_ref, v_ref, o_ref, qs, ks, vs, *state):
    seq = q_ref.shape[0]
    scale = HEAD_DIM ** -0.5
    blk = ATT_BLOCK
    n_pat = len(DILATIONS)
    m_s, l_s, a_s = state[:n_pat], state[n_pat:2 * n_pat], state[2 * n_pat:]
    qi = lax.broadcasted_iota(I32, (blk, 2 * blk), 0)
    kj = lax.broadcasted_iota(I32, (blk, 2 * blk), 1)
    band = (kj >= qi) & (kj <= qi + blk)
    zero_blk = jnp.zeros((blk, HEAD_DIM), BF16)
    vs[:, HEAD_DIM:] = jnp.ones((vs.shape[0], HEAD_DIM), BF16)

    for p, d in enumerate(DILATIONS):
        sub = seq // d
        n_blocks = sub // blk
        chunk = min(sub, DILATED_FILL_ROWS)

        def fill(i, c, d=d, sub=sub, chunk=chunk):
            r = i // (sub // chunk)
            j = i % (sub // chunk)
            first = r + j * (chunk * d)
            src = pl.ds(first, chunk) if d == 1 else pl.ds(first, chunk, stride=d)
            dst_q = pl.multiple_of(r * sub + j * chunk, blk)
            dst_kv = pl.multiple_of(r * (sub + blk) + blk + j * chunk, blk)
            qs[pl.ds(dst_q, chunk), :] = q_ref[src, :].astype(BF16)
            ks[pl.ds(dst_kv, chunk), :] = k_ref[src, :].astype(BF16)
            vs[pl.ds(dst_kv, chunk), :HEAD_DIM] = v_ref[src, :].astype(BF16)
            return c

        def pad(r, c, sub=sub):
            at = pl.multiple_of(r * (sub + blk), blk)
            ks[pl.ds(at, blk), :] = zero_blk
            vs[pl.ds(at, blk), :HEAD_DIM] = zero_blk
            return c

        lax.fori_loop(0, d, pad, 0)
        lax.fori_loop(0, d * (sub // chunk), fill, 0)

        def block(c, d=d, n_blocks=n_blocks, p=p):
            r = c // n_blocks
            n = c % n_blocks
            q = qs[pl.ds(pl.multiple_of(c * blk, blk), blk), :]
            win = pl.ds(pl.multiple_of((c + r) * blk, blk), 2 * blk)
            s = _dot_nt(q, ks[win, :]) * scale
            valid = band & (kj >= jnp.where(n > 0, 0, blk))
            s = jnp.where(valid, s, NEG_INF)
            m = jnp.max(s, axis=-1, keepdims=True)
            e = jnp.exp(s - m)
            acc = jnp.dot(e.astype(BF16), vs[win, :], preferred_element_type=F32)
            first = n * (blk * d) + r
            out = pl.ds(first, blk) if d == 1 else pl.ds(first, blk, stride=d)
            m_s[p][out, :] = jnp.broadcast_to(m, (blk, HEAD_DIM))
            l_s[p][out, :] = acc[:, HEAD_DIM:]
            a_s[p][out, :] = acc[:, :HEAD_DIM]

        def group(i, c, block=block):
            for g in range(DILATED_GROUP):
                block(i * DILATED_GROUP + g)
            return c

        lax.fori_loop(0, seq // blk // DILATED_GROUP, group, 0)

    def merge(j, c):
        rows = pl.ds(pl.multiple_of(j * blk, blk), blk)
        ms = [m[rows, :] for m in m_s]
        m_all = functools.reduce(jnp.maximum, ms)
        ws = [jnp.exp(m - m_all) for m in ms]
        den = sum(w * l[rows, :] for w, l in zip(ws, l_s))
        num = sum(w * a[rows, :] for w, a in zip(ws, a_s))
        o_ref[rows, :] = num / den
        return c

    lax.fori_loop(0, seq // blk, merge, 0)


def _dilated_attention(zq, zk, zv, batch, seq):
    width = zq.shape[1]
    heads = width // HEAD_DIM
    spec = pl.BlockSpec((seq, HEAD_DIM), lambda b, h: (b, h))
    kv_rows = seq + max(DILATIONS) * ATT_BLOCK
    return pl.pallas_call(
        _dilated_kernel,
        grid=(batch, heads),
        in_specs=[spec, spec, spec],
        out_specs=spec,
        out_shape=jax.ShapeDtypeStruct((batch * seq, width), F32),
        scratch_shapes=[pltpu.VMEM((seq, HEAD_DIM), BF16), pltpu.VMEM((kv_rows, HEAD_DIM), BF16),
                        pltpu.VMEM((kv_rows, 2 * HEAD_DIM), BF16)]
                       + [pltpu.VMEM((seq, HEAD_DIM), F32)] * (3 * len(DILATIONS)),
        compiler_params=_params(("arbitrary", "arbitrary")),
        name="dilated",
    )(zq, zk, zv)


def _out_router_kernel(alpha, x_ref, ya_ref, yb_ref, ym_ref, wo_ref, nbg_ref, g_ref, b_ref,
                       wrh_ref, wrl_ref, br_ref,
                       x1_ref, idx_ref, gate_ref, rank_ref, cnt_ref, base):
    step = pl.program_id(0)
    tm = x_ref.shape[0]
    a_w = ya_ref.shape[1]
    b_w = yb_ref.shape[1]
    n_exp = br_ref.shape[1]

    @pl.when(step == 0)
    def _():
        base[...] = jnp.zeros_like(base)

    yb = _rms_norm(yb_ref[...], nbg_ref[...]).astype(BF16)
    mix = jnp.dot(ya_ref[...], wo_ref[0:a_w, :], preferred_element_type=F32)
    mix += jnp.dot(yb, wo_ref[a_w:a_w + b_w, :], preferred_element_type=F32)
    mix += jnp.dot(ym_ref[...], wo_ref[a_w + b_w:, :], preferred_element_type=F32)
    x1 = _layer_norm(alpha * x_ref[...] + mix, g_ref[...], b_ref[...])
    x1_ref[...] = x1

    x_hi = x1.astype(BF16)
    x_lo = (x1 - x_hi.astype(F32)).astype(BF16)
    logits = (jnp.dot(x_hi, wrh_ref[...], preferred_element_type=F32)
              + jnp.dot(x_lo, wrh_ref[...], preferred_element_type=F32)
              + jnp.dot(x_hi, wrl_ref[...], preferred_element_type=F32)) + br_ref[...]

    lane = lax.broadcasted_iota(I32, (tm, n_exp), 1).astype(F32)
    work = logits
    vals, sels = [], []
    for _ in range(TOP_K):
        m = jnp.max(work, axis=-1, keepdims=True)
        sel = jnp.min(jnp.where(work == m, lane, float(n_exp)), axis=-1, keepdims=True)
        vals.append(m)
        sels.append(sel)
        work = jnp.where(lane == sel, -jnp.inf, work)
    exps = [jnp.exp(v - vals[0]) for v in vals]
    denom = exps[0] + exps[1] + exps[2] + exps[3]

    hot = [lane == s for s in sels]
    multi = sum(h.astype(F32) for h in hot)
    t_i = lax.broadcasted_iota(I32, (tm, tm), 0)
    t_j = lax.broadcasted_iota(I32, (tm, tm), 1)
    earlier = (t_j < t_i).astype(BF16)
    before = jnp.dot(earlier, multi.astype(BF16), preferred_element_type=F32) + base[...]
    k_lane = lax.broadcasted_iota(I32, (tm, TOP_K), 1)
    idx_out = jnp.zeros((tm, TOP_K), I32)
    gate_out = jnp.zeros((tm, TOP_K), F32)
    rank_out = jnp.zeros((tm, TOP_K), I32)
    for k in range(TOP_K):
        rank_k = jnp.sum(jnp.where(hot[k], before, 0.0), axis=-1, keepdims=True).astype(I32)
        idx_out = jnp.where(k_lane == k, sels[k].astype(I32), idx_out)
        gate_out = jnp.where(k_lane == k, exps[k] / denom, gate_out)
        rank_out = jnp.where(k_lane == k, rank_k, rank_out)
    idx_ref[...] = idx_out
    gate_ref[...] = gate_out
    rank_ref[...] = rank_out
    base[...] += jnp.sum(multi, axis=0, keepdims=True)
    cnt_ref[...] = base[...].astype(I32)


def _out_router(alpha, x2, ya, yb, ym, w_out_bf, norm_b_g, ln_g, ln_b, wr_hi, wr_lo, b_router):
    t, d = x2.shape
    tm = OUT_ROWS
    n_exp = b_router.shape[1]
    row = lambda w: pl.BlockSpec((tm, w), lambda i: (i, 0))
    return pl.pallas_call(
        functools.partial(_out_router_kernel, alpha),
        grid=(t // tm,),
        in_specs=[row(d), row(ya.shape[1]), row(yb.shape[1]), row(ym.shape[1])] + [_full_vmem()] * 7,
        out_specs=[row(d), row(TOP_K), row(TOP_K), row(TOP_K),
                   pl.BlockSpec((1, n_exp), lambda i: (0, 0))],
        out_shape=[jax.ShapeDtypeStruct((t, d), F32),
                   jax.ShapeDtypeStruct((t, TOP_K), I32),
                   jax.ShapeDtypeStruct((t, TOP_K), F32),
                   jax.ShapeDtypeStruct((t, TOP_K), I32),
                   jax.ShapeDtypeStruct((1, n_exp), I32)],
        scratch_shapes=[pltpu.VMEM((1, n_exp), F32)],
        compiler_params=_params(("arbitrary",)),
        name="out_router",
    )(x2, ya, yb, ym, w_out_bf, norm_b_g, ln_g, ln_b, wr_hi, wr_lo, b_router)


def _dispatch_kernel(bv_ref, dest_ref, x_ref, xs_hbm, zeros, sem, zsem):
    tm, d = x_ref.shape
    zrows = zeros.shape[0]
    per_block = EXPERT_ROWS // zrows
    n_blocks = xs_hbm.shape[0] // EXPERT_ROWS

    @pl.when(pl.program_id(0) == 0)
    def _():
        zeros[...] = jnp.zeros_like(zeros)

        def fill(i, j):
            return pltpu.make_async_copy(zeros, xs_hbm.at[pl.ds(i * EXPERT_ROWS + j * zrows, zrows), :], zsem)

        def start(i, c):
            @pl.when(bv_ref[i] < EXPERT_ROWS)
            def _():
                for j in range(per_block):
                    fill(i, j).start()
            return c

        def wait(i, c):
            @pl.when(bv_ref[i] < EXPERT_ROWS)
            def _():
                for j in range(per_block):
                    fill(i, j).wait()
            return c

        lax.fori_loop(0, n_blocks, start, 0)
        lax.fori_loop(0, n_blocks, wait, 0)

    def row_copy(t, slot):
        return pltpu.make_async_copy(x_ref.at[pl.ds(t, 1), :], xs_hbm.at[pl.ds(slot, 1), :], sem)

    def issue(t, c):
        for k in range(TOP_K):
            row_copy(t, dest_ref[t * TOP_K + k]).start()
        return c

    lax.fori_loop(0, tm, issue, 0, unroll=8)
    for _ in range(TOP_K):
        pltpu.make_async_copy(x_ref, xs_hbm.at[pl.ds(0, tm), :], sem).wait()


def _dispatch(x1, dest_flat, block_valid):
    t, d = x1.shape
    tm = DISPATCH_ROWS
    n_rows = block_valid.shape[0] * EXPERT_ROWS
    grid_spec = pltpu.PrefetchScalarGridSpec(
        num_scalar_prefetch=1,
        grid=(t // tm,),
        in_specs=[pl.BlockSpec((tm * TOP_K,), lambda i, bv: (i,), memory_space=pltpu.SMEM),
                  pl.BlockSpec((tm, d), lambda i, bv: (i, 0))],
        out_specs=pl.BlockSpec(memory_space=pl.ANY),
        scratch_shapes=[pltpu.VMEM((DISPATCH_ROWS, d), F32),
                        pltpu.SemaphoreType.DMA, pltpu.SemaphoreType.DMA],
    )
    return pl.pallas_call(
        _dispatch_kernel,
        grid_spec=grid_spec,
        out_shape=jax.ShapeDtypeStruct((n_rows, d), F32),
        compiler_params=_params(("arbitrary",)),
        name="dispatch",
    )(block_valid, dest_flat, x1)


def _experts_kernel(n_ff_steps, be_ref, na_ref, x_ref, wg_ref, wu_ref, bg_ref, bu_ref,
                    wd_ref, bd_ref, o_ref, xb, act):
    i = pl.program_id(0)
    s = pl.program_id(1)

    @pl.when(i >= na_ref[0])
    def _():
        o_ref[...] = jnp.zeros_like(o_ref)

    @pl.when(i < na_ref[0])
    def _():
        @pl.when(s == 0)
        def _():
            xb[...] = x_ref[...].astype(BF16)

        @pl.when(s < n_ff_steps)
        def _():
            x = xb[...]
            gate = jnp.dot(x, wg_ref[...].astype(BF16), preferred_element_type=F32) + bg_ref[...]
            up = jnp.dot(x, wu_ref[...].astype(BF16), preferred_element_type=F32) + bu_ref[...]
            gate = jnp.minimum(gate, SWIGLU_LIMIT)
            up = jnp.clip(up, -SWIGLU_LIMIT, SWIGLU_LIMIT)
            a = (up + 1.0) * (gate * jax.nn.sigmoid(SWIGLU_ALPHA * gate))
            act[s] = a.astype(BF16)

        @pl.when(s >= n_ff_steps)
        def _():
            a = jnp.concatenate([act[j] for j in range(n_ff_steps)], axis=1)
            o_ref[...] = jnp.dot(a, wd_ref[...].astype(BF16),
                                 preferred_element_type=F32) + bd_ref[...]


def _experts(xs, block_expert, n_active, w_gate_up, b_gate_up, w_down, b_down):
    n_rows, d = xs.shape
    n_exp, _, two_ff = w_gate_up.shape
    d_ff = two_ff // 2
    tm, tf, tn = EXPERT_ROWS, EXPERT_FF_TILE, EXPERT_OUT_TILE
    n_blocks = n_rows // tm
    nf, nn = d_ff // tf, d // tn
    bgu = b_gate_up.reshape(n_exp, 1, two_ff)
    bdn = b_down.reshape(n_exp, 1, d)

    def blk(i, na):
        return jnp.minimum(i, na[0] - 1)

    def ff_step(i, s, na):
        return jnp.where(i < na[0], jnp.minimum(s, nf - 1), nf - 1)

    def down_step(i, s, na):
        return jnp.where(i < na[0], jnp.maximum(s - nf, 0), nn - 1)

    in_specs = [
        pl.BlockSpec((tm, d), lambda i, s, be, na: (blk(i, na), 0)),
        pl.BlockSpec((None, d, tf), lambda i, s, be, na: (be[blk(i, na)], 0, ff_step(i, s, na))),
        pl.BlockSpec((None, d, tf), lambda i, s, be, na: (be[blk(i, na)], 0, nf + ff_step(i, s, na))),
        pl.BlockSpec((None, 1, tf), lambda i, s, be, na: (be[blk(i, na)], 0, ff_step(i, s, na))),
        pl.BlockSpec((None, 1, tf), lambda i, s, be, na: (be[blk(i, na)], 0, nf + ff_step(i, s, na))),
        pl.BlockSpec((None, d_ff, tn), lambda i, s, be, na: (be[blk(i, na)], 0, down_step(i, s, na))),
        pl.BlockSpec((None, 1, tn), lambda i, s, be, na: (be[blk(i, na)], 0, down_step(i, s, na))),
    ]
    grid_spec = pltpu.PrefetchScalarGridSpec(
        num_scalar_prefetch=2,
        grid=(n_blocks, nf + nn),
        in_specs=in_specs,
        out_specs=pl.BlockSpec((tm, tn), lambda i, s, be, na: (i, jnp.maximum(s - nf, 0))),
        scratch_shapes=[pltpu.VMEM((tm, d), BF16), pltpu.VMEM((nf, tm, tf), BF16)],
    )
    return pl.pallas_call(
        functools.partial(_experts_kernel, nf),
        grid_spec=grid_spec,
        out_shape=jax.ShapeDtypeStruct((n_rows, d), F32),
        compiler_params=_params(("arbitrary", "arbitrary")),
        name="experts",
    )(block_expert, n_active, xs, w_gate_up, w_gate_up, bgu, bgu, w_down, bdn)


def _combine_kernel(alpha, dest_ref, x1_ref, gate_ref, g_ref, b_ref, ys_hbm, o_ref, rows, sem):
    tm, d = x1_ref.shape

    def row_copy(k, t, slot):
        return pltpu.make_async_copy(ys_hbm.at[pl.ds(slot, 1), :], rows.at[k, pl.ds(t, 1), :], sem)

    def issue(t, c):
        for k in range(TOP_K):
            row_copy(k, t, dest_ref[t * TOP_K + k]).start()
        return c

    lax.fori_loop(0, tm, issue, 0, unroll=8)
    for k in range(TOP_K):
        pltpu.make_async_copy(ys_hbm.at[pl.ds(0, tm), :], rows.at[k], sem).wait()

    gates = gate_ref[...]
    ffn = gates[:, 0:1] * rows[0]
    for k in range(1, TOP_K):
        ffn += gates[:, k:k + 1] * rows[k]
    o_ref[...] = _layer_norm(alpha * x1_ref[...] + ffn, g_ref[...], b_ref[...])


def _combine(alpha, dest_flat, x1, gates, ln_g, ln_b, ys):
    t, d = x1.shape
    tm = COMBINE_ROWS
    return pl.pallas_call(
        functools.partial(_combine_kernel, alpha),
        grid=(t // tm,),
        in_specs=[pl.BlockSpec((tm * TOP_K,), lambda i: (i,), memory_space=pltpu.SMEM),
                  pl.BlockSpec((tm, d), lambda i: (i, 0)),
                  pl.BlockSpec((tm, TOP_K), lambda i: (i, 0)),
                  _full_vmem(), _full_vmem(),
                  pl.BlockSpec(memory_space=pl.ANY)],
        out_specs=pl.BlockSpec((tm, d), lambda i: (i, 0)),
        out_shape=jax.ShapeDtypeStruct((t, d), F32),
        scratch_shapes=[pltpu.VMEM((TOP_K, tm, d), F32), pltpu.SemaphoreType.DMA],
        compiler_params=_params(("arbitrary",)),
        name="combine",
    )(dest_flat, x1, gates, ln_g, ln_b, ys)


def _routing_tables(idx, rank, counts, n_blocks):
    tm = EXPERT_ROWS
    n_exp = counts.shape[0]
    padded = (counts + tm - 1) // tm * tm
    pad_end = jnp.cumsum(padded)
    pad_start = pad_end - padded
    dest = (pad_start[idx] + rank).reshape(-1).astype(I32)
    n_active = (pad_end[-1] // tm).astype(I32).reshape(1)
    first_row = jnp.arange(n_blocks, dtype=I32) * tm
    owner = jnp.sum((first_row[:, None] >= pad_end[None, :]).astype(I32), axis=1)
    block_expert = jnp.minimum(owner, n_exp - 1).astype(I32)
    block_valid = jnp.clip(counts[block_expert] - (first_row - pad_start[block_expert]), 0, tm).astype(I32)
    return dest, block_expert, block_valid, n_active


def _layer(alpha, x, mem, w_in, w_spatial, b_spatial, a_ln_g, a_ln_b, w_mem_kv, norm_a_g, norm_b_g,
           norm_m_g, w_out, ln1_g, ln1_b, w_router, b_router, w_gate_up, b_gate_up, w_down, b_down,
           ln2_g, ln2_b):
    batch, seq, d = x.shape
    t = batch * seq
    m_width = w_mem_kv.shape[1] // 2
    a_width = a_ln_g.shape[0]
    b_width = norm_b_g.shape[0]
    n_exp = w_router.shape[1]
    row = lambda v: v.reshape(1, -1)

    x2 = x.reshape(t, d)
    za, zq, zk, zv, zm = _in_proj(x2, w_in.astype(BF16), (2 * a_width, b_width, b_width, b_width, m_width))
    kv = _kv_proj(mem, w_mem_kv.astype(BF16))
    ya, ym = _local_mix(za, zm, kv, w_spatial, b_spatial.T, row(a_ln_g), row(a_ln_b),
                        row(norm_a_g), row(norm_m_g), batch, seq)
    yb = _dilated_attention(zq, zk, zv, batch, seq)

    wr_hi = w_router.astype(BF16)
    wr_lo = (w_router - wr_hi.astype(F32)).astype(BF16)
    x1, idx, gates, rank, counts = _out_router(
        alpha, x2, ya, yb, ym, w_out.astype(BF16), row(norm_b_g), row(ln1_g), row(ln1_b),
        wr_hi, wr_lo, row(b_router))

    tm = EXPERT_ROWS
    n_blocks = (t * TOP_K + n_exp * (tm - 1)) // tm
    dest, block_expert, block_valid, n_active = _routing_tables(idx, rank, counts.reshape(n_exp), n_blocks)
    xs = _dispatch(x1, dest, block_valid)
    ys = _experts(xs, block_expert, n_active, w_gate_up, b_gate_up, w_down, b_down)
    out = _combine(alpha, dest, x1, gates, row(ln2_g), row(ln2_b), ys)
    return out.reshape(batch, seq, d)


def kernel(x, mem, w_in, w_spatial, b_spatial, a_ln_g, a_ln_b, w_mem_kv, norm_a_g, norm_b_g, norm_m_g, w_out, ln1_g, ln1_b, w_router, b_router, w_gate_up, b_gate_up, w_down, b_down, ln2_g, ln2_b):
    depth = w_in.shape[0]
    alpha = (2.0 * depth) ** 0.25
    for l in range(depth):
        x = _layer(alpha, x, mem, w_in[l], w_spatial[l], b_spatial[l], a_ln_g[l], a_ln_b[l], w_mem_kv[l],
                   norm_a_g[l], norm_b_g[l], norm_m_g[l], w_out[l], ln1_g[l], ln1_b[l], w_router[l],
                   b_router[l], w_gate_up[l], b_gate_up[l], w_down[l], b_down[l], ln2_g[l], ln2_b[l])
    return x
```

```python
import functools

import jax
import jax.numpy as jnp
from jax import lax
from jax.experimental import pallas as pl
from jax.experimental.pallas import tpu as pltpu

F32 = jnp.float32
BF16 = jnp.bfloat16
I32 = jnp.int32
U32 = jnp.uint32

HEAD_DIM = 128
CHUNK = 128
ATT_BLOCK = 128
DILATIONS = (1, 4, 16)
TOP_K = 4
SWIGLU_LIMIT = 7.0
SWIGLU_ALPHA = 1.702
LN_EPS = 1e-5
NEG_INF = -1e30

V7X_VMEM_LIMIT_BYTES = 56 * 1024 * 1024

IN_PROJ_ROWS = 256
LOCAL_ROWS = 512
OUT_ROWS = 256
DISPATCH_ROWS = 256
COMBINE_ROWS = 256
DILATED_FILL_ROWS = 256
DILATED_GROUP = 16
EXPERT_ROWS = 3072
EXPERT_SUB = 256
EXPERT_FF_TILE = 256
EXPERT_OUT_TILE = 256


def _params(semantics):
    return pltpu.CompilerParams(dimension_semantics=semantics,
                                vmem_limit_bytes=V7X_VMEM_LIMIT_BYTES)


def _full_vmem():
    return pl.BlockSpec(memory_space=pltpu.VMEM)


def _layer_norm(h, g, b):
    mu = jnp.mean(h, axis=-1, keepdims=True)
    hc = h - mu
    var = jnp.mean(hc * hc, axis=-1, keepdims=True)
    return hc * lax.rsqrt(var + LN_EPS) * g + b


def _rms_norm(y, g):
    return y * lax.rsqrt(jnp.mean(y * y, axis=-1, keepdims=True) + LN_EPS) * g


def _gelu_tanh(x):
    c = 0.7978845608028654
    return x * (0.5 * (1.0 + jnp.tanh(c * (x + 0.044715 * (x * x * x)))))


def _dot_nt(a, b):
    return lax.dot_general(a, b, (((1,), (1,)), ((), ())), preferred_element_type=F32)


def _in_proj_kernel(bounds, x_ref, w_ref, *out_refs):
    xb = x_ref[...].astype(BF16)
    for o_ref, (lo, hi) in zip(out_refs, bounds):
        o_ref[...] = jnp.dot(xb, w_ref[:, lo:hi], preferred_element_type=F32)


def _in_proj(x2, w_in_bf, widths):
    t, d = x2.shape
    tm = IN_PROJ_ROWS
    bounds, lo = [], 0
    for w in widths:
        bounds.append((lo, lo + w))
        lo += w
    return pl.pallas_call(
        functools.partial(_in_proj_kernel, tuple(bounds)),
        grid=(t // tm,),
        in_specs=[pl.BlockSpec((tm, d), lambda i: (i, 0)), _full_vmem()],
        out_specs=[pl.BlockSpec((tm, w), lambda i: (i, 0)) for w in widths],
        out_shape=[jax.ShapeDtypeStruct((t, w), F32) for w in widths],
        compiler_params=_params(("arbitrary",)),
        name="in_proj",
    )(x2, w_in_bf)


def _kv_proj_kernel(mem_ref, w_ref, kv_ref):
    kv_ref[...] = jnp.dot(mem_ref[...].astype(BF16), w_ref[...],
                          preferred_element_type=F32).astype(BF16)


def _kv_proj(mem, w_kv_bf):
    b, n_mem, d = mem.shape
    n = w_kv_bf.shape[1]
    return pl.pallas_call(
        _kv_proj_kernel,
        grid=(b,),
        in_specs=[pl.BlockSpec((None, n_mem, d), lambda i: (i, 0, 0)), _full_vmem()],
        out_specs=pl.BlockSpec((None, n_mem, n), lambda i: (i, 0, 0)),
        out_shape=jax.ShapeDtypeStruct((b, n_mem, n), BF16),
        compiler_params=_params(("arbitrary",)),
        name="kv_proj",
    )(mem, w_kv_bf)


def _local_mix_kernel(za_ref, zm_ref, kv_ref, wsp_ref, bsp_ref, lng_ref, lnb_ref, nag_ref, nmg_ref,
                      ya_ref, ym_ref, ya_acc, ym_acc):
    rows = za_ref.shape[0]
    a_width = za_ref.shape[1] // 2
    n_groups = a_width // HEAD_DIM
    n_chunks = rows // CHUNK
    m_width = zm_ref.shape[1]
    m_heads = m_width // HEAD_DIM

    act = _gelu_tanh(za_ref[...])
    u = act[:, :a_width]
    v = _layer_norm(act[:, a_width:], lng_ref[...], lnb_ref[...])
    t_idx = lax.broadcasted_iota(I32, (CHUNK, CHUNK), 0)
    s_idx = lax.broadcasted_iota(I32, (CHUNK, CHUNK), 1)
    causal = s_idx <= t_idx
    bsp = bsp_ref[...]
    for g in range(n_groups):
        cols = slice(g * HEAD_DIM, (g + 1) * HEAD_DIM)
        w_g = jnp.where(causal, wsp_ref[g], 0.0).astype(BF16)
        v_g = jnp.concatenate([v[c * CHUNK:(c + 1) * CHUNK, cols] for c in range(n_chunks)], axis=1)
        mixed = jnp.dot(w_g, v_g.astype(BF16), preferred_element_type=F32) + bsp[:, g:g + 1]
        for c in range(n_chunks):
            rs = slice(c * CHUNK, (c + 1) * CHUNK)
            ya_acc[rs, cols] = u[rs, cols] * mixed[:, c * HEAD_DIM:(c + 1) * HEAD_DIM]
    ya_ref[...] = _rms_norm(ya_acc[...], nag_ref[...]).astype(BF16)

    scale = HEAD_DIM ** -0.5
    q = zm_ref[...]
    for h in range(m_heads):
        cols = slice(h * HEAD_DIM, (h + 1) * HEAD_DIM)
        k_h = kv_ref[:, h * HEAD_DIM:(h + 1) * HEAD_DIM]
        v_h = kv_ref[:, m_width + h * HEAD_DIM:m_width + (h + 1) * HEAD_DIM]
        s = _dot_nt(q[:, cols].astype(BF16), k_h) * scale
        p = jnp.exp(s - jnp.max(s, axis=-1, keepdims=True))
        l = jnp.sum(p, axis=-1, keepdims=True)
        ym_acc[:, cols] = jnp.dot(p.astype(BF16), v_h, preferred_element_type=F32) / l
    ym_ref[...] = _rms_norm(ym_acc[...], nmg_ref[...]).astype(BF16)


def _local_mix(za, zm, kv, w_spatial, b_spatial_t, a_ln_g, a_ln_b, norm_a_g, norm_m_g, batch, seq):
    rows = LOCAL_ROWS
    a2 = za.shape[1]
    a_width = a2 // 2
    m_width = zm.shape[1]
    n_mem, kv_w = kv.shape[1], kv.shape[2]
    tiles = seq // rows
    return pl.pallas_call(
        _local_mix_kernel,
        grid=(batch, tiles),
        in_specs=[
            pl.BlockSpec((rows, a2), lambda b, i: (b * tiles + i, 0)),
            pl.BlockSpec((rows, m_width), lambda b, i: (b * tiles + i, 0)),
            pl.BlockSpec((None, n_mem, kv_w), lambda b, i: (b, 0, 0)),
            _full_vmem(), _full_vmem(), _full_vmem(), _full_vmem(), _full_vmem(), _full_vmem(),
        ],
        out_specs=[pl.BlockSpec((rows, a_width), lambda b, i: (b * tiles + i, 0)),
                   pl.BlockSpec((rows, m_width), lambda b, i: (b * tiles + i, 0))],
        out_shape=[jax.ShapeDtypeStruct((batch * seq, a_width), BF16),
                   jax.ShapeDtypeStruct((batch * seq, m_width), BF16)],
        scratch_shapes=[pltpu.VMEM((rows, a_width), F32), pltpu.VMEM((rows, m_width), F32)],
        compiler_params=_params(("arbitrary", "arbitrary")),
        name="local_mix",
    )(za, zm, kv, w_spatial, b_spatial_t, a_ln_g, a_ln_b, norm_a_g, norm_m_g)


def _dilated_kernel(q_ref, k_ref, v_ref, o_ref, qs, ks, vs, *state):
    seq = q_ref.shape[0]
    scale = HEAD_DIM ** -0.5
    blk = ATT_BLOCK
    n_pat = len(DILATIONS)
    m_s, l_s, a_s = state[:n_pat], state[n_pat:2 * n_pat], state[2 * n_pat:]
    qi = lax.broadcasted_iota(I32, (blk, 2 * blk), 0)
    kj = lax.broadcasted_iota(I32, (blk, 2 * blk), 1)
    band = (kj >= qi) & (kj <= qi + blk)
    zero_blk = jnp.zeros((blk, HEAD_DIM), BF16)
    vs[:, HEAD_DIM:] = jnp.ones((vs.shape[0], HEAD_DIM), BF16)

    for p, d in enumerate(DILATIONS):
        sub = seq // d
        n_blocks = sub // blk
        chunk = min(sub, DILATED_FILL_ROWS)

        def fill(i, c, d=d, sub=sub, chunk=chunk):
            r = i // (sub // chunk)
            j = i % (sub // chunk)
            first = r + j * (chunk * d)
            src = pl.ds(first, chunk) if d == 1 else pl.ds(first, chunk, stride=d)
            dst_q = pl.multiple_of(r * sub + j * chunk, blk)
            dst_kv = pl.multiple_of(r * (sub + blk) + blk + j * chunk, blk)
            qs[pl.ds(dst_q, chunk), :] = q_ref[src, :].astype(BF16)
            ks[pl.ds(dst_kv, chunk), :] = k_ref[src, :].astype(BF16)
            vs[pl.ds(dst_kv, chunk), :HEAD_DIM] = v_ref[src, :].astype(BF16)
            return c

        def pad(r, c, sub=sub):
            at = pl.multiple_of(r * (sub + blk), blk)
            ks[pl.ds(at, blk), :] = zero_blk
            vs[pl.ds(at, blk), :HEAD_DIM] = zero_blk
            return c

        lax.fori_loop(0, d, pad, 0)
        lax.fori_loop(0, d * (sub // chunk), fill, 0)

        def block(c, d=d, n_blocks=n_blocks, p=p):
            r = c // n_blocks
            n = c % n_blocks
            q = qs[pl.ds(pl.multiple_of(c * blk, blk), blk), :]
            win = pl.ds(pl.multiple_of((c + r) * blk, blk), 2 * blk)
            s = _dot_nt(q, ks[win, :]) * scale
            valid = band & (kj >= jnp.where(n > 0, 0, blk))
            s = jnp.where(valid, s, NEG_INF)
            m = jnp.max(s, axis=-1, keepdims=True)
            e = jnp.exp(s - m)
            acc = jnp.dot(e.astype(BF16), vs[win, :], preferred_element_type=F32)
            first = n * (blk * d) + r
            out = pl.ds(first, blk) if d == 1 else pl.ds(first, blk, stride=d)
            m_s[p][out, :] = jnp.broadcast_to(m, (blk, HEAD_DIM))
            l_s[p][out, :] = acc[:, HEAD_DIM:]
            a_s[p][out, :] = acc[:, :HEAD_DIM]

        def group(i, c, block=block):
            for g in range(DILATED_GROUP):
                block(i * DILATED_GROUP + g)
            return c

        lax.fori_loop(0, seq // blk // DILATED_GROUP, group, 0)

    def merge(j, c):
        rows = pl.ds(pl.multiple_of(j * blk, blk), blk)
        ms = [m[rows, :] for m in m_s]
        m_all = functools.reduce(jnp.maximum, ms)
        ws = [jnp.exp(m - m_all) for m in ms]
        den = sum(w * l[rows, :] for w, l in zip(ws, l_s))
        num = sum(w * a[rows, :] for w, a in zip(ws, a_s))
        o_ref[rows, :] = num / den
        return c

    lax.fori_loop(0, seq // blk, merge, 0)


def _dilated_attention(zq, zk, zv, batch, seq):
    width = zq.shape[1]
    heads = width // HEAD_DIM
    spec = pl.BlockSpec((seq, HEAD_DIM), lambda b, h: (b, h))
    kv_rows = seq + max(DILATIONS) * ATT_BLOCK
    return pl.pallas_call(
        _dilated_kernel,
        grid=(batch, heads),
        in_specs=[spec, spec, spec],
        out_specs=spec,
        out_shape=jax.ShapeDtypeStruct((batch * seq, width), F32),
        scratch_shapes=[pltpu.VMEM((seq, HEAD_DIM), BF16), pltpu.VMEM((kv_rows, HEAD_DIM), BF16),
                        pltpu.VMEM((kv_rows, 2 * HEAD_DIM), BF16)]
                       + [pltpu.VMEM((seq, HEAD_DIM), F32)] * (3 * len(DILATIONS)),
        compiler_params=_params(("arbitrary", "arbitrary")),
        name="dilated",
    )(zq, zk, zv)


def _out_router_kernel(alpha, x_ref, ya_ref, yb_ref, ym_ref, wo_ref, nbg_ref, g_ref, b_ref,
                       wrh_ref, wrl_ref, br_ref,
                       x1_ref, idx_ref, gate_ref, rank_ref, cnt_ref, base):
    step = pl.program_id(0)
    tm = x_ref.shape[0]
    a_w = ya_ref.shape[1]
    b_w = yb_ref.shape[1]
    n_exp = br_ref.shape[1]

    @pl.when(step == 0)
    def _():
        base[...] = jnp.zeros_like(base)

    yb = _rms_norm(yb_ref[...], nbg_ref[...]).astype(BF16)
    mix = jnp.dot(ya_ref[...], wo_ref[0:a_w, :], preferred_element_type=F32)
    mix += jnp.dot(yb, wo_ref[a_w:a_w + b_w, :], preferred_element_type=F32)
    mix += jnp.dot(ym_ref[...], wo_ref[a_w + b_w:, :], preferred_element_type=F32)
    x1 = _layer_norm(alpha * x_ref[...] + mix, g_ref[...], b_ref[...])
    x1_ref[...] = x1

    x_hi = x1.astype(BF16)
    x_lo = (x1 - x_hi.astype(F32)).astype(BF16)
    logits = (jnp.dot(x_hi, wrh_ref[...], preferred_element_type=F32)
              + jnp.dot(x_lo, wrh_ref[...], preferred_element_type=F32)
              + jnp.dot(x_hi, wrl_ref[...], preferred_element_type=F32)) + br_ref[...]

    lane = lax.broadcasted_iota(I32, (tm, n_exp), 1).astype(F32)
    work = logits
    vals, sels = [], []
    for _ in range(TOP_K):
        m = jnp.max(work, axis=-1, keepdims=True)
        sel = jnp.min(jnp.where(work == m, lane, float(n_exp)), axis=-1, keepdims=True)
        vals.append(m)
        sels.append(sel)
        work = jnp.where(lane == sel, -jnp.inf, work)
    exps = [jnp.exp(v - vals[0]) for v in vals]
    denom = exps[0] + exps[1] + exps[2] + exps[3]

    hot = [lane == s for s in sels]
    multi = sum(h.astype(F32) for h in hot)
    t_i = lax.broadcasted_iota(I32, (tm, tm), 0)
    t_j = lax.broadcasted_iota(I32, (tm, tm), 1)
    earlier = (t_j < t_i).astype(BF16)
    before = jnp.dot(earlier, multi.astype(BF16), preferred_element_type=F32) + base[...]
    k_lane = lax.broadcasted_iota(I32, (tm, TOP_K), 1)
    idx_out = jnp.zeros((tm, TOP_K), I32)
    gate_out = jnp.zeros((tm, TOP_K), F32)
    rank_out = jnp.zeros((tm, TOP_K), I32)
    for k in range(TOP_K):
        rank_k = jnp.sum(jnp.where(hot[k], before, 0.0), axis=-1, keepdims=True).astype(I32)
        idx_out = jnp.where(k_lane == k, sels[k].astype(I32), idx_out)
        gate_out = jnp.where(k_lane == k, exps[k] / denom, gate_out)
        rank_out = jnp.where(k_lane == k, rank_k, rank_out)
    idx_ref[...] = idx_out
    gate_ref[...] = gate_out
    rank_ref[...] = rank_out
    base[...] += jnp.sum(multi, axis=0, keepdims=True)
    cnt_ref[...] = base[...].astype(I32)


def _out_router(alpha, x2, ya, yb, ym, w_out_bf, norm_b_g, ln_g, ln_b, wr_hi, wr_lo, b_router):
    t, d = x2.shape
    tm = OUT_ROWS
    n_exp = b_router.shape[1]
    row = lambda w: pl.BlockSpec((tm, w), lambda i: (i, 0))
    return pl.pallas_call(
        functools.partial(_out_router_kernel, alpha),
        grid=(t // tm,),
        in_specs=[row(d), row(ya.shape[1]), row(yb.shape[1]), row(ym.shape[1])] + [_full_vmem()] * 7,
        out_specs=[row(d), row(TOP_K), row(TOP_K), row(TOP_K),
                   pl.BlockSpec((1, n_exp), lambda i: (0, 0))],
        out_shape=[jax.ShapeDtypeStruct((t, d), F32),
                   jax.ShapeDtypeStruct((t, TOP_K), I32),
                   jax.ShapeDtypeStruct((t, TOP_K), F32),
                   jax.ShapeDtypeStruct((t, TOP_K), I32),
                   jax.ShapeDtypeStruct((1, n_exp), I32)],
        scratch_shapes=[pltpu.VMEM((1, n_exp), F32)],
        compiler_params=_params(("arbitrary",)),
        name="out_router",
    )(x2, ya, yb, ym, w_out_bf, norm_b_g, ln_g, ln_b, wr_hi, wr_lo, b_router)


def _dispatch_kernel(ztile_ref, total_ref, dest_ref, x_ref, xs_hbm, zeros, sem, zsem):
    tm, d = x_ref.shape
    sub = zeros.shape[0]
    n_tiles = xs_hbm.shape[0] // sub

    @pl.when(pl.program_id(0) == 0)
    def _():
        zeros[...] = jnp.zeros_like(zeros)

        def fill(first_row):
            return pltpu.make_async_copy(zeros, xs_hbm.at[pl.ds(pl.multiple_of(first_row, sub), sub), :], zsem)

        def partial_tiles(run):
            def step(e, c):
                @pl.when(ztile_ref[e] >= 0)
                def _():
                    run(fill(ztile_ref[e]))
                return c
            lax.fori_loop(0, ztile_ref.shape[0], step, 0)

        def tail_tiles(run):
            def step(j, c):
                run(fill(j * sub))
                return c
            lax.fori_loop(total_ref[0] // sub, n_tiles, step, 0)

        partial_tiles(lambda copy: copy.start())
        tail_tiles(lambda copy: copy.start())
        partial_tiles(lambda copy: copy.wait())
        tail_tiles(lambda copy: copy.wait())

    def row_copy(t, slot):
        return pltpu.make_async_copy(x_ref.at[pl.ds(t, 1), :], xs_hbm.at[pl.ds(slot, 1), :], sem)

    def issue(t, c):
        for k in range(TOP_K):
            row_copy(t, dest_ref[t * TOP_K + k]).start()
        return c

    lax.fori_loop(0, tm, issue, 0, unroll=8)
    for _ in range(TOP_K):
        pltpu.make_async_copy(x_ref, xs_hbm.at[pl.ds(0, tm), :], sem).wait()


def _dispatch(x1, dest_flat, ztile, total_rows, n_rows):
    t, d = x1.shape
    tm = DISPATCH_ROWS
    grid_spec = pltpu.PrefetchScalarGridSpec(
        num_scalar_prefetch=2,
        grid=(t // tm,),
        in_specs=[pl.BlockSpec((tm * TOP_K,), lambda i, z, n: (i,), memory_space=pltpu.SMEM),
                  pl.BlockSpec((tm, d), lambda i, z, n: (i, 0))],
        out_specs=pl.BlockSpec(memory_space=pl.ANY),
        scratch_shapes=[pltpu.VMEM((EXPERT_SUB, d), F32),
                        pltpu.SemaphoreType.DMA, pltpu.SemaphoreType.DMA],
    )
    return pl.pallas_call(
        _dispatch_kernel,
        grid_spec=grid_spec,
        out_shape=jax.ShapeDtypeStruct((n_rows, d), F32),
        compiler_params=_params(("arbitrary",)),
        name="dispatch",
    )(ztile, total_rows, dest_flat, x1)


def _experts_kernel(nf, sbe_ref, sbr_ref, sbf_ref, na_ref, xs_hbm, wg_ref, wu_ref, bg_ref, bu_ref,
                    wd_ref, bd_ref, o_ref, xb, act, stage, wgu, wdn, sem):
    g = pl.program_id(0)
    s = pl.program_id(1)
    nn = pl.num_programs(1) - nf
    rows_max = xb.shape[0]
    sub = stage.shape[1]
    n_chunks = rows_max // sub
    n_sub = sbr_ref[g] // sub
    nxt = jnp.minimum(g + 1, pl.num_programs(0) - 1)
    n_next = jnp.where(g + 1 < pl.num_programs(0), sbr_ref[nxt], 0) // sub

    def chunk_copy(sb, c, slot):
        first = pl.multiple_of(sbf_ref[sb] + c * sub, sub)
        return pltpu.make_async_copy(xs_hbm.at[pl.ds(first, sub), :], stage.at[slot], sem.at[slot])

    def land(c, slot):
        chunk_copy(0, 0, slot).wait()
        xb[pl.ds(pl.multiple_of(c * sub, sub), sub), :] = stage[slot].astype(BF16)

    def row_tiles(body):
        n_big = n_sub // 4

        def big(j, c):
            body(pl.multiple_of(j * (4 * sub), 4 * sub), 4 * sub)
            return c

        lax.fori_loop(0, n_big, big, 0)

        @pl.when(n_sub % 4 >= 2)
        def _():
            body(pl.multiple_of(n_big * (4 * sub), 2 * sub), 2 * sub)

        @pl.when(n_sub % 2 == 1)
        def _():
            body(pl.multiple_of((n_sub - 1) * sub, sub), sub)

    @pl.when(g >= na_ref[0])
    def _():
        o_ref[...] = jnp.zeros_like(o_ref)

    @pl.when(g < na_ref[0])
    def _():
        @pl.when((g == 0) & (s == 0))
        def _():
            def fetch(c, carry):
                chunk_copy(0, c, 0).start()
                land(c, 0)
                return carry
            lax.fori_loop(0, n_sub, fetch, 0)

        for slot in range(2):
            @pl.when((g > 0) & (s == 0) & (nn - 1 + slot * nn < n_sub))
            def _(slot=slot):
                land(nn - 1 + slot * nn, slot)

        @pl.when(s < nf)
        def _():
            wgu[0] = wg_ref[...].astype(BF16)
            wgu[1] = wu_ref[...].astype(BF16)

            def body(first, rows):
                x = xb[pl.ds(first, rows), :]
                gate = jnp.dot(x, wgu[0], preferred_element_type=F32) + bg_ref[...]
                up = jnp.dot(x, wgu[1], preferred_element_type=F32) + bu_ref[...]
                gate = jnp.minimum(gate, SWIGLU_LIMIT)
                up = jnp.clip(up, -SWIGLU_LIMIT, SWIGLU_LIMIT)
                a = (up + 1.0) * (gate * jax.nn.sigmoid(SWIGLU_ALPHA * gate))
                act[s, pl.ds(first, rows), :] = a.astype(BF16)

            row_tiles(body)

        @pl.when(s >= nf)
        def _():
            k = s - nf
            for slot in range(2):
                @pl.when((k >= 1) & (k - 1 + slot * nn < n_next))
                def _(slot=slot):
                    land(k - 1 + slot * nn, slot)

                @pl.when(k + slot * nn < n_next)
                def _(slot=slot):
                    chunk_copy(nxt, k + slot * nn, slot).start()

            wdn[...] = wd_ref[...].astype(BF16)

            def body(first, rows):
                a = jnp.concatenate([act[j, pl.ds(first, rows), :] for j in range(nf)], axis=1)
                o_ref[pl.ds(first, rows), :] = jnp.dot(a, wdn[...], preferred_element_type=F32) + bd_ref[...]

            row_tiles(body)

            def clear(c, carry):
                o_ref[pl.ds(pl.multiple_of(c * sub, sub), sub), :] = jnp.zeros((sub, o_ref.shape[1]), F32)
                return carry
            lax.fori_loop(n_sub, n_chunks, clear, 0)


def _experts(xs, sb_expert, sb_rows, sb_first, n_active, n_super, w_gate_up, b_gate_up, w_down, b_down):
    d = xs.shape[1]
    n_exp, _, two_ff = w_gate_up.shape
    d_ff = two_ff // 2
    tm, tf, tn, sub = EXPERT_ROWS, EXPERT_FF_TILE, EXPERT_OUT_TILE, EXPERT_SUB
    nf, nn = d_ff // tf, d // tn
    assert 2 * nn >= tm // sub, "two row chunks per down-projection step must cover a super-block"
    bgu = b_gate_up.reshape(n_exp, 1, two_ff)
    bdn = b_down.reshape(n_exp, 1, d)

    def expert(g, sbe, na):
        return sbe[jnp.minimum(g, na[0] - 1)]

    def ff_step(g, s, na):
        return jnp.where(g < na[0], jnp.minimum(s, nf - 1), nf - 1)

    def down_step(g, s, na):
        return jnp.where(g < na[0], jnp.maximum(s - nf, 0), nn - 1)

    in_specs = [
        pl.BlockSpec(memory_space=pl.ANY),
        pl.BlockSpec((None, d, tf), lambda g, s, sbe, sbr, sbf, na: (expert(g, sbe, na), 0, ff_step(g, s, na))),
        pl.BlockSpec((None, d, tf), lambda g, s, sbe, sbr, sbf, na: (expert(g, sbe, na), 0, nf + ff_step(g, s, na))),
        pl.BlockSpec((None, 1, tf), lambda g, s, sbe, sbr, sbf, na: (expert(g, sbe, na), 0, ff_step(g, s, na))),
        pl.BlockSpec((None, 1, tf), lambda g, s, sbe, sbr, sbf, na: (expert(g, sbe, na), 0, nf + ff_step(g, s, na))),
        pl.BlockSpec((None, d_ff, tn), lambda g, s, sbe, sbr, sbf, na: (expert(g, sbe, na), 0, down_step(g, s, na))),
        pl.BlockSpec((None, 1, tn), lambda g, s, sbe, sbr, sbf, na: (expert(g, sbe, na), 0, down_step(g, s, na))),
    ]
    grid_spec = pltpu.PrefetchScalarGridSpec(
        num_scalar_prefetch=4,
        grid=(n_super, nf + nn),
        in_specs=in_specs,
        out_specs=pl.BlockSpec((tm, tn), lambda g, s, sbe, sbr, sbf, na: (g, jnp.maximum(s - nf, 0))),
        scratch_shapes=[pltpu.VMEM((tm, d), BF16), pltpu.VMEM((nf, tm, tf), BF16),
                        pltpu.VMEM((2, sub, d), F32), pltpu.VMEM((2, d, tf), BF16),
                        pltpu.VMEM((d_ff, tn), BF16), pltpu.SemaphoreType.DMA((2,))],
    )
    return pl.pallas_call(
        functools.partial(_experts_kernel, nf),
        grid_spec=grid_spec,
        out_shape=jax.ShapeDtypeStruct((n_super * tm, d), F32),
        compiler_params=_params(("arbitrary", "arbitrary")),
        name="experts",
    )(sb_expert, sb_rows, sb_first, n_active, xs, w_gate_up, w_gate_up, bgu, bgu, w_down, bdn)


def _combine_kernel(alpha, dest_ref, x1_ref, gate_ref, g_ref, b_ref, ys_hbm, o_ref, rows, sem):
    tm, d = x1_ref.shape

    def row_copy(k, t, slot):
        return pltpu.make_async_copy(ys_hbm.at[pl.ds(slot, 1), :], rows.at[k, pl.ds(t, 1), :], sem)

    def issue(t, c):
        for k in range(TOP_K):
            row_copy(k, t, dest_ref[t * TOP_K + k]).start()
        return c

    lax.fori_loop(0, tm, issue, 0, unroll=8)
    for k in range(TOP_K):
        pltpu.make_async_copy(ys_hbm.at[pl.ds(0, tm), :], rows.at[k], sem).wait()

    gates = gate_ref[...]
    ffn = gates[:, 0:1] * rows[0]
    for k in range(1, TOP_K):
        ffn += gates[:, k:k + 1] * rows[k]
    o_ref[...] = _layer_norm(alpha * x1_ref[...] + ffn, g_ref[...], b_ref[...])


def _combine(alpha, dest_flat, x1, gates, ln_g, ln_b, ys):
    t, d = x1.shape
    tm = COMBINE_ROWS
    return pl.pallas_call(
        functools.partial(_combine_kernel, alpha),
        grid=(t // tm,),
        in_specs=[pl.BlockSpec((tm * TOP_K,), lambda i: (i,), memory_space=pltpu.SMEM),
                  pl.BlockSpec((tm, d), lambda i: (i, 0)),
                  pl.BlockSpec((tm, TOP_K), lambda i: (i, 0)),
                  _full_vmem(), _full_vmem(),
                  pl.BlockSpec(memory_space=pl.ANY)],
        out_specs=pl.BlockSpec((tm, d), lambda i: (i, 0)),
        out_shape=jax.ShapeDtypeStruct((t, d), F32),
        scratch_shapes=[pltpu.VMEM((TOP_K, tm, d), F32), pltpu.SemaphoreType.DMA],
        compiler_params=_params(("arbitrary",)),
        name="combine",
    )(dest_flat, x1, gates, ln_g, ln_b, ys)


def _routing_tables(idx, rank, counts, n_super):
    tm, sub = EXPERT_ROWS, EXPERT_SUB
    n_exp = counts.shape[0]
    padded = (counts + sub - 1) // sub * sub
    row_end = jnp.cumsum(padded)
    row_start = row_end - padded
    per_expert = (padded + tm - 1) // tm
    sb_end = jnp.cumsum(per_expert)
    sb_start = sb_end - per_expert
    dest_x = (row_start[idx] + rank).reshape(-1).astype(I32)
    dest_y = ((sb_start[idx] + rank // tm) * tm + rank % tm).reshape(-1).astype(I32)
    ztile = jnp.where(counts % sub != 0, row_start + counts // sub * sub, -1).astype(I32)
    g = jnp.arange(n_super, dtype=I32)
    owner = jnp.sum((g[:, None] >= sb_end[None, :]).astype(I32), axis=1)
    sb_expert = jnp.minimum(owner, n_exp - 1).astype(I32)
    local = (g - sb_start[sb_expert]) * tm
    active = g < sb_end[-1]
    sb_rows = jnp.where(active, jnp.clip(padded[sb_expert] - local, 0, tm), 0).astype(I32)
    sb_first = jnp.where(active, row_start[sb_expert] + local, 0).astype(I32)
    return (dest_x, dest_y, ztile, row_end[-1:].astype(I32), sb_expert, sb_rows, sb_first,
            sb_end[-1:].astype(I32))


def _layer(alpha, x, mem, w_in, w_spatial, b_spatial, a_ln_g, a_ln_b, w_mem_kv, norm_a_g, norm_b_g,
           norm_m_g, w_out, ln1_g, ln1_b, w_router, b_router, w_gate_up, b_gate_up, w_down, b_down,
           ln2_g, ln2_b):
    batch, seq, d = x.shape
    t = batch * seq
    m_width = w_mem_kv.shape[1] // 2
    a_width = a_ln_g.shape[0]
    b_width = norm_b_g.shape[0]
    n_exp = w_router.shape[1]
    row = lambda v: v.reshape(1, -1)

    x2 = x.reshape(t, d)
    za, zq, zk, zv, zm = _in_proj(x2, w_in.astype(BF16), (2 * a_width, b_width, b_width, b_width, m_width))
    kv = _kv_proj(mem, w_mem_kv.astype(BF16))
    ya, ym = _local_mix(za, zm, kv, w_spatial, b_spatial.T, row(a_ln_g), row(a_ln_b),
                        row(norm_a_g), row(norm_m_g), batch, seq)
    yb = _dilated_attention(zq, zk, zv, batch, seq)

    wr_hi = w_router.astype(BF16)
    wr_lo = (w_router - wr_hi.astype(F32)).astype(BF16)
    x1, idx, gates, rank, counts = _out_router(
        alpha, x2, ya, yb, ym, w_out.astype(BF16), row(norm_b_g), row(ln1_g), row(ln1_b),
        wr_hi, wr_lo, row(b_router))

    n_assign = t * TOP_K
    n_super = (n_assign + n_exp * (EXPERT_ROWS - 1)) // EXPERT_ROWS
    n_x_rows = (n_assign + n_exp * (EXPERT_SUB - 1)) // EXPERT_SUB * EXPERT_SUB
    dest_x, dest_y, ztile, total_rows, sb_expert, sb_rows, sb_first, n_active = _routing_tables(
        idx, rank, counts.reshape(n_exp), n_super)
    xs = _dispatch(x1, dest_x, ztile, total_rows, n_x_rows)
    ys = _experts(xs, sb_expert, sb_rows, sb_first, n_active, n_super, w_gate_up, b_gate_up, w_down, b_down)
    out = _combine(alpha, dest_y, x1, gates, row(ln2_g), row(ln2_b), ys)
    return out.reshape(batch, seq, d)


def kernel(x, mem, w_in, w_spatial, b_spatial, a_ln_g, a_ln_b, w_mem_kv, norm_a_g, norm_b_g, norm_m_g, w_out, ln1_g, ln1_b, w_router, b_router, w_gate_up, b_gate_up, w_down, b_down, ln2_g, ln2_b):
    depth = w_in.shape[0]
    alpha = (2.0 * depth) ** 0.25
    for l in range(depth):
        x = _layer(alpha, x, mem, w_in[l], w_spatial[l], b_spatial[l], a_ln_g[l], a_ln_b[l], w_mem_kv[l],
                   norm_a_g[l], norm_b_g[l], norm_m_g[l], w_out[l], ln1_g[l], ln1_b[l], w_router[l],
                   b_router[l], w_gate_up[l], b_gate_up[l], w_down[l], b_down[l], ln2_g[l], ln2_b[l])
    return x
```

```python
import functools

import jax
import jax.numpy as jnp
from jax import lax
from jax.experimental import pallas as pl
from jax.experimental.pallas import tpu as pltpu

F32 = jnp.float32
BF16 = jnp.bfloat16
I32 = jnp.int32
U32 = jnp.uint32

HEAD_DIM = 128
CHUNK = 128
ATT_BLOCK = 128
DILATIONS = (1, 4, 16)
TOP_K = 4
SWIGLU_LIMIT = 7.0
SWIGLU_ALPHA = 1.702
LN_EPS = 1e-5
NEG_INF = -1e30

V7X_VMEM_LIMIT_BYTES = 56 * 1024 * 1024

IN_PROJ_ROWS = 256
LOCAL_ROWS = 512
OUT_ROWS = 512
OUT_SUB_ROWS = 256
DISPATCH_ROWS = 256
COMBINE_ROWS = 256
DILATED_FILL_ROWS = 256
DILATED_GROUP = 16
EXPERT_ROWS = 3072
EXPERT_SUB = 256
EXPERT_FF_TILE = 256
EXPERT_OUT_TILE = 256


def _params(semantics):
    return pltpu.CompilerParams(dimension_semantics=semantics,
                                vmem_limit_bytes=V7X_VMEM_LIMIT_BYTES)


def _full_vmem():
    return pl.BlockSpec(memory_space=pltpu.VMEM)


def _layer_norm(h, g, b):
    mu = jnp.mean(h, axis=-1, keepdims=True)
    hc = h - mu
    var = jnp.mean(hc * hc, axis=-1, keepdims=True)
    return hc * lax.rsqrt(var + LN_EPS) * g + b


def _rms_norm(y, g):
    return y * lax.rsqrt(jnp.mean(y * y, axis=-1, keepdims=True) + LN_EPS) * g


def _gelu_tanh(x):
    c = 0.7978845608028654
    return x * (0.5 * (1.0 + jnp.tanh(c * (x + 0.044715 * (x * x * x)))))


def _dot_nt(a, b):
    return lax.dot_general(a, b, (((1,), (1,)), ((), ())), preferred_element_type=F32)


def _in_proj_kernel(bounds, x_ref, w_ref, *out_refs):
    xb = x_ref[...].astype(BF16)
    for o_ref, (lo, hi) in zip(out_refs, bounds):
        o_ref[...] = jnp.dot(xb, w_ref[:, lo:hi], preferred_element_type=F32)


def _in_proj(x2, w_in_bf, widths):
    t, d = x2.shape
    tm = IN_PROJ_ROWS
    bounds, lo = [], 0
    for w in widths:
        bounds.append((lo, lo + w))
        lo += w
    return pl.pallas_call(
        functools.partial(_in_proj_kernel, tuple(bounds)),
        grid=(t // tm,),
        in_specs=[pl.BlockSpec((tm, d), lambda i: (i, 0)), _full_vmem()],
        out_specs=[pl.BlockSpec((tm, w), lambda i: (i, 0)) for w in widths],
        out_shape=[jax.ShapeDtypeStruct((t, w), F32) for w in widths],
        compiler_params=_params(("arbitrary",)),
        name="in_proj",
    )(x2, w_in_bf)


def _kv_proj_kernel(mem_ref, w_ref, kv_ref):
    kv_ref[...] = jnp.dot(mem_ref[...].astype(BF16), w_ref[...],
                          preferred_element_type=F32).astype(BF16)


def _kv_proj(mem, w_kv_bf):
    b, n_mem, d = mem.shape
    n = w_kv_bf.shape[1]
    return pl.pallas_call(
        _kv_proj_kernel,
        grid=(b,),
        in_specs=[pl.BlockSpec((None, n_mem, d), lambda i: (i, 0, 0)), _full_vmem()],
        out_specs=pl.BlockSpec((None, n_mem, n), lambda i: (i, 0, 0)),
        out_shape=jax.ShapeDtypeStruct((b, n_mem, n), BF16),
        compiler_params=_params(("arbitrary",)),
        name="kv_proj",
    )(mem, w_kv_bf)


def _local_mix_kernel(za_ref, zm_ref, kv_ref, wsp_ref, bsp_ref, lng_ref, lnb_ref, nag_ref, nmg_ref,
                      ya_ref, ym_ref, ya_acc, ym_acc):
    rows = za_ref.shape[0]
    a_width = za_ref.shape[1] // 2
    n_groups = a_width // HEAD_DIM
    n_chunks = rows // CHUNK
    m_width = zm_ref.shape[1]
    m_heads = m_width // HEAD_DIM

    act = _gelu_tanh(za_ref[...])
    u = act[:, :a_width]
    v = _layer_norm(act[:, a_width:], lng_ref[...], lnb_ref[...])
    t_idx = lax.broadcasted_iota(I32, (CHUNK, CHUNK), 0)
    s_idx = lax.broadcasted_iota(I32, (CHUNK, CHUNK), 1)
    causal = s_idx <= t_idx
    bsp = bsp_ref[...]
    for g in range(n_groups):
        cols = slice(g * HEAD_DIM, (g + 1) * HEAD_DIM)
        w_g = jnp.where(causal, wsp_ref[g], 0.0).astype(BF16)
        v_g = jnp.concatenate([v[c * CHUNK:(c + 1) * CHUNK, cols] for c in range(n_chunks)], axis=1)
        mixed = jnp.dot(w_g, v_g.astype(BF16), preferred_element_type=F32) + bsp[:, g:g + 1]
        for c in range(n_chunks):
            rs = slice(c * CHUNK, (c + 1) * CHUNK)
            ya_acc[rs, cols] = u[rs, cols] * mixed[:, c * HEAD_DIM:(c + 1) * HEAD_DIM]
    ya_ref[...] = _rms_norm(ya_acc[...], nag_ref[...]).astype(BF16)

    scale = HEAD_DIM ** -0.5
    q = zm_ref[...]
    for h in range(m_heads):
        cols = slice(h * HEAD_DIM, (h + 1) * HEAD_DIM)
        k_h = kv_ref[:, h * HEAD_DIM:(h + 1) * HEAD_DIM]
        v_h = kv_ref[:, m_width + h * HEAD_DIM:m_width + (h + 1) * HEAD_DIM]
        s = _dot_nt(q[:, cols].astype(BF16), k_h) * scale
        p = jnp.exp(s - jnp.max(s, axis=-1, keepdims=True))
        l = jnp.sum(p, axis=-1, keepdims=True)
        ym_acc[:, cols] = jnp.dot(p.astype(BF16), v_h, preferred_element_type=F32) / l
    ym_ref[...] = _rms_norm(ym_acc[...], nmg_ref[...]).astype(BF16)


def _local_mix(za, zm, kv, w_spatial, b_spatial_t, a_ln_g, a_ln_b, norm_a_g, norm_m_g, batch, seq):
    rows = LOCAL_ROWS
    a2 = za.shape[1]
    a_width = a2 // 2
    m_width = zm.shape[1]
    n_mem, kv_w = kv.shape[1], kv.shape[2]
    tiles = seq // rows
    return pl.pallas_call(
        _local_mix_kernel,
        grid=(batch, tiles),
        in_specs=[
            pl.BlockSpec((rows, a2), lambda b, i: (b * tiles + i, 0)),
            pl.BlockSpec((rows, m_width), lambda b, i: (b * tiles + i, 0)),
            pl.BlockSpec((None, n_mem, kv_w), lambda b, i: (b, 0, 0)),
            _full_vmem(), _full_vmem(), _full_vmem(), _full_vmem(), _full_vmem(), _full_vmem(),
        ],
        out_specs=[pl.BlockSpec((rows, a_width), lambda b, i: (b * tiles + i, 0)),
                   pl.BlockSpec((rows, m_width), lambda b, i: (b * tiles + i, 0))],
        out_shape=[jax.ShapeDtypeStruct((batch * seq, a_width), BF16),
                   jax.ShapeDtypeStruct((batch * seq, m_width), BF16)],
        scratch_shapes=[pltpu.VMEM((rows, a_width), F32), pltpu.VMEM((rows, m_width), F32)],
        compiler_params=_params(("arbitrary", "arbitrary")),
        name="local_mix",
    )(za, zm, kv, w_spatial, b_spatial_t, a_ln_g, a_ln_b, norm_a_g, norm_m_g)


def _dilated_kernel(q_ref, k_ref, v_ref, o_ref, qs, ks, vs, *state):
    seq = q_ref.shape[0]
    scale = HEAD_DIM ** -0.5
    blk = ATT_BLOCK
    n_pat = len(DILATIONS)
    m_s, l_s, a_s = state[:n_pat], state[n_pat:2 * n_pat], state[2 * n_pat:]
    qi = lax.broadcasted_iota(I32, (blk, 2 * blk), 0)
    kj = lax.broadcasted_iota(I32, (blk, 2 * blk), 1)
    band = (kj >= qi) & (kj <= qi + blk)
    zero_blk = jnp.zeros((blk, HEAD_DIM), BF16)
    vs[:, HEAD_DIM:] = jnp.ones((vs.shape[0], HEAD_DIM), BF16)

    for p, d in enumerate(DILATIONS):
        sub = seq // d
        n_blocks = sub // blk
        chunk = min(sub, DILATED_FILL_ROWS)

        def fill(i, c, d=d, sub=sub, chunk=chunk):
            r = i // (sub // chunk)
            j = i % (sub // chunk)
            first = r + j * (chunk * d)
            src = pl.ds(first, chunk) if d == 1 else pl.ds(first, chunk, stride=d)
            dst_q = pl.multiple_of(r * sub + j * chunk, blk)
            dst_kv = pl.multiple_of(r * (sub + blk) + blk + j * chunk, blk)
            qs[pl.ds(dst_q, chunk), :] = q_ref[src, :].astype(BF16)
            ks[pl.ds(dst_kv, chunk), :] = k_ref[src, :].astype(BF16)
            vs[pl.ds(dst_kv, chunk), :HEAD_DIM] = v_ref[src, :].astype(BF16)
            return c

        def pad(r, c, sub=sub):
            at = pl.multiple_of(r * (sub + blk), blk)
            ks[pl.ds(at, blk), :] = zero_blk
            vs[pl.ds(at, blk), :HEAD_DIM] = zero_blk
            return c

        lax.fori_loop(0, d, pad, 0)
        lax.fori_loop(0, d * (sub // chunk), fill, 0)

        def block(c, d=d, n_blocks=n_blocks, p=p):
            r = c // n_blocks
            n = c % n_blocks
            q = qs[pl.ds(pl.multiple_of(c * blk, blk), blk), :]
            win = pl.ds(pl.multiple_of((c + r) * blk, blk), 2 * blk)
            s = _dot_nt(q, ks[win, :]) * scale
            valid = band & (kj >= jnp.where(n > 0, 0, blk))
            s = jnp.where(valid, s, NEG_INF)
            m = jnp.max(s, axis=-1, keepdims=True)
            e = jnp.exp(s - m)
            acc = jnp.dot(e.astype(BF16), vs[win, :], preferred_element_type=F32)
            first = n * (blk * d) + r
            out = pl.ds(first, blk) if d == 1 else pl.ds(first, blk, stride=d)
            m_s[p][out, :] = jnp.broadcast_to(m, (blk, HEAD_DIM))
            l_s[p][out, :] = acc[:, HEAD_DIM:]
            a_s[p][out, :] = acc[:, :HEAD_DIM]

        def group(i, c, block=block):
            for g in range(DILATED_GROUP):
                block(i * DILATED_GROUP + g)
            return c

        lax.fori_loop(0, seq // blk // DILATED_GROUP, group, 0)

    def merge(j, c):
        rows = pl.ds(pl.multiple_of(j * blk, blk), blk)
        ms = [m[rows, :] for m in m_s]
        m_all = functools.reduce(jnp.maximum, ms)
        ws = [jnp.exp(m - m_all) for m in ms]
        den = sum(w * l[rows, :] for w, l in zip(ws, l_s))
        num = sum(w * a[rows, :] for w, a in zip(ws, a_s))
        o_ref[rows, :] = num / den
        return c

    lax.fori_loop(0, seq // blk, merge, 0)


def _dilated_attention(zq, zk, zv, batch, seq):
    width = zq.shape[1]
    heads = width // HEAD_DIM
    spec = pl.BlockSpec((seq, HEAD_DIM), lambda b, h: (b, h))
    kv_rows = seq + max(DILATIONS) * ATT_BLOCK
    return pl.pallas_call(
        _dilated_kernel,
        grid=(batch, heads),
        in_specs=[spec, spec, spec],
        out_specs=spec,
        out_shape=jax.ShapeDtypeStruct((batch * seq, width), F32),
        scratch_shapes=[pltpu.VMEM((seq, HEAD_DIM), BF16), pltpu.VMEM((kv_rows, HEAD_DIM), BF16),
                        pltpu.VMEM((kv_rows, 2 * HEAD_DIM), BF16)]
                       + [pltpu.VMEM((seq, HEAD_DIM), F32)] * (3 * len(DILATIONS)),
        compiler_params=_params(("arbitrary", "arbitrary")),
        name="dilated",
    )(zq, zk, zv)


def _out_router_kernel(alpha, x_ref, ya_ref, yb_ref, ym_ref, wo_ref, nbg_ref, g_ref, b_ref,
                       wrh_ref, wrl_ref, br_ref,
                       x1_ref, idx_ref, gate_ref, rank_ref, cnt_ref, base):
    step = pl.program_id(0)
    tm = OUT_SUB_ROWS
    a_w = ya_ref.shape[1]
    b_w = yb_ref.shape[1]
    n_exp = br_ref.shape[1]

    @pl.when(step == 0)
    def _():
        base[...] = jnp.zeros_like(base)

    lane = lax.broadcasted_iota(I32, (tm, n_exp), 1).astype(F32)
    t_i = lax.broadcasted_iota(I32, (tm, tm), 0)
    t_j = lax.broadcasted_iota(I32, (tm, tm), 1)
    earlier = (t_j < t_i).astype(BF16)
    k_lane = lax.broadcasted_iota(I32, (tm, TOP_K), 1)
    running = base[...]

    for h in range(x_ref.shape[0] // tm):
        rows = slice(h * tm, (h + 1) * tm)
        yb = _rms_norm(yb_ref[rows, :], nbg_ref[...]).astype(BF16)
        y = jnp.concatenate([ya_ref[rows, :], yb, ym_ref[rows, :]], axis=1)
        mix = jnp.dot(y, wo_ref[...], preferred_element_type=F32)
        x1 = _layer_norm(alpha * x_ref[rows, :] + mix, g_ref[...], b_ref[...])
        x1_ref[rows, :] = x1

        x_hi = x1.astype(BF16)
        x_lo = (x1 - x_hi.astype(F32)).astype(BF16)
        logits = (jnp.dot(x_hi, wrh_ref[...], preferred_element_type=F32)
                  + jnp.dot(x_lo, wrh_ref[...], preferred_element_type=F32)
                  + jnp.dot(x_hi, wrl_ref[...], preferred_element_type=F32)) + br_ref[...]

        work = logits
        vals, sels = [], []
        for _ in range(TOP_K):
            m = jnp.max(work, axis=-1, keepdims=True)
            sel = jnp.min(jnp.where(work == m, lane, float(n_exp)), axis=-1, keepdims=True)
            vals.append(m)
            sels.append(sel)
            work = jnp.where(lane == sel, -jnp.inf, work)
        exps = [jnp.exp(v - vals[0]) for v in vals]
        denom = exps[0] + exps[1] + exps[2] + exps[3]

        hot = [lane == s for s in sels]
        multi = sum(h_.astype(F32) for h_ in hot)
        before = jnp.dot(earlier, multi.astype(BF16), preferred_element_type=F32) + running
        idx_out = jnp.zeros((tm, TOP_K), I32)
        gate_out = jnp.zeros((tm, TOP_K), F32)
        rank_out = jnp.zeros((tm, TOP_K), I32)
        for k in range(TOP_K):
            rank_k = jnp.sum(jnp.where(hot[k], before, 0.0), axis=-1, keepdims=True).astype(I32)
            idx_out = jnp.where(k_lane == k, sels[k].astype(I32), idx_out)
            gate_out = jnp.where(k_lane == k, exps[k] / denom, gate_out)
            rank_out = jnp.where(k_lane == k, rank_k, rank_out)
        idx_ref[rows, :] = idx_out
        gate_ref[rows, :] = gate_out
        rank_ref[rows, :] = rank_out
        running = running + jnp.sum(multi, axis=0, keepdims=True)

    base[...] = running
    cnt_ref[...] = running.astype(I32)


def _out_router(alpha, x2, ya, yb, ym, w_out_bf, norm_b_g, ln_g, ln_b, wr_hi, wr_lo, b_router):
    t, d = x2.shape
    tm = OUT_ROWS
    n_exp = b_router.shape[1]
    row = lambda w: pl.BlockSpec((tm, w), lambda i: (i, 0))
    return pl.pallas_call(
        functools.partial(_out_router_kernel, alpha),
        grid=(t // tm,),
        in_specs=[row(d), row(ya.shape[1]), row(yb.shape[1]), row(ym.shape[1])] + [_full_vmem()] * 7,
        out_specs=[row(d), row(TOP_K), row(TOP_K), row(TOP_K),
                   pl.BlockSpec((1, n_exp), lambda i: (0, 0))],
        out_shape=[jax.ShapeDtypeStruct((t, d), F32),
                   jax.ShapeDtypeStruct((t, TOP_K), I32),
                   jax.ShapeDtypeStruct((t, TOP_K), F32),
                   jax.ShapeDtypeStruct((t, TOP_K), I32),
                   jax.ShapeDtypeStruct((1, n_exp), I32)],
        scratch_shapes=[pltpu.VMEM((1, n_exp), F32)],
        compiler_params=_params(("arbitrary",)),
        name="out_router",
    )(x2, ya, yb, ym, w_out_bf, norm_b_g, ln_g, ln_b, wr_hi, wr_lo, b_router)


def _dispatch_kernel(ztile_ref, total_ref, dest_ref, x_ref, xs_hbm, zeros, sem, zsem):
    tm, d = x_ref.shape
    sub = zeros.shape[0]
    n_tiles = xs_hbm.shape[0] // sub

    @pl.when(pl.program_id(0) == 0)
    def _():
        zeros[...] = jnp.zeros_like(zeros)

        def fill(first_row):
            return pltpu.make_async_copy(zeros, xs_hbm.at[pl.ds(pl.multiple_of(first_row, sub), sub), :], zsem)

        def partial_tiles(run):
            def step(e, c):
                @pl.when(ztile_ref[e] >= 0)
                def _():
                    run(fill(ztile_ref[e]))
                return c
            lax.fori_loop(0, ztile_ref.shape[0], step, 0)

        def tail_tiles(run):
            def step(j, c):
                run(fill(j * sub))
                return c
            lax.fori_loop(total_ref[0] // sub, n_tiles, step, 0)

        partial_tiles(lambda copy: copy.start())
        tail_tiles(lambda copy: copy.start())
        partial_tiles(lambda copy: copy.wait())
        tail_tiles(lambda copy: copy.wait())

    def row_copy(t, slot):
        return pltpu.make_async_copy(x_ref.at[pl.ds(t, 1), :], xs_hbm.at[pl.ds(slot, 1), :], sem)

    for t in range(tm):
        for k in range(TOP_K):
            row_copy(t, dest_ref[t * TOP_K + k]).start()
    for _ in range(TOP_K):
        pltpu.make_async_copy(x_ref, xs_hbm.at[pl.ds(0, tm), :], sem).wait()


def _dispatch(x1, dest_flat, ztile, total_rows, n_rows):
    t, d = x1.shape
    tm = DISPATCH_ROWS
    grid_spec = pltpu.PrefetchScalarGridSpec(
        num_scalar_prefetch=2,
        grid=(t // tm,),
        in_specs=[pl.BlockSpec((tm * TOP_K,), lambda i, z, n: (i,), memory_space=pltpu.SMEM),
                  pl.BlockSpec((tm, d), lambda i, z, n: (i, 0))],
        out_specs=pl.BlockSpec(memory_space=pl.ANY),
        scratch_shapes=[pltpu.VMEM((EXPERT_SUB, d), F32),
                        pltpu.SemaphoreType.DMA, pltpu.SemaphoreType.DMA],
    )
    return pl.pallas_call(
        _dispatch_kernel,
        grid_spec=grid_spec,
        out_shape=jax.ShapeDtypeStruct((n_rows, d), F32),
        compiler_params=_params(("arbitrary",)),
        name="dispatch",
    )(ztile, total_rows, dest_flat, x1)


def _experts_kernel(nf, sbe_ref, sbr_ref, sbf_ref, na_ref, xs_hbm, wg_ref, wu_ref, bg_ref, bu_ref,
                    wd_ref, bd_ref, o_ref, xb, act, stage, wgu, wdn, sem):
    g = pl.program_id(0)
    s = pl.program_id(1)
    nn = pl.num_programs(1) - nf
    rows_max = xb.shape[0]
    sub = stage.shape[1]
    n_chunks = rows_max // sub
    n_sub = sbr_ref[g] // sub
    nxt = jnp.minimum(g + 1, pl.num_programs(0) - 1)
    n_next = jnp.where(g + 1 < pl.num_programs(0), sbr_ref[nxt], 0) // sub

    def chunk_copy(sb, c, slot):
        first = pl.multiple_of(sbf_ref[sb] + c * sub, sub)
        return pltpu.make_async_copy(xs_hbm.at[pl.ds(first, sub), :], stage.at[slot], sem.at[slot])

    def land(c, slot):
        chunk_copy(0, 0, slot).wait()
        xb[pl.ds(pl.multiple_of(c * sub, sub), sub), :] = stage[slot].astype(BF16)

    def row_tiles(body):
        n_big = n_sub // 8

        def big(j, c):
            body(pl.multiple_of(j * (8 * sub), 8 * sub), 8 * sub)
            return c

        lax.fori_loop(0, n_big, big, 0)
        done = n_big * 8
        for size in (4, 2, 1):
            take = (n_sub - done) >= size

            @pl.when(take)
            def _(done=done, size=size):
                body(pl.multiple_of(done * sub, size * sub), size * sub)

            done = done + jnp.where(take, size, 0)

    @pl.when(g >= na_ref[0])
    def _():
        o_ref[...] = jnp.zeros_like(o_ref)

    @pl.when(g < na_ref[0])
    def _():
        @pl.when((g == 0) & (s == 0))
        def _():
            def fetch(c, carry):
                chunk_copy(0, c, 0).start()
                land(c, 0)
                return carry
            lax.fori_loop(0, n_sub, fetch, 0)

        for slot in range(2):
            @pl.when((g > 0) & (s == 0) & (nn - 1 + slot * nn < n_sub))
            def _(slot=slot):
                land(nn - 1 + slot * nn, slot)

        @pl.when(s < nf)
        def _():
            wgu[0] = wg_ref[...].astype(BF16)
            wgu[1] = wu_ref[...].astype(BF16)

            def body(first, rows):
                x = xb[pl.ds(first, rows), :]
                gate = jnp.dot(x, wgu[0], preferred_element_type=F32) + bg_ref[...]
                up = jnp.dot(x, wgu[1], preferred_element_type=F32) + bu_ref[...]
                gate = jnp.minimum(gate, SWIGLU_LIMIT)
                up = jnp.clip(up, -SWIGLU_LIMIT, SWIGLU_LIMIT)
                a = (up + 1.0) * (gate * jax.nn.sigmoid(SWIGLU_ALPHA * gate))
                act[s, pl.ds(first, rows), :] = a.astype(BF16)

            row_tiles(body)

        @pl.when(s >= nf)
        def _():
            k = s - nf
            for slot in range(2):
                @pl.when((k >= 1) & (k - 1 + slot * nn < n_next))
                def _(slot=slot):
                    land(k - 1 + slot * nn, slot)

                @pl.when(k + slot * nn < n_next)
                def _(slot=slot):
                    chunk_copy(nxt, k + slot * nn, slot).start()

            wdn[...] = wd_ref[...].astype(BF16)

            def body(first, rows):
                a = jnp.concatenate([act[j, pl.ds(first, rows), :] for j in range(nf)], axis=1)
                o_ref[pl.ds(first, rows), :] = jnp.dot(a, wdn[...], preferred_element_type=F32) + bd_ref[...]

            row_tiles(body)

            def clear(c, carry):
                o_ref[pl.ds(pl.multiple_of(c * sub, sub), sub), :] = jnp.zeros((sub, o_ref.shape[1]), F32)
                return carry
            lax.fori_loop(n_sub, n_chunks, clear, 0)


def _experts(xs, sb_expert, sb_rows, sb_first, n_active, n_super, w_gate_up, b_gate_up, w_down, b_down):
    d = xs.shape[1]
    n_exp, _, two_ff = w_gate_up.shape
    d_ff = two_ff // 2
    tm, tf, tn, sub = EXPERT_ROWS, EXPERT_FF_TILE, EXPERT_OUT_TILE, EXPERT_SUB
    nf, nn = d_ff // tf, d // tn
    assert 2 * nn >= tm // sub, "two row chunks per down-projection step must cover a super-block"
    bgu = b_gate_up.reshape(n_exp, 1, two_ff)
    bdn = b_down.reshape(n_exp, 1, d)

    def expert(g, sbe, na):
        return sbe[jnp.minimum(g, na[0] - 1)]

    def ff_step(g, s, na):
        return jnp.where(g < na[0], jnp.minimum(s, nf - 1), nf - 1)

    def down_step(g, s, na):
        return jnp.where(g < na[0], jnp.maximum(s - nf, 0), nn - 1)

    in_specs = [
        pl.BlockSpec(memory_space=pl.ANY),
        pl.BlockSpec((None, d, tf), lambda g, s, sbe, sbr, sbf, na: (expert(g, sbe, na), 0, ff_step(g, s, na))),
        pl.BlockSpec((None, d, tf), lambda g, s, sbe, sbr, sbf, na: (expert(g, sbe, na), 0, nf + ff_step(g, s, na))),
        pl.BlockSpec((None, 1, tf), lambda g, s, sbe, sbr, sbf, na: (expert(g, sbe, na), 0, ff_step(g, s, na))),
        pl.BlockSpec((None, 1, tf), lambda g, s, sbe, sbr, sbf, na: (expert(g, sbe, na), 0, nf + ff_step(g, s, na))),
        pl.BlockSpec((None, d_ff, tn), lambda g, s, sbe, sbr, sbf, na: (expert(g, sbe, na), 0, down_step(g, s, na))),
        pl.BlockSpec((None, 1, tn), lambda g, s, sbe, sbr, sbf, na: (expert(g, sbe, na), 0, down_step(g, s, na))),
    ]
    grid_spec = pltpu.PrefetchScalarGridSpec(
        num_scalar_prefetch=4,
        grid=(n_super, nf + nn),
        in_specs=in_specs,
        out_specs=pl.BlockSpec((tm, tn), lambda g, s, sbe, sbr, sbf, na: (g, jnp.maximum(s - nf, 0))),
        scratch_shapes=[pltpu.VMEM((tm, d), BF16), pltpu.VMEM((nf, tm, tf), BF16),
                        pltpu.VMEM((2, sub, d), F32), pltpu.VMEM((2, d, tf), BF16),
                        pltpu.VMEM((d_ff, tn), BF16), pltpu.SemaphoreType.DMA((2,))],
    )
    return pl.pallas_call(
        functools.partial(_experts_kernel, nf),
        grid_spec=grid_spec,
        out_shape=jax.ShapeDtypeStruct((n_super * tm, d), F32),
        compiler_params=_params(("arbitrary", "arbitrary")),
        name="experts",
    )(sb_expert, sb_rows, sb_first, n_active, xs, w_gate_up, w_gate_up, bgu, bgu, w_down, bdn)


def _combine_kernel(alpha, dest_ref, x1_ref, gate_ref, g_ref, b_ref, ys_hbm, o_ref, rows, sem):
    tm, d = x1_ref.shape

    def row_copy(k, t, slot):
        return pltpu.make_async_copy(ys_hbm.at[pl.ds(slot, 1), :], rows.at[k, pl.ds(t, 1), :], sem)

    for t in range(tm):
        for k in range(TOP_K):
            row_copy(k, t, dest_ref[t * TOP_K + k]).start()
    for k in range(TOP_K):
        pltpu.make_async_copy(ys_hbm.at[pl.ds(0, tm), :], rows.at[k], sem).wait()

    gates = gate_ref[...]
    ffn = gates[:, 0:1] * rows[0]
    for k in range(1, TOP_K):
        ffn += gates[:, k:k + 1] * rows[k]
    o_ref[...] = _layer_norm(alpha * x1_ref[...] + ffn, g_ref[...], b_ref[...])


def _combine(alpha, dest_flat, x1, gates, ln_g, ln_b, ys):
    t, d = x1.shape
    tm = COMBINE_ROWS
    return pl.pallas_call(
        functools.partial(_combine_kernel, alpha),
        grid=(t // tm,),
        in_specs=[pl.BlockSpec((tm * TOP_K,), lambda i: (i,), memory_space=pltpu.SMEM),
                  pl.BlockSpec((tm, d), lambda i: (i, 0)),
                  pl.BlockSpec((tm, TOP_K), lambda i: (i, 0)),
                  _full_vmem(), _full_vmem(),
                  pl.BlockSpec(memory_space=pl.ANY)],
        out_specs=pl.BlockSpec((tm, d), lambda i: (i, 0)),
        out_shape=jax.ShapeDtypeStruct((t, d), F32),
        scratch_shapes=[pltpu.VMEM((TOP_K, tm, d), F32), pltpu.SemaphoreType.DMA],
        compiler_params=_params(("arbitrary",)),
        name="combine",
    )(dest_flat, x1, gates, ln_g, ln_b, ys)


def _routing_tables(idx, rank, counts, n_super):
    tm, sub = EXPERT_ROWS, EXPERT_SUB
    n_exp = counts.shape[0]
    padded = (counts + sub - 1) // sub * sub
    row_end = jnp.cumsum(padded)
    row_start = row_end - padded
    per_expert = (padded + tm - 1) // tm
    sb_end = jnp.cumsum(per_expert)
    sb_start = sb_end - per_expert
    idx_flat, rank_flat = idx.reshape(-1), rank.reshape(-1)
    x_off = sum(jnp.where(idx_flat == e, row_start[e], 0) for e in range(n_exp))
    y_off = sum(jnp.where(idx_flat == e, sb_start[e] * tm, 0) for e in range(n_exp))
    dest_x = (x_off + rank_flat).astype(I32)
    dest_y = (y_off + rank_flat).astype(I32)
    ztile = jnp.where(counts % sub != 0, row_start + counts // sub * sub, -1).astype(I32)
    g = jnp.arange(n_super, dtype=I32)
    owner = jnp.sum((g[:, None] >= sb_end[None, :]).astype(I32), axis=1)
    sb_expert = jnp.minimum(owner, n_exp - 1).astype(I32)
    local = (g - sb_start[sb_expert]) * tm
    active = g < sb_end[-1]
    sb_rows = jnp.where(active, jnp.clip(padded[sb_expert] - local, 0, tm), 0).astype(I32)
    sb_first = jnp.where(active, row_start[sb_expert] + local, 0).astype(I32)
    return (dest_x, dest_y, ztile, row_end[-1:].astype(I32), sb_expert, sb_rows, sb_first,
            sb_end[-1:].astype(I32))


def _layer(alpha, x, mem, w_in, w_spatial, b_spatial, a_ln_g, a_ln_b, w_mem_kv, norm_a_g, norm_b_g,
           norm_m_g, w_out, ln1_g, ln1_b, w_router, b_router, w_gate_up, b_gate_up, w_down, b_down,
           ln2_g, ln2_b):
    batch, seq, d = x.shape
    t = batch * seq
    m_width = w_mem_kv.shape[1] // 2
    a_width = a_ln_g.shape[0]
    b_width = norm_b_g.shape[0]
    n_exp = w_router.shape[1]
    row = lambda v: v.reshape(1, -1)

    x2 = x.reshape(t, d)
    za, zq, zk, zv, zm = _in_proj(x2, w_in.astype(BF16), (2 * a_width, b_width, b_width, b_width, m_width))
    kv = _kv_proj(mem, w_mem_kv.astype(BF16))
    ya, ym = _local_mix(za, zm, kv, w_spatial, b_spatial.T, row(a_ln_g), row(a_ln_b),
                        row(norm_a_g), row(norm_m_g), batch, seq)
    yb = _dilated_attention(zq, zk, zv, batch, seq)

    wr_hi = w_router.astype(BF16)
    wr_lo = (w_router - wr_hi.astype(F32)).astype(BF16)
    x1, idx, gates, rank, counts = _out_router(
        alpha, x2, ya, yb, ym, w_out.astype(BF16), row(norm_b_g), row(ln1_g), row(ln1_b),
        wr_hi, wr_lo, row(b_router))

    n_assign = t * TOP_K
    n_super = (n_assign + n_exp * (EXPERT_ROWS - 1)) // EXPERT_ROWS
    n_x_rows = (n_assign + n_exp * (EXPERT_SUB - 1)) // EXPERT_SUB * EXPERT_SUB
    dest_x, dest_y, ztile, total_rows, sb_expert, sb_rows, sb_first, n_active = _routing_tables(
        idx, rank, counts.reshape(n_exp), n_super)
    xs = _dispatch(x1, dest_x, ztile, total_rows, n_x_rows)
    ys = _experts(xs, sb_expert, sb_rows, sb_first, n_active, n_super, w_gate_up, b_gate_up, w_down, b_down)
    out = _combine(alpha, dest_y, x1, gates, row(ln2_g), row(ln2_b), ys)
    return out.reshape(batch, seq, d)


def kernel(x, mem, w_in, w_spatial, b_spatial, a_ln_g, a_ln_b, w_mem_kv, norm_a_g, norm_b_g, norm_m_g, w_out, ln1_g, ln1_b, w_router, b_router, w_gate_up, b_gate_up, w_down, b_down, ln2_g, ln2_b):
    depth = w_in.shape[0]
    alpha = (2.0 * depth) ** 0.25
    for l in range(depth):
        x = _layer(alpha, x, mem, w_in[l], w_spatial[l], b_spatial[l], a_ln_g[l], a_ln_b[l], w_mem_kv[l],
                   norm_a_g[l], norm_b_g[l], norm_m_g[l], w_out[l], ln1_g[l], ln1_b[l], w_router[l],
                   b_router[l], w_gate_up[l], b_gate_up[l], w_down[l], b_down[l], ln2_g[l], ln2_b[l])
    return x
```

```python
import functools

import jax
import jax.numpy as jnp
from jax import lax
from jax.experimental import pallas as pl
from jax.experimental.pallas import tpu as pltpu

F32 = jnp.float32
BF16 = jnp.bfloat16
I32 = jnp.int32
U32 = jnp.uint32

HEAD_DIM = 128
CHUNK = 128
ATT_BLOCK = 128
DILATIONS = (1, 4, 16)
TOP_K = 4
SWIGLU_LIMIT = 7.0
SWIGLU_ALPHA = 1.702
LN_EPS = 1e-5
NEG_INF = -1e30

V7X_VMEM_LIMIT_BYTES = 56 * 1024 * 1024

IN_PROJ_ROWS = 256
LOCAL_ROWS = 512
OUT_ROWS = 512
OUT_SUB_ROWS = 256
DISPATCH_ROWS = 256
COMBINE_ROWS = 256
DILATED_FILL_ROWS = 256
DILATED_GROUP = 16
EXPERT_ROWS = 3072
EXPERT_SUB = 256
EXPERT_FF_TILE = 256
EXPERT_OUT_TILE = 256


def _params(semantics):
    return pltpu.CompilerParams(dimension_semantics=semantics,
                                vmem_limit_bytes=V7X_VMEM_LIMIT_BYTES)


def _full_vmem():
    return pl.BlockSpec(memory_space=pltpu.VMEM)


def _layer_norm(h, g, b):
    mu = jnp.mean(h, axis=-1, keepdims=True)
    hc = h - mu
    var = jnp.mean(hc * hc, axis=-1, keepdims=True)
    return hc * lax.rsqrt(var + LN_EPS) * g + b


def _rms_norm(y, g):
    return y * lax.rsqrt(jnp.mean(y * y, axis=-1, keepdims=True) + LN_EPS) * g


def _gelu_tanh(x):
    c = 0.7978845608028654
    return x * (0.5 * (1.0 + jnp.tanh(c * (x + 0.044715 * (x * x * x)))))


def _dot_nt(a, b):
    return lax.dot_general(a, b, (((1,), (1,)), ((), ())), preferred_element_type=F32)


def _in_proj_kernel(bounds, x_ref, w_ref, *out_refs):
    xb = x_ref[...].astype(BF16)
    for o_ref, (lo, hi) in zip(out_refs, bounds):
        o_ref[...] = jnp.dot(xb, w_ref[:, lo:hi], preferred_element_type=F32)


def _in_proj(x2, w_in_bf, widths):
    t, d = x2.shape
    tm = IN_PROJ_ROWS
    bounds, lo = [], 0
    for w in widths:
        bounds.append((lo, lo + w))
        lo += w
    return pl.pallas_call(
        functools.partial(_in_proj_kernel, tuple(bounds)),
        grid=(t // tm,),
        in_specs=[pl.BlockSpec((tm, d), lambda i: (i, 0)), _full_vmem()],
        out_specs=[pl.BlockSpec((tm, w), lambda i: (i, 0)) for w in widths],
        out_shape=[jax.ShapeDtypeStruct((t, w), F32) for w in widths],
        compiler_params=_params(("arbitrary",)),
        name="in_proj",
    )(x2, w_in_bf)


def _kv_proj_kernel(mem_ref, w_ref, kv_ref):
    kv_ref[...] = jnp.dot(mem_ref[...].astype(BF16), w_ref[...],
                          preferred_element_type=F32).astype(BF16)


def _kv_proj(mem, w_kv_bf):
    b, n_mem, d = mem.shape
    n = w_kv_bf.shape[1]
    return pl.pallas_call(
        _kv_proj_kernel,
        grid=(b,),
        in_specs=[pl.BlockSpec((None, n_mem, d), lambda i: (i, 0, 0)), _full_vmem()],
        out_specs=pl.BlockSpec((None, n_mem, n), lambda i: (i, 0, 0)),
        out_shape=jax.ShapeDtypeStruct((b, n_mem, n), BF16),
        compiler_params=_params(("arbitrary",)),
        name="kv_proj",
    )(mem, w_kv_bf)


def _local_mix_kernel(za_ref, zm_ref, kv_ref, wsp_ref, bsp_ref, lng_ref, lnb_ref, nag_ref, nmg_ref,
                      ya_ref, ym_ref, ya_acc, ym_acc):
    rows = za_ref.shape[0]
    a_width = za_ref.shape[1] // 2
    n_groups = a_width // HEAD_DIM
    n_chunks = rows // CHUNK
    m_width = zm_ref.shape[1]
    m_heads = m_width // HEAD_DIM

    act = _gelu_tanh(za_ref[...])
    u = act[:, :a_width]
    v = _layer_norm(act[:, a_width:], lng_ref[...], lnb_ref[...])
    t_idx = lax.broadcasted_iota(I32, (CHUNK, CHUNK), 0)
    s_idx = lax.broadcasted_iota(I32, (CHUNK, CHUNK), 1)
    causal = s_idx <= t_idx
    bsp = bsp_ref[...]
    for g in range(n_groups):
        cols = slice(g * HEAD_DIM, (g + 1) * HEAD_DIM)
        w_g = jnp.where(causal, wsp_ref[g], 0.0).astype(BF16)
        v_g = jnp.concatenate([v[c * CHUNK:(c + 1) * CHUNK, cols] for c in range(n_chunks)], axis=1)
        mixed = jnp.dot(w_g, v_g.astype(BF16), preferred_element_type=F32) + bsp[:, g:g + 1]
        for c in range(n_chunks):
            rs = slice(c * CHUNK, (c + 1) * CHUNK)
            ya_acc[rs, cols] = u[rs, cols] * mixed[:, c * HEAD_DIM:(c + 1) * HEAD_DIM]
    ya_ref[...] = _rms_norm(ya_acc[...], nag_ref[...]).astype(BF16)

    scale = HEAD_DIM ** -0.5
    q = zm_ref[...]
    for h in range(m_heads):
        cols = slice(h * HEAD_DIM, (h + 1) * HEAD_DIM)
        k_h = kv_ref[:, h * HEAD_DIM:(h + 1) * HEAD_DIM]
        v_h = kv_ref[:, m_width + h * HEAD_DIM:m_width + (h + 1) * HEAD_DIM]
        s = _dot_nt(q[:, cols].astype(BF16), k_h) * scale
        p = jnp.exp(s - jnp.max(s, axis=-1, keepdims=True))
        l = jnp.sum(p, axis=-1, keepdims=True)
        ym_acc[:, cols] = jnp.dot(p.astype(BF16), v_h, preferred_element_type=F32) / l
    ym_ref[...] = _rms_norm(ym_acc[...], nmg_ref[...]).astype(BF16)


def _local_mix(za, zm, kv, w_spatial, b_spatial_t, a_ln_g, a_ln_b, norm_a_g, norm_m_g, batch, seq):
    rows = LOCAL_ROWS
    a2 = za.shape[1]
    a_width = a2 // 2
    m_width = zm.shape[1]
    n_mem, kv_w = kv.shape[1], kv.shape[2]
    tiles = seq // rows
    return pl.pallas_call(
        _local_mix_kernel,
        grid=(batch, tiles),
        in_specs=[
            pl.BlockSpec((rows, a2), lambda b, i: (b * tiles + i, 0)),
            pl.BlockSpec((rows, m_width), lambda b, i: (b * tiles + i, 0)),
            pl.BlockSpec((None, n_mem, kv_w), lambda b, i: (b, 0, 0)),
            _full_vmem(), _full_vmem(), _full_vmem(), _full_vmem(), _full_vmem(), _full_vmem(),
        ],
        out_specs=[pl.BlockSpec((rows, a_width), lambda b, i: (b * tiles + i, 0)),
                   pl.BlockSpec((rows, m_width), lambda b, i: (b * tiles + i, 0))],
        out_shape=[jax.ShapeDtypeStruct((batch * seq, a_width), BF16),
                   jax.ShapeDtypeStruct((batch * seq, m_width), BF16)],
        scratch_shapes=[pltpu.VMEM((rows, a_width), F32), pltpu.VMEM((rows, m_width), F32)],
        compiler_params=_params(("arbitrary", "arbitrary")),
        name="local_mix",
    )(za, zm, kv, w_spatial, b_spatial_t, a_ln_g, a_ln_b, norm_a_g, norm_m_g)


def _dilated_kernel(q_ref, k_ref, v_ref, o_ref, qs, ks, vs, *state):
    seq = q_ref.shape[0]
    scale = HEAD_DIM ** -0.5
    blk = ATT_BLOCK
    n_pat = len(DILATIONS)
    m_s, l_s, a_s = state[:n_pat], state[n_pat:2 * n_pat], state[2 * n_pat:]
    qi = lax.broadcasted_iota(I32, (blk, 2 * blk), 0)
    kj = lax.broadcasted_iota(I32, (blk, 2 * blk), 1)
    band = (kj >= qi) & (kj <= qi + blk)
    zero_blk = jnp.zeros((blk, HEAD_DIM), BF16)
    vs[:, HEAD_DIM:] = jnp.ones((vs.shape[0], HEAD_DIM), BF16)

    for p, d in enumerate(DILATIONS):
        sub = seq // d
        n_blocks = sub // blk
        chunk = min(sub, DILATED_FILL_ROWS)

        def fill(i, c, d=d, sub=sub, chunk=chunk):
            r = i // (sub // chunk)
            j = i % (sub // chunk)
            first = r + j * (chunk * d)
            src = pl.ds(first, chunk) if d == 1 else pl.ds(first, chunk, stride=d)
            dst_q = pl.multiple_of(r * sub + j * chunk, blk)
            dst_kv = pl.multiple_of(r * (sub + blk) + blk + j * chunk, blk)
            qs[pl.ds(dst_q, chunk), :] = q_ref[src, :].astype(BF16)
            ks[pl.ds(dst_kv, chunk), :] = k_ref[src, :].astype(BF16)
            vs[pl.ds(dst_kv, chunk), :HEAD_DIM] = v_ref[src, :].astype(BF16)
            return c

        def pad(r, c, sub=sub):
            at = pl.multiple_of(r * (sub + blk), blk)
            ks[pl.ds(at, blk), :] = zero_blk
            vs[pl.ds(at, blk), :HEAD_DIM] = zero_blk
            return c

        lax.fori_loop(0, d, pad, 0)
        lax.fori_loop(0, d * (sub // chunk), fill, 0)

        def block(c, d=d, n_blocks=n_blocks, p=p):
            r = c // n_blocks
            n = c % n_blocks
            q = qs[pl.ds(pl.multiple_of(c * blk, blk), blk), :]
            win = pl.ds(pl.multiple_of((c + r) * blk, blk), 2 * blk)
            s = _dot_nt(q, ks[win, :]) * scale
            valid = band & (kj >= jnp.where(n > 0, 0, blk))
            s = jnp.where(valid, s, NEG_INF)
            m = jnp.max(s, axis=-1, keepdims=True)
            e = jnp.exp(s - m)
            acc = jnp.dot(e.astype(BF16), vs[win, :], preferred_element_type=F32)
            first = n * (blk * d) + r
            out = pl.ds(first, blk) if d == 1 else pl.ds(first, blk, stride=d)
            m_s[p][out, :] = jnp.broadcast_to(m, (blk, HEAD_DIM))
            l_s[p][out, :] = acc[:, HEAD_DIM:]
            a_s[p][out, :] = acc[:, :HEAD_DIM]

        def group(i, c, block=block):
            for g in range(DILATED_GROUP):
                block(i * DILATED_GROUP + g)
            return c

        lax.fori_loop(0, seq // blk // DILATED_GROUP, group, 0)

    def merge(j, c):
        rows = pl.ds(pl.multiple_of(j * blk, blk), blk)
        ms = [m[rows, :] for m in m_s]
        m_all = functools.reduce(jnp.maximum, ms)
        ws = [jnp.exp(m - m_all) for m in ms]
        den = sum(w * l[rows, :] for w, l in zip(ws, l_s))
        num = sum(w * a[rows, :] for w, a in zip(ws, a_s))
        o_ref[rows, :] = num / den
        return c

    lax.fori_loop(0, seq // blk, merge, 0)


def _dilated_attention(zq, zk, zv, batch, seq):
    width = zq.shape[1]
    heads = width // HEAD_DIM
    spec = pl.BlockSpec((seq, HEAD_DIM), lambda b, h: (b, h))
    kv_rows = seq + max(DILATIONS) * ATT_BLOCK
    return pl.pallas_call(
        _dilated_kernel,
        grid=(batch, heads),
        in_specs=[spec, spec, spec],
        out_specs=spec,
        out_shape=jax.ShapeDtypeStruct((batch * seq, width), F32),
        scratch_shapes=[pltpu.VMEM((seq, HEAD_DIM), BF16), pltpu.VMEM((kv_rows, HEAD_DIM), BF16),
                        pltpu.VMEM((kv_rows, 2 * HEAD_DIM), BF16)]
                       + [pltpu.VMEM((seq, HEAD_DIM), F32)] * (3 * len(DILATIONS)),
        compiler_params=_params(("arbitrary", "arbitrary")),
        name="dilated",
    )(zq, zk, zv)


def _out_router_kernel(alpha, x_ref, ya_ref, yb_ref, ym_ref, wo_ref, nbg_ref, g_ref, b_ref,
                       wrh_ref, wrl_ref, br_ref,
                       x1_ref, gate_ref, meta_ref, cnt_ref, base):
    step = pl.program_id(0)
    tm = OUT_SUB_ROWS
    a_w = ya_ref.shape[1]
    b_w = yb_ref.shape[1]
    n_exp = br_ref.shape[1]

    @pl.when(step == 0)
    def _():
        base[...] = jnp.zeros_like(base)

    lane = lax.broadcasted_iota(I32, (tm, n_exp), 1).astype(F32)
    t_i = lax.broadcasted_iota(I32, (tm, tm), 0)
    t_j = lax.broadcasted_iota(I32, (tm, tm), 1)
    earlier = (t_j < t_i).astype(BF16)
    k_lane = lax.broadcasted_iota(I32, (tm, TOP_K), 1)
    s_lane = lax.broadcasted_iota(I32, (tm, HEAD_DIM), 1)
    running = base[...]

    for h in range(x_ref.shape[0] // tm):
        rows = slice(h * tm, (h + 1) * tm)
        yb = _rms_norm(yb_ref[rows, :], nbg_ref[...]).astype(BF16)
        y = jnp.concatenate([ya_ref[rows, :], yb, ym_ref[rows, :]], axis=1)
        mix = jnp.dot(y, wo_ref[...], preferred_element_type=F32)
        x1 = _layer_norm(alpha * x_ref[rows, :] + mix, g_ref[...], b_ref[...])
        x1_ref[rows, :] = x1

        x_hi = x1.astype(BF16)
        x_lo = (x1 - x_hi.astype(F32)).astype(BF16)
        logits = (jnp.dot(x_hi, wrh_ref[...], preferred_element_type=F32)
                  + jnp.dot(x_lo, wrh_ref[...], preferred_element_type=F32)
                  + jnp.dot(x_hi, wrl_ref[...], preferred_element_type=F32)) + br_ref[...]

        work = logits
        vals, sels = [], []
        for _ in range(TOP_K):
            m = jnp.max(work, axis=-1, keepdims=True)
            sel = jnp.min(jnp.where(work == m, lane, float(n_exp)), axis=-1, keepdims=True)
            vals.append(m)
            sels.append(sel)
            work = jnp.where(lane == sel, -jnp.inf, work)
        exps = [jnp.exp(v - vals[0]) for v in vals]
        denom = exps[0] + exps[1] + exps[2] + exps[3]

        hot = [lane == s for s in sels]
        multi = sum(h_.astype(F32) for h_ in hot)
        before = jnp.dot(earlier, multi.astype(BF16), preferred_element_type=F32) + running
        gate_out = jnp.zeros((tm, TOP_K), F32)
        slots = jnp.zeros((tm, HEAD_DIM), F32)
        for k in range(TOP_K):
            rank_k = jnp.sum(jnp.where(hot[k], before, 0.0), axis=-1, keepdims=True)
            gate_out = jnp.where(k_lane == k, exps[k] / denom, gate_out)
            slots = jnp.where(s_lane == k, sels[k], slots)
            slots = jnp.where(s_lane == TOP_K + k, rank_k, slots)
        gate_ref[rows, :] = gate_out
        meta_ref[:, rows] = jnp.transpose(slots)[:2 * TOP_K, :].astype(I32)
        running = running + jnp.sum(multi, axis=0, keepdims=True)

    base[...] = running
    cnt_ref[...] = running.astype(I32)


def _out_router(alpha, x2, ya, yb, ym, w_out_bf, norm_b_g, ln_g, ln_b, wr_hi, wr_lo, b_router):
    t, d = x2.shape
    tm = OUT_ROWS
    n_exp = b_router.shape[1]
    row = lambda w: pl.BlockSpec((tm, w), lambda i: (i, 0))
    return pl.pallas_call(
        functools.partial(_out_router_kernel, alpha),
        grid=(t // tm,),
        in_specs=[row(d), row(ya.shape[1]), row(yb.shape[1]), row(ym.shape[1])] + [_full_vmem()] * 7,
        out_specs=[row(d), row(TOP_K), pl.BlockSpec((2 * TOP_K, tm), lambda i: (0, i)),
                   pl.BlockSpec((1, n_exp), lambda i: (0, 0))],
        out_shape=[jax.ShapeDtypeStruct((t, d), F32),
                   jax.ShapeDtypeStruct((t, TOP_K), F32),
                   jax.ShapeDtypeStruct((2 * TOP_K, t), I32),
                   jax.ShapeDtypeStruct((1, n_exp), I32)],
        scratch_shapes=[pltpu.VMEM((1, n_exp), F32)],
        compiler_params=_params(("arbitrary",)),
        name="out_router",
    )(x2, ya, yb, ym, w_out_bf, norm_b_g, ln_g, ln_b, wr_hi, wr_lo, b_router)


def _dispatch_kernel(ztile_ref, total_ref, dest_ref, x_ref, xs_hbm, zeros, sem, zsem):
    tm, d = x_ref.shape
    sub = zeros.shape[0]
    n_tiles = xs_hbm.shape[0] // sub

    @pl.when(pl.program_id(0) == 0)
    def _():
        zeros[...] = jnp.zeros_like(zeros)

        def fill(first_row):
            return pltpu.make_async_copy(zeros, xs_hbm.at[pl.ds(pl.multiple_of(first_row, sub), sub), :], zsem)

        def partial_tiles(run):
            def step(e, c):
                @pl.when(ztile_ref[e] >= 0)
                def _():
                    run(fill(ztile_ref[e]))
                return c
            lax.fori_loop(0, ztile_ref.shape[0], step, 0)

        def tail_tiles(run):
            def step(j, c):
                run(fill(j * sub))
                return c
            lax.fori_loop(total_ref[0] // sub, n_tiles, step, 0)

        partial_tiles(lambda copy: copy.start())
        tail_tiles(lambda copy: copy.start())
        partial_tiles(lambda copy: copy.wait())
        tail_tiles(lambda copy: copy.wait())

    def row_copy(t, slot):
        return pltpu.make_async_copy(x_ref.at[pl.ds(t, 1), :], xs_hbm.at[pl.ds(slot, 1), :], sem)

    for t in range(tm):
        for k in range(TOP_K):
            row_copy(t, dest_ref[k, t]).start()
    for _ in range(TOP_K):
        pltpu.make_async_copy(x_ref, xs_hbm.at[pl.ds(0, tm), :], sem).wait()


def _dispatch(x1, dest_flat, ztile, total_rows, n_rows):
    t, d = x1.shape
    tm = DISPATCH_ROWS
    grid_spec = pltpu.PrefetchScalarGridSpec(
        num_scalar_prefetch=2,
        grid=(t // tm,),
        in_specs=[pl.BlockSpec((TOP_K, tm), lambda i, z, n: (0, i), memory_space=pltpu.SMEM),
                  pl.BlockSpec((tm, d), lambda i, z, n: (i, 0))],
        out_specs=pl.BlockSpec(memory_space=pl.ANY),
        scratch_shapes=[pltpu.VMEM((EXPERT_SUB, d), F32),
                        pltpu.SemaphoreType.DMA, pltpu.SemaphoreType.DMA],
    )
    return pl.pallas_call(
        _dispatch_kernel,
        grid_spec=grid_spec,
        out_shape=jax.ShapeDtypeStruct((n_rows, d), F32),
        compiler_params=_params(("arbitrary",)),
        name="dispatch",
    )(ztile, total_rows, dest_flat, x1)


def _experts_kernel(nf, sbe_ref, sbr_ref, sbf_ref, na_ref, xs_hbm, wg_ref, wu_ref, bg_ref, bu_ref,
                    wd_ref, bd_ref, o_ref, xb, act, stage, wgu, wdn, sem):
    g = pl.program_id(0)
    s = pl.program_id(1)
    nn = pl.num_programs(1) - nf
    rows_max = xb.shape[0]
    sub = stage.shape[1]
    n_chunks = rows_max // sub
    n_sub = sbr_ref[g] // sub
    nxt = jnp.minimum(g + 1, pl.num_programs(0) - 1)
    n_next = jnp.where(g + 1 < pl.num_programs(0), sbr_ref[nxt], 0) // sub

    def chunk_copy(sb, c, slot):
        first = pl.multiple_of(sbf_ref[sb] + c * sub, sub)
        return pltpu.make_async_copy(xs_hbm.at[pl.ds(first, sub), :], stage.at[slot], sem.at[slot])

    def land(c, slot):
        chunk_copy(0, 0, slot).wait()
        xb[pl.ds(pl.multiple_of(c * sub, sub), sub), :] = stage[slot].astype(BF16)

    def row_tiles(body):
        n_big = n_sub // 8

        def big(j, c):
            body(pl.multiple_of(j * (8 * sub), 8 * sub), 8 * sub)
            return c

        lax.fori_loop(0, n_big, big, 0)
        done = n_big * 8
        for size in (4, 2, 1):
            take = (n_sub - done) >= size

            @pl.when(take)
            def _(done=done, size=size):
                body(pl.multiple_of(done * sub, size * sub), size * sub)

            done = done + jnp.where(take, size, 0)

    @pl.when(g >= na_ref[0])
    def _():
        o_ref[...] = jnp.zeros_like(o_ref)

    @pl.when(g < na_ref[0])
    def _():
        @pl.when((g == 0) & (s == 0))
        def _():
            def fetch(c, carry):
                chunk_copy(0, c, 0).start()
                land(c, 0)
                return carry
            lax.fori_loop(0, n_sub, fetch, 0)

        for slot in range(2):
            @pl.when((g > 0) & (s == 0) & (nn - 1 + slot * nn < n_sub))
            def _(slot=slot):
                land(nn - 1 + slot * nn, slot)

        @pl.when(s < nf)
        def _():
            wgu[0] = wg_ref[...].astype(BF16)
            wgu[1] = wu_ref[...].astype(BF16)

            def body(first, rows):
                x = xb[pl.ds(first, rows), :]
                gate = jnp.dot(x, wgu[0], preferred_element_type=F32) + bg_ref[...]
                up = jnp.dot(x, wgu[1], preferred_element_type=F32) + bu_ref[...]
                gate = jnp.minimum(gate, SWIGLU_LIMIT)
                up = jnp.clip(up, -SWIGLU_LIMIT, SWIGLU_LIMIT)
                a = (up + 1.0) * (gate * jax.nn.sigmoid(SWIGLU_ALPHA * gate))
                act[s, pl.ds(first, rows), :] = a.astype(BF16)

            row_tiles(body)

        @pl.when(s >= nf)
        def _():
            k = s - nf
            for slot in range(2):
                @pl.when((k >= 1) & (k - 1 + slot * nn < n_next))
                def _(slot=slot):
                    land(k - 1 + slot * nn, slot)

                @pl.when(k + slot * nn < n_next)
                def _(slot=slot):
                    chunk_copy(nxt, k + slot * nn, slot).start()

            wdn[...] = wd_ref[...].astype(BF16)

            def body(first, rows):
                a = jnp.concatenate([act[j, pl.ds(first, rows), :] for j in range(nf)], axis=1)
                o_ref[pl.ds(first, rows), :] = jnp.dot(a, wdn[...], preferred_element_type=F32) + bd_ref[...]

            row_tiles(body)

            def clear(c, carry):
                o_ref[pl.ds(pl.multiple_of(c * sub, sub), sub), :] = jnp.zeros((sub, o_ref.shape[1]), F32)
                return carry
            lax.fori_loop(n_sub, n_chunks, clear, 0)


def _experts(xs, sb_expert, sb_rows, sb_first, n_active, n_super, w_gate_up, b_gate_up, w_down, b_down):
    d = xs.shape[1]
    n_exp, _, two_ff = w_gate_up.shape
    d_ff = two_ff // 2
    tm, tf, tn, sub = EXPERT_ROWS, EXPERT_FF_TILE, EXPERT_OUT_TILE, EXPERT_SUB
    nf, nn = d_ff // tf, d // tn
    assert 2 * nn >= tm // sub, "two row chunks per down-projection step must cover a super-block"
    bgu = b_gate_up.reshape(n_exp, 1, two_ff)
    bdn = b_down.reshape(n_exp, 1, d)

    def expert(g, sbe, na):
        return sbe[jnp.minimum(g, na[0] - 1)]

    def ff_step(g, s, na):
        return jnp.where(g < na[0], jnp.minimum(s, nf - 1), nf - 1)

    def down_step(g, s, na):
        return jnp.where(g < na[0], jnp.maximum(s - nf, 0), nn - 1)

    in_specs = [
        pl.BlockSpec(memory_space=pl.ANY),
        pl.BlockSpec((None, d, tf), lambda g, s, sbe, sbr, sbf, na: (expert(g, sbe, na), 0, ff_step(g, s, na))),
        pl.BlockSpec((None, d, tf), lambda g, s, sbe, sbr, sbf, na: (expert(g, sbe, na), 0, nf + ff_step(g, s, na))),
        pl.BlockSpec((None, 1, tf), lambda g, s, sbe, sbr, sbf, na: (expert(g, sbe, na), 0, ff_step(g, s, na))),
        pl.BlockSpec((None, 1, tf), lambda g, s, sbe, sbr, sbf, na: (expert(g, sbe, na), 0, nf + ff_step(g, s, na))),
        pl.BlockSpec((None, d_ff, tn), lambda g, s, sbe, sbr, sbf, na: (expert(g, sbe, na), 0, down_step(g, s, na))),
        pl.BlockSpec((None, 1, tn), lambda g, s, sbe, sbr, sbf, na: (expert(g, sbe, na), 0, down_step(g, s, na))),
    ]
    grid_spec = pltpu.PrefetchScalarGridSpec(
        num_scalar_prefetch=4,
        grid=(n_super, nf + nn),
        in_specs=in_specs,
        out_specs=pl.BlockSpec((tm, tn), lambda g, s, sbe, sbr, sbf, na: (g, jnp.maximum(s - nf, 0))),
        scratch_shapes=[pltpu.VMEM((tm, d), BF16), pltpu.VMEM((nf, tm, tf), BF16),
                        pltpu.VMEM((2, sub, d), F32), pltpu.VMEM((2, d, tf), BF16),
                        pltpu.VMEM((d_ff, tn), BF16), pltpu.SemaphoreType.DMA((2,))],
    )
    return pl.pallas_call(
        functools.partial(_experts_kernel, nf),
        grid_spec=grid_spec,
        out_shape=jax.ShapeDtypeStruct((n_super * tm, d), F32),
        compiler_params=_params(("arbitrary", "arbitrary")),
        name="experts",
    )(sb_expert, sb_rows, sb_first, n_active, xs, w_gate_up, w_gate_up, bgu, bgu, w_down, bdn)


def _combine_kernel(alpha, dest_ref, x1_ref, gate_ref, g_ref, b_ref, ys_hbm, o_ref, rows, sem):
    tm, d = x1_ref.shape

    def row_copy(k, t, slot):
        return pltpu.make_async_copy(ys_hbm.at[pl.ds(slot, 1), :], rows.at[k, pl.ds(t, 1), :], sem)

    for t in range(tm):
        for k in range(TOP_K):
            row_copy(k, t, dest_ref[k, t]).start()
    for k in range(TOP_K):
        pltpu.make_async_copy(ys_hbm.at[pl.ds(0, tm), :], rows.at[k], sem).wait()

    gates = gate_ref[...]
    ffn = gates[:, 0:1] * rows[0]
    for k in range(1, TOP_K):
        ffn += gates[:, k:k + 1] * rows[k]
    o_ref[...] = _layer_norm(alpha * x1_ref[...] + ffn, g_ref[...], b_ref[...])


def _combine(alpha, dest_flat, x1, gates, ln_g, ln_b, ys):
    t, d = x1.shape
    tm = COMBINE_ROWS
    return pl.pallas_call(
        functools.partial(_combine_kernel, alpha),
        grid=(t // tm,),
        in_specs=[pl.BlockSpec((TOP_K, tm), lambda i: (0, i), memory_space=pltpu.SMEM),
                  pl.BlockSpec((tm, d), lambda i: (i, 0)),
                  pl.BlockSpec((tm, TOP_K), lambda i: (i, 0)),
                  _full_vmem(), _full_vmem(),
                  pl.BlockSpec(memory_space=pl.ANY)],
        out_specs=pl.BlockSpec((tm, d), lambda i: (i, 0)),
        out_shape=jax.ShapeDtypeStruct((t, d), F32),
        scratch_shapes=[pltpu.VMEM((TOP_K, tm, d), F32), pltpu.SemaphoreType.DMA],
        compiler_params=_params(("arbitrary",)),
        name="combine",
    )(dest_flat, x1, gates, ln_g, ln_b, ys)


def _routing_tables(meta, counts, n_super):
    tm, sub = EXPERT_ROWS, EXPERT_SUB
    n_exp = counts.shape[0]
    padded = (counts + sub - 1) // sub * sub
    row_end = jnp.cumsum(padded)
    row_start = row_end - padded
    per_expert = (padded + tm - 1) // tm
    sb_end = jnp.cumsum(per_expert)
    sb_start = sb_end - per_expert
    idx, rank = meta[:TOP_K], meta[TOP_K:]
    hit = idx[None] == jnp.arange(n_exp, dtype=I32)[:, None, None]
    dest_x = jnp.sum(jnp.where(hit, row_start[:, None, None], 0), axis=0) + rank
    dest_y = jnp.sum(jnp.where(hit, (sb_start * tm)[:, None, None], 0), axis=0) + rank
    dest_x, dest_y = dest_x.astype(I32), dest_y.astype(I32)
    ztile = jnp.where(counts % sub != 0, row_start + counts // sub * sub, -1).astype(I32)
    g = jnp.arange(n_super, dtype=I32)
    owner = jnp.sum((g[:, None] >= sb_end[None, :]).astype(I32), axis=1)
    sb_expert = jnp.minimum(owner, n_exp - 1).astype(I32)
    local = (g - sb_start[sb_expert]) * tm
    active = g < sb_end[-1]
    sb_rows = jnp.where(active, jnp.clip(padded[sb_expert] - local, 0, tm), 0).astype(I32)
    sb_first = jnp.where(active, row_start[sb_expert] + local, 0).astype(I32)
    return (dest_x, dest_y, ztile, row_end[-1:].astype(I32), sb_expert, sb_rows, sb_first,
            sb_end[-1:].astype(I32))


def _layer(alpha, x, mem, w_in, w_spatial, b_spatial, a_ln_g, a_ln_b, w_mem_kv, norm_a_g, norm_b_g,
           norm_m_g, w_out, ln1_g, ln1_b, w_router, b_router, w_gate_up, b_gate_up, w_down, b_down,
           ln2_g, ln2_b):
    batch, seq, d = x.shape
    t = batch * seq
    m_width = w_mem_kv.shape[1] // 2
    a_width = a_ln_g.shape[0]
    b_width = norm_b_g.shape[0]
    n_exp = w_router.shape[1]
    row = lambda v: v.reshape(1, -1)

    x2 = x.reshape(t, d)
    za, zq, zk, zv, zm = _in_proj(x2, w_in.astype(BF16), (2 * a_width, b_width, b_width, b_width, m_width))
    kv = _kv_proj(mem, w_mem_kv.astype(BF16))
    ya, ym = _local_mix(za, zm, kv, w_spatial, b_spatial.T, row(a_ln_g), row(a_ln_b),
                        row(norm_a_g), row(norm_m_g), batch, seq)
    yb = _dilated_attention(zq, zk, zv, batch, seq)

    wr_hi = w_router.astype(BF16)
    wr_lo = (w_router - wr_hi.astype(F32)).astype(BF16)
    x1, gates, meta, counts = _out_router(
        alpha, x2, ya, yb, ym, w_out.astype(BF16), row(norm_b_g), row(ln1_g), row(ln1_b),
        wr_hi, wr_lo, row(b_router))

    n_assign = t * TOP_K
    n_super = (n_assign + n_exp * (EXPERT_ROWS - 1)) // EXPERT_ROWS
    n_x_rows = (n_assign + n_exp * (EXPERT_SUB - 1)) // EXPERT_SUB * EXPERT_SUB
    dest_x, dest_y, ztile, total_rows, sb_expert, sb_rows, sb_first, n_active = _routing_tables(
        meta, counts.reshape(n_exp), n_super)
    xs = _dispatch(x1, dest_x, ztile, total_rows, n_x_rows)
    ys = _experts(xs, sb_expert, sb_rows, sb_first, n_active, n_super, w_gate_up, b_gate_up, w_down, b_down)
    out = _combine(alpha, dest_y, x1, gates, row(ln2_g), row(ln2_b), ys)
    return out.reshape(batch, seq, d)


def kernel(x, mem, w_in, w_spatial, b_spatial, a_ln_g, a_ln_b, w_mem_kv, norm_a_g, norm_b_g, norm_m_g, w_out, ln1_g, ln1_b, w_router, b_router, w_gate_up, b_gate_up, w_down, b_down, ln2_g, ln2_b):
    depth = w_in.shape[0]
    alpha = (2.0 * depth) ** 0.25
    for l in range(depth):
        x = _layer(alpha, x, mem, w_in[l], w_spatial[l], b_spatial[l], a_ln_g[l], a_ln_b[l], w_mem_kv[l],
                   norm_a_g[l], norm_b_g[l], norm_m_g[l], w_out[l], ln1_g[l], ln1_b[l], w_router[l],
                   b_router[l], w_gate_up[l], b_gate_up[l], w_down[l], b_down[l], ln2_g[l], ln2_b[l])
    return x
```

```python
import functools

import jax
import jax.numpy as jnp
from jax import lax
from jax.experimental import pallas as pl
from jax.experimental.pallas import tpu as pltpu

F32 = jnp.float32
BF16 = jnp.bfloat16
I32 = jnp.int32
U32 = jnp.uint32

HEAD_DIM = 128
CHUNK = 128
ATT_BLOCK = 128
DILATIONS = (1, 4, 16)
TOP_K = 4
SWIGLU_LIMIT = 7.0
SWIGLU_ALPHA = 1.702
LN_EPS = 1e-5
NEG_INF = -1e30

V7X_VMEM_LIMIT_BYTES = 56 * 1024 * 1024

IN_PROJ_ROWS = 256
LOCAL_ROWS = 512
OUT_ROWS = 512
OUT_SUB_ROWS = 256
DISPATCH_ROWS = 256
COMBINE_ROWS = 256
DILATED_FILL_ROWS = 256
DILATED_GROUP = 16
EXPERT_ROWS = 3072
EXPERT_SUB = 256
EXPERT_FF_TILE = 256
EXPERT_OUT_TILE = 256


def _params(semantics):
    return pltpu.CompilerParams(dimension_semantics=semantics,
                                vmem_limit_bytes=V7X_VMEM_LIMIT_BYTES)


def _full_vmem():
    return pl.BlockSpec(memory_space=pltpu.VMEM)


def _layer_norm(h, g, b):
    mu = jnp.mean(h, axis=-1, keepdims=True)
    hc = h - mu
    var = jnp.mean(hc * hc, axis=-1, keepdims=True)
    return hc * lax.rsqrt(var + LN_EPS) * g + b


def _rms_norm(y, g):
    return y * lax.rsqrt(jnp.mean(y * y, axis=-1, keepdims=True) + LN_EPS) * g


def _gelu_tanh(x):
    c = 0.7978845608028654
    return x * (0.5 * (1.0 + jnp.tanh(c * (x + 0.044715 * (x * x * x)))))


def _dot_nt(a, b):
    return lax.dot_general(a, b, (((1,), (1,)), ((), ())), preferred_element_type=F32)


def _in_proj_kernel(bounds, x_ref, w_ref, *out_refs):
    xb = x_ref[...].astype(BF16)
    for o_ref, (lo, hi) in zip(out_refs, bounds):
        o_ref[...] = jnp.dot(xb, w_ref[:, lo:hi], preferred_element_type=F32)


def _in_proj(x2, w_in_bf, widths):
    t, d = x2.shape
    tm = IN_PROJ_ROWS
    bounds, lo = [], 0
    for w in widths:
        bounds.append((lo, lo + w))
        lo += w
    return pl.pallas_call(
        functools.partial(_in_proj_kernel, tuple(bounds)),
        grid=(t // tm,),
        in_specs=[pl.BlockSpec((tm, d), lambda i: (i, 0)), _full_vmem()],
        out_specs=[pl.BlockSpec((tm, w), lambda i: (i, 0)) for w in widths],
        out_shape=[jax.ShapeDtypeStruct((t, w), F32) for w in widths],
        compiler_params=_params(("arbitrary",)),
        name="in_proj",
    )(x2, w_in_bf)


def _kv_proj_kernel(mem_ref, w_ref, kv_ref):
    kv_ref[...] = jnp.dot(mem_ref[...].astype(BF16), w_ref[...],
                          preferred_element_type=F32).astype(BF16)


def _kv_proj(mem, w_kv_bf):
    b, n_mem, d = mem.shape
    n = w_kv_bf.shape[1]
    return pl.pallas_call(
        _kv_proj_kernel,
        grid=(b,),
        in_specs=[pl.BlockSpec((None, n_mem, d), lambda i: (i, 0, 0)), _full_vmem()],
        out_specs=pl.BlockSpec((None, n_mem, n), lambda i: (i, 0, 0)),
        out_shape=jax.ShapeDtypeStruct((b, n_mem, n), BF16),
        compiler_params=_params(("arbitrary",)),
        name="kv_proj",
    )(mem, w_kv_bf)


def _local_mix_kernel(za_ref, zm_ref, kv_ref, wsp_ref, bsp_ref, lng_ref, lnb_ref, nag_ref, nmg_ref,
                      ya_ref, ym_ref, ya_acc, ym_acc):
    rows = za_ref.shape[0]
    a_width = za_ref.shape[1] // 2
    n_groups = a_width // HEAD_DIM
    n_chunks = rows // CHUNK
    m_width = zm_ref.shape[1]
    m_heads = m_width // HEAD_DIM

    act = _gelu_tanh(za_ref[...])
    u = act[:, :a_width]
    v = _layer_norm(act[:, a_width:], lng_ref[...], lnb_ref[...])
    t_idx = lax.broadcasted_iota(I32, (CHUNK, CHUNK), 0)
    s_idx = lax.broadcasted_iota(I32, (CHUNK, CHUNK), 1)
    causal = s_idx <= t_idx
    bsp = bsp_ref[...]
    for g in range(n_groups):
        cols = slice(g * HEAD_DIM, (g + 1) * HEAD_DIM)
        w_g = jnp.where(causal, wsp_ref[g], 0.0).astype(BF16)
        v_g = jnp.concatenate([v[c * CHUNK:(c + 1) * CHUNK, cols] for c in range(n_chunks)], axis=1)
        mixed = jnp.dot(w_g, v_g.astype(BF16), preferred_element_type=F32) + bsp[:, g:g + 1]
        for c in range(n_chunks):
            rs = slice(c * CHUNK, (c + 1) * CHUNK)
            ya_acc[rs, cols] = u[rs, cols] * mixed[:, c * HEAD_DIM:(c + 1) * HEAD_DIM]
    ya_ref[...] = _rms_norm(ya_acc[...], nag_ref[...]).astype(BF16)

    scale = HEAD_DIM ** -0.5
    q = zm_ref[...]
    for h in range(m_heads):
        cols = slice(h * HEAD_DIM, (h + 1) * HEAD_DIM)
        k_h = kv_ref[:, h * HEAD_DIM:(h + 1) * HEAD_DIM]
        v_h = kv_ref[:, m_width + h * HEAD_DIM:m_width + (h + 1) * HEAD_DIM]
        s = _dot_nt(q[:, cols].astype(BF16), k_h) * scale
        p = jnp.exp(s - jnp.max(s, axis=-1, keepdims=True))
        l = jnp.sum(p, axis=-1, keepdims=True)
        ym_acc[:, cols] = jnp.dot(p.astype(BF16), v_h, preferred_element_type=F32) / l
    ym_ref[...] = _rms_norm(ym_acc[...], nmg_ref[...]).astype(BF16)


def _local_mix(za, zm, kv, w_spatial, b_spatial_t, a_ln_g, a_ln_b, norm_a_g, norm_m_g, batch, seq):
    rows = LOCAL_ROWS
    a2 = za.shape[1]
    a_width = a2 // 2
    m_width = zm.shape[1]
    n_mem, kv_w = kv.shape[1], kv.shape[2]
    tiles = seq // rows
    return pl.pallas_call(
        _local_mix_kernel,
        grid=(batch, tiles),
        in_specs=[
            pl.BlockSpec((rows, a2), lambda b, i: (b * tiles + i, 0)),
            pl.BlockSpec((rows, m_width), lambda b, i: (b * tiles + i, 0)),
            pl.BlockSpec((None, n_mem, kv_w), lambda b, i: (b, 0, 0)),
            _full_vmem(), _full_vmem(), _full_vmem(), _full_vmem(), _full_vmem(), _full_vmem(),
        ],
        out_specs=[pl.BlockSpec((rows, a_width), lambda b, i: (b * tiles + i, 0)),
                   pl.BlockSpec((rows, m_width), lambda b, i: (b * tiles + i, 0))],
        out_shape=[jax.ShapeDtypeStruct((batch * seq, a_width), BF16),
                   jax.ShapeDtypeStruct((batch * seq, m_width), BF16)],
        scratch_shapes=[pltpu.VMEM((rows, a_width), F32), pltpu.VMEM((rows, m_width), F32)],
        compiler_params=_params(("arbitrary", "arbitrary")),
        name="local_mix",
    )(za, zm, kv, w_spatial, b_spatial_t, a_ln_g, a_ln_b, norm_a_g, norm_m_g)


def _dilated_kernel(q_ref, k_ref, v_ref, o_ref, qs, ks, vs, *state):
    seq = q_ref.shape[0]
    scale = HEAD_DIM ** -0.5
    blk = ATT_BLOCK
    n_pat = len(DILATIONS)
    m_s, l_s, a_s = state[:n_pat], state[n_pat:2 * n_pat], state[2 * n_pat:]
    qi = lax.broadcasted_iota(I32, (blk, 2 * blk), 0)
    kj = lax.broadcasted_iota(I32, (blk, 2 * blk), 1)
    band = (kj >= qi) & (kj <= qi + blk)
    zero_blk = jnp.zeros((blk, HEAD_DIM), BF16)
    vs[:, HEAD_DIM:] = jnp.ones((vs.shape[0], HEAD_DIM), BF16)

    for p, d in enumerate(DILATIONS):
        sub = seq // d
        n_blocks = sub // blk
        chunk = min(sub, DILATED_FILL_ROWS)

        def fill(i, c, d=d, sub=sub, chunk=chunk):
            r = i // (sub // chunk)
            j = i % (sub // chunk)
            first = r + j * (chunk * d)
            src = pl.ds(first, chunk) if d == 1 else pl.ds(first, chunk, stride=d)
            dst_q = pl.multiple_of(r * sub + j * chunk, blk)
            dst_kv = pl.multiple_of(r * (sub + blk) + blk + j * chunk, blk)
            qs[pl.ds(dst_q, chunk), :] = q_ref[src, :].astype(BF16)
            ks[pl.ds(dst_kv, chunk), :] = k_ref[src, :].astype(BF16)
            vs[pl.ds(dst_kv, chunk), :HEAD_DIM] = v_ref[src, :].astype(BF16)
            return c

        def pad(r, c, sub=sub):
            at = pl.multiple_of(r * (sub + blk), blk)
            ks[pl.ds(at, blk), :] = zero_blk
            vs[pl.ds(at, blk), :HEAD_DIM] = zero_blk
            return c

        lax.fori_loop(0, d, pad, 0)
        lax.fori_loop(0, d * (sub // chunk), fill, 0)

        def block(c, d=d, n_blocks=n_blocks, p=p):
            r = c // n_blocks
            n = c % n_blocks
            q = qs[pl.ds(pl.multiple_of(c * blk, blk), blk), :]
            win = pl.ds(pl.multiple_of((c + r) * blk, blk), 2 * blk)
            s = _dot_nt(q, ks[win, :]) * scale
            valid = band & (kj >= jnp.where(n > 0, 0, blk))
            s = jnp.where(valid, s, NEG_INF)
            m = jnp.max(s, axis=-1, keepdims=True)
            e = jnp.exp(s - m)
            acc = jnp.dot(e.astype(BF16), vs[win, :], preferred_element_type=F32)
            first = n * (blk * d) + r
            out = pl.ds(first, blk) if d == 1 else pl.ds(first, blk, stride=d)
            m_s[p][out, :] = jnp.broadcast_to(m, (blk, HEAD_DIM))
            l_s[p][out, :] = acc[:, HEAD_DIM:]
            a_s[p][out, :] = acc[:, :HEAD_DIM]

        def group(i, c, block=block):
            for g in range(DILATED_GROUP):
                block(i * DILATED_GROUP + g)
            return c

        lax.fori_loop(0, seq // blk // DILATED_GROUP, group, 0)

    def merge(j, c):
        rows = pl.ds(pl.multiple_of(j * blk, blk), blk)
        ms = [m[rows, :] for m in m_s]
        m_all = functools.reduce(jnp.maximum, ms)
        ws = [jnp.exp(m - m_all) for m in ms]
        den = sum(w * l[rows, :] for w, l in zip(ws, l_s))
        num = sum(w * a[rows, :] for w, a in zip(ws, a_s))
        o_ref[rows, :] = num / den
        return c

    lax.fori_loop(0, seq // blk, merge, 0)


def _dilated_attention(zq, zk, zv, batch, seq):
    width = zq.shape[1]
    heads = width // HEAD_DIM
    spec = pl.BlockSpec((seq, HEAD_DIM), lambda b, h: (b, h))
    kv_rows = seq + max(DILATIONS) * ATT_BLOCK
    return pl.pallas_call(
        _dilated_kernel,
        grid=(batch, heads),
        in_specs=[spec, spec, spec],
        out_specs=spec,
        out_shape=jax.ShapeDtypeStruct((batch * seq, width), F32),
        scratch_shapes=[pltpu.VMEM((seq, HEAD_DIM), BF16), pltpu.VMEM((kv_rows, HEAD_DIM), BF16),
                        pltpu.VMEM((kv_rows, 2 * HEAD_DIM), BF16)]
                       + [pltpu.VMEM((seq, HEAD_DIM), F32)] * (3 * len(DILATIONS)),
        compiler_params=_params(("arbitrary", "arbitrary")),
        name="dilated",
    )(zq, zk, zv)


def _out_router_kernel(alpha, x_ref, ya_ref, yb_ref, ym_ref, wo_ref, nbg_ref, g_ref, b_ref,
                       wrh_ref, wrl_ref, br_ref,
                       x1_ref, gate_ref, meta_ref, cnt_ref, base):
    step = pl.program_id(0)
    tm = OUT_SUB_ROWS
    a_w = ya_ref.shape[1]
    b_w = yb_ref.shape[1]
    n_exp = br_ref.shape[1]

    @pl.when(step == 0)
    def _():
        base[...] = jnp.zeros_like(base)

    lane = lax.broadcasted_iota(I32, (tm, n_exp), 1).astype(F32)
    t_i = lax.broadcasted_iota(I32, (tm, tm), 0)
    t_j = lax.broadcasted_iota(I32, (tm, tm), 1)
    earlier = (t_j < t_i).astype(BF16)
    k_lane = lax.broadcasted_iota(I32, (tm, TOP_K), 1)
    s_lane = lax.broadcasted_iota(I32, (tm, HEAD_DIM), 1)
    running = base[...]

    for h in range(x_ref.shape[0] // tm):
        rows = slice(h * tm, (h + 1) * tm)
        yb = _rms_norm(yb_ref[rows, :], nbg_ref[...]).astype(BF16)
        y = jnp.concatenate([ya_ref[rows, :], yb, ym_ref[rows, :]], axis=1)
        mix = jnp.dot(y, wo_ref[...], preferred_element_type=F32)
        x1 = _layer_norm(alpha * x_ref[rows, :] + mix, g_ref[...], b_ref[...])
        x1_ref[rows, :] = x1

        x_hi = x1.astype(BF16)
        x_lo = (x1 - x_hi.astype(F32)).astype(BF16)
        logits = (jnp.dot(x_hi, wrh_ref[...], preferred_element_type=F32)
                  + jnp.dot(x_lo, wrh_ref[...], preferred_element_type=F32)
                  + jnp.dot(x_hi, wrl_ref[...], preferred_element_type=F32)) + br_ref[...]

        work = logits
        vals, sels = [], []
        for _ in range(TOP_K):
            m = jnp.max(work, axis=-1, keepdims=True)
            sel = jnp.min(jnp.where(work == m, lane, float(n_exp)), axis=-1, keepdims=True)
            vals.append(m)
            sels.append(sel)
            work = jnp.where(lane == sel, -jnp.inf, work)
        exps = [jnp.exp(v - vals[0]) for v in vals]
        denom = exps[0] + exps[1] + exps[2] + exps[3]

        hot = [lane == s for s in sels]
        multi = sum(h_.astype(F32) for h_ in hot)
        before = jnp.dot(earlier, multi.astype(BF16), preferred_element_type=F32) + running
        gate_out = jnp.zeros((tm, TOP_K), F32)
        slots = jnp.zeros((tm, HEAD_DIM), F32)
        for k in range(TOP_K):
            rank_k = jnp.sum(jnp.where(hot[k], before, 0.0), axis=-1, keepdims=True)
            gate_out = jnp.where(k_lane == k, exps[k] / denom, gate_out)
            slots = jnp.where(s_lane == k, sels[k], slots)
            slots = jnp.where(s_lane == TOP_K + k, rank_k, slots)
        gate_ref[rows, :] = gate_out
        meta_ref[:, rows] = jnp.transpose(slots)[:2 * TOP_K, :].astype(I32)
        running = running + jnp.sum(multi, axis=0, keepdims=True)

    base[...] = running
    cnt_ref[...] = running.astype(I32)


def _out_router(alpha, x2, ya, yb, ym, w_out_bf, norm_b_g, ln_g, ln_b, wr_hi, wr_lo, b_router):
    t, d = x2.shape
    tm = OUT_ROWS
    n_exp = b_router.shape[1]
    row = lambda w: pl.BlockSpec((tm, w), lambda i: (i, 0))
    return pl.pallas_call(
        functools.partial(_out_router_kernel, alpha),
        grid=(t // tm,),
        in_specs=[row(d), row(ya.shape[1]), row(yb.shape[1]), row(ym.shape[1])] + [_full_vmem()] * 7,
        out_specs=[row(d), row(TOP_K), pl.BlockSpec((2 * TOP_K, tm), lambda i: (0, i)),
                   pl.BlockSpec((1, n_exp), lambda i: (0, 0))],
        out_shape=[jax.ShapeDtypeStruct((t, d), F32),
                   jax.ShapeDtypeStruct((t, TOP_K), F32),
                   jax.ShapeDtypeStruct((2 * TOP_K, t), I32),
                   jax.ShapeDtypeStruct((1, n_exp), I32)],
        scratch_shapes=[pltpu.VMEM((1, n_exp), F32)],
        compiler_params=_params(("arbitrary",)),
        name="out_router",
    )(x2, ya, yb, ym, w_out_bf, norm_b_g, ln_g, ln_b, wr_hi, wr_lo, b_router)


def _dispatch_kernel(ztile_ref, total_ref, dest_ref, x_ref, xs_hbm, zeros, rows, sem, zsem):
    tm, d = x_ref.shape
    sub = zeros.shape[0]
    n_tiles = xs_hbm.shape[0] // sub

    @pl.when(pl.program_id(0) == 0)
    def _():
        zeros[...] = jnp.zeros_like(zeros)

        def fill(first_row):
            return pltpu.make_async_copy(zeros, xs_hbm.at[pl.ds(pl.multiple_of(first_row, sub), sub), :], zsem)

        def partial_tiles(run):
            def step(e, c):
                @pl.when(ztile_ref[e] >= 0)
                def _():
                    run(fill(ztile_ref[e]))
                return c
            lax.fori_loop(0, ztile_ref.shape[0], step, 0)

        def tail_tiles(run):
            def step(j, c):
                run(fill(j * sub))
                return c
            lax.fori_loop(total_ref[0] // sub, n_tiles, step, 0)

        partial_tiles(lambda copy: copy.start())
        tail_tiles(lambda copy: copy.start())
        partial_tiles(lambda copy: copy.wait())
        tail_tiles(lambda copy: copy.wait())

    i = pl.program_id(0)
    buf = i % 2
    rows[buf] = x_ref[...]
    for t in range(tm):
        for k in range(TOP_K):
            pltpu.make_async_copy(rows.at[buf, pl.ds(t, 1), :], xs_hbm.at[pl.ds(dest_ref[k, t], 1), :],
                                  sem.at[buf]).start(priority=k % 2)

    def drain(b):
        for _ in range(TOP_K):
            pltpu.make_async_copy(rows.at[b], xs_hbm.at[pl.ds(0, tm), :], sem.at[b]).wait()

    @pl.when(i > 0)
    def _():
        drain(1 - buf)

    @pl.when(i == pl.num_programs(0) - 1)
    def _():
        drain(buf)


def _dispatch(x1, dest_flat, ztile, total_rows, n_rows):
    t, d = x1.shape
    tm = DISPATCH_ROWS
    grid_spec = pltpu.PrefetchScalarGridSpec(
        num_scalar_prefetch=2,
        grid=(t // tm,),
        in_specs=[pl.BlockSpec((TOP_K, tm), lambda i, z, n: (0, i), memory_space=pltpu.SMEM),
                  pl.BlockSpec((tm, d), lambda i, z, n: (i, 0))],
        out_specs=pl.BlockSpec(memory_space=pl.ANY),
        scratch_shapes=[pltpu.VMEM((EXPERT_SUB, d), F32), pltpu.VMEM((2, tm, d), F32),
                        pltpu.SemaphoreType.DMA((2,)), pltpu.SemaphoreType.DMA],
    )
    return pl.pallas_call(
        _dispatch_kernel,
        grid_spec=grid_spec,
        out_shape=jax.ShapeDtypeStruct((n_rows, d), F32),
        compiler_params=_params(("arbitrary",)),
        name="dispatch",
    )(ztile, total_rows, dest_flat, x1)


def _experts_kernel(nf, sbe_ref, sbr_ref, sbf_ref, na_ref, xs_hbm, wg_ref, wu_ref, bg_ref, bu_ref,
                    wd_ref, bd_ref, o_ref, xb, act, stage, sem):
    g = pl.program_id(0)
    s = pl.program_id(1)
    nn = pl.num_programs(1) - nf
    rows_max = xb.shape[0]
    sub = stage.shape[1]
    n_chunks = rows_max // sub
    n_sub = sbr_ref[g] // sub
    nxt = jnp.minimum(g + 1, pl.num_programs(0) - 1)
    n_next = jnp.where(g + 1 < pl.num_programs(0), sbr_ref[nxt], 0) // sub

    def chunk_copy(sb, c, slot):
        first = pl.multiple_of(sbf_ref[sb] + c * sub, sub)
        return pltpu.make_async_copy(xs_hbm.at[pl.ds(first, sub), :], stage.at[slot], sem.at[slot])

    def land(c, slot):
        chunk_copy(0, 0, slot).wait()
        xb[pl.ds(pl.multiple_of(c * sub, sub), sub), :] = stage[slot].astype(BF16)

    def row_tiles(body):
        n_big = n_sub // 8

        def big(j, c):
            body(pl.multiple_of(j * (8 * sub), 8 * sub), 8 * sub)
            return c

        lax.fori_loop(0, n_big, big, 0)
        done = n_big * 8
        for size in (4, 2, 1):
            take = (n_sub - done) >= size

            @pl.when(take)
            def _(done=done, size=size):
                body(pl.multiple_of(done * sub, size * sub), size * sub)

            done = done + jnp.where(take, size, 0)

    @pl.when(g >= na_ref[0])
    def _():
        o_ref[...] = jnp.zeros_like(o_ref)

    @pl.when(g < na_ref[0])
    def _():
        @pl.when((g == 0) & (s == 0))
        def _():
            def fetch(c, carry):
                chunk_copy(0, c, 0).start()
                land(c, 0)
                return carry
            lax.fori_loop(0, n_sub, fetch, 0)

        for slot in range(2):
            @pl.when((g > 0) & (s == 0) & (nn - 1 + slot * nn < n_sub))
            def _(slot=slot):
                land(nn - 1 + slot * nn, slot)

        @pl.when(s < nf)
        def _():
            def body(first, rows):
                x = xb[pl.ds(first, rows), :]
                gate = jnp.dot(x, wg_ref[...].astype(BF16), preferred_element_type=F32) + bg_ref[...]
                up = jnp.dot(x, wu_ref[...].astype(BF16), preferred_element_type=F32) + bu_ref[...]
                gate = jnp.minimum(gate, SWIGLU_LIMIT)
                up = jnp.clip(up, -SWIGLU_LIMIT, SWIGLU_LIMIT)
                a = (up + 1.0) * (gate * jax.nn.sigmoid(SWIGLU_ALPHA * gate))
                act[s, pl.ds(first, rows), :] = a.astype(BF16)

            row_tiles(body)

        @pl.when(s >= nf)
        def _():
            k = s - nf
            for slot in range(2):
                @pl.when((k >= 1) & (k - 1 + slot * nn < n_next))
                def _(slot=slot):
                    land(k - 1 + slot * nn, slot)

                @pl.when(k + slot * nn < n_next)
                def _(slot=slot):
                    chunk_copy(nxt, k + slot * nn, slot).start()

            def body(first, rows):
                a = jnp.concatenate([act[j, pl.ds(first, rows), :] for j in range(nf)], axis=1)
                o_ref[pl.ds(first, rows), :] = jnp.dot(a, wd_ref[...].astype(BF16),
                                                       preferred_element_type=F32) + bd_ref[...]

            row_tiles(body)

            def clear(c, carry):
                o_ref[pl.ds(pl.multiple_of(c * sub, sub), sub), :] = jnp.zeros((sub, o_ref.shape[1]), F32)
                return carry
            lax.fori_loop(n_sub, n_chunks, clear, 0)


def _experts(xs, sb_expert, sb_rows, sb_first, n_active, n_super, w_gate_up, b_gate_up, w_down, b_down):
    d = xs.shape[1]
    n_exp, _, two_ff = w_gate_up.shape
    d_ff = two_ff // 2
    tm, tf, tn, sub = EXPERT_ROWS, EXPERT_FF_TILE, EXPERT_OUT_TILE, EXPERT_SUB
    nf, nn = d_ff // tf, d // tn
    assert 2 * nn >= tm // sub, "two row chunks per down-projection step must cover a super-block"
    bgu = b_gate_up.reshape(n_exp, 1, two_ff)
    bdn = b_down.reshape(n_exp, 1, d)

    def expert(g, sbe, na):
        return sbe[jnp.minimum(g, na[0] - 1)]

    def ff_step(g, s, na):
        return jnp.where(g < na[0], jnp.minimum(s, nf - 1), nf - 1)

    def down_step(g, s, na):
        return jnp.where(g < na[0], jnp.maximum(s - nf, 0), nn - 1)

    in_specs = [
        pl.BlockSpec(memory_space=pl.ANY),
        pl.BlockSpec((None, d, tf), lambda g, s, sbe, sbr, sbf, na: (expert(g, sbe, na), 0, ff_step(g, s, na))),
        pl.BlockSpec((None, d, tf), lambda g, s, sbe, sbr, sbf, na: (expert(g, sbe, na), 0, nf + ff_step(g, s, na))),
        pl.BlockSpec((None, 1, tf), lambda g, s, sbe, sbr, sbf, na: (expert(g, sbe, na), 0, ff_step(g, s, na))),
        pl.BlockSpec((None, 1, tf), lambda g, s, sbe, sbr, sbf, na: (expert(g, sbe, na), 0, nf + ff_step(g, s, na))),
        pl.BlockSpec((None, d_ff, tn), lambda g, s, sbe, sbr, sbf, na: (expert(g, sbe, na), 0, down_step(g, s, na))),
        pl.BlockSpec((None, 1, tn), lambda g, s, sbe, sbr, sbf, na: (expert(g, sbe, na), 0, down_step(g, s, na))),
    ]
    grid_spec = pltpu.PrefetchScalarGridSpec(
        num_scalar_prefetch=4,
        grid=(n_super, nf + nn),
        in_specs=in_specs,
        out_specs=pl.BlockSpec((tm, tn), lambda g, s, sbe, sbr, sbf, na: (g, jnp.maximum(s - nf, 0))),
        scratch_shapes=[pltpu.VMEM((tm, d), BF16), pltpu.VMEM((nf, tm, tf), BF16),
                        pltpu.VMEM((2, sub, d), F32), pltpu.SemaphoreType.DMA((2,))],
    )
    return pl.pallas_call(
        functools.partial(_experts_kernel, nf),
        grid_spec=grid_spec,
        out_shape=jax.ShapeDtypeStruct((n_super * tm, d), F32),
        compiler_params=_params(("arbitrary", "arbitrary")),
        name="experts",
    )(sb_expert, sb_rows, sb_first, n_active, xs, w_gate_up, w_gate_up, bgu, bgu, w_down, bdn)


def _combine_kernel(alpha, dest_ref, next_ref, x1_ref, gate_ref, g_ref, b_ref, ys_hbm, o_ref, rows, sem):
    i = pl.program_id(0)
    tm, d = x1_ref.shape

    def gather(slots_ref, buf):
        for t in range(tm):
            for k in range(TOP_K):
                pltpu.make_async_copy(ys_hbm.at[pl.ds(slots_ref[k, t], 1), :],
                                      rows.at[buf, k, pl.ds(t, 1), :],
                                      sem.at[buf]).start(priority=k % 2)

    @pl.when(i == 0)
    def _():
        gather(dest_ref, 0)

    @pl.when(i + 1 < pl.num_programs(0))
    def _():
        gather(next_ref, (i + 1) % 2)

    buf = i % 2
    for k in range(TOP_K):
        pltpu.make_async_copy(ys_hbm.at[pl.ds(0, tm), :], rows.at[buf, k], sem.at[buf]).wait()

    gates = gate_ref[...]
    ffn = gates[:, 0:1] * rows[buf, 0]
    for k in range(1, TOP_K):
        ffn += gates[:, k:k + 1] * rows[buf, k]
    o_ref[...] = _layer_norm(alpha * x1_ref[...] + ffn, g_ref[...], b_ref[...])


def _combine(alpha, dest_flat, x1, gates, ln_g, ln_b, ys):
    t, d = x1.shape
    tm = COMBINE_ROWS
    return pl.pallas_call(
        functools.partial(_combine_kernel, alpha),
        grid=(t // tm,),
        in_specs=[pl.BlockSpec((TOP_K, tm), lambda i: (0, i), memory_space=pltpu.SMEM),
                  pl.BlockSpec((TOP_K, tm), lambda i: (0, jnp.minimum(i + 1, t // tm - 1)),
                               memory_space=pltpu.SMEM),
                  pl.BlockSpec((tm, d), lambda i: (i, 0)),
                  pl.BlockSpec((tm, TOP_K), lambda i: (i, 0)),
                  _full_vmem(), _full_vmem(),
                  pl.BlockSpec(memory_space=pl.ANY)],
        out_specs=pl.BlockSpec((tm, d), lambda i: (i, 0)),
        out_shape=jax.ShapeDtypeStruct((t, d), F32),
        scratch_shapes=[pltpu.VMEM((2, TOP_K, tm, d), F32), pltpu.SemaphoreType.DMA((2,))],
        compiler_params=_params(("arbitrary",)),
        name="combine",
    )(dest_flat, dest_flat, x1, gates, ln_g, ln_b, ys)


def _routing_tables(meta, counts, n_super):
    tm, sub = EXPERT_ROWS, EXPERT_SUB
    n_exp = counts.shape[0]
    padded = (counts + sub - 1) // sub * sub
    row_end = jnp.cumsum(padded)
    row_start = row_end - padded
    per_expert = (padded + tm - 1) // tm
    sb_end = jnp.cumsum(per_expert)
    sb_start = sb_end - per_expert
    idx, rank = meta[:TOP_K], meta[TOP_K:]
    hit = idx[None] == jnp.arange(n_exp, dtype=I32)[:, None, None]
    dest_x = jnp.sum(jnp.where(hit, row_start[:, None, None], 0), axis=0) + rank
    dest_y = jnp.sum(jnp.where(hit, (sb_start * tm)[:, None, None], 0), axis=0) + rank
    dest_x, dest_y = dest_x.astype(I32), dest_y.astype(I32)
    ztile = jnp.where(counts % sub != 0, row_start + counts // sub * sub, -1).astype(I32)
    g = jnp.arange(n_super, dtype=I32)
    owner = jnp.sum((g[:, None] >= sb_end[None, :]).astype(I32), axis=1)
    sb_expert = jnp.minimum(owner, n_exp - 1).astype(I32)
    local = (g - sb_start[sb_expert]) * tm
    active = g < sb_end[-1]
    sb_rows = jnp.where(active, jnp.clip(padded[sb_expert] - local, 0, tm), 0).astype(I32)
    sb_first = jnp.where(active, row_start[sb_expert] + local, 0).astype(I32)
    return (dest_x, dest_y, ztile, row_end[-1:].astype(I32), sb_expert, sb_rows, sb_first,
            sb_end[-1:].astype(I32))


def _layer(alpha, x, mem, w_in, w_spatial, b_spatial, a_ln_g, a_ln_b, w_mem_kv, norm_a_g, norm_b_g,
           norm_m_g, w_out, ln1_g, ln1_b, w_router, b_router, w_gate_up, b_gate_up, w_down, b_down,
           ln2_g, ln2_b):
    batch, seq, d = x.shape
    t = batch * seq
    m_width = w_mem_kv.shape[1] // 2
    a_width = a_ln_g.shape[0]
    b_width = norm_b_g.shape[0]
    n_exp = w_router.shape[1]
    row = lambda v: v.reshape(1, -1)

    x2 = x.reshape(t, d)
    za, zq, zk, zv, zm = _in_proj(x2, w_in.astype(BF16), (2 * a_width, b_width, b_width, b_width, m_width))
    kv = _kv_proj(mem, w_mem_kv.astype(BF16))
    ya, ym = _local_mix(za, zm, kv, w_spatial, b_spatial.T, row(a_ln_g), row(a_ln_b),
                        row(norm_a_g), row(norm_m_g), batch, seq)
    yb = _dilated_attention(zq, zk, zv, batch, seq)

    wr_hi = w_router.astype(BF16)
    wr_lo = (w_router - wr_hi.astype(F32)).astype(BF16)
    x1, gates, meta, counts = _out_router(
        alpha, x2, ya, yb, ym, w_out.astype(BF16), row(norm_b_g), row(ln1_g), row(ln1_b),
        wr_hi, wr_lo, row(b_router))

    n_assign = t * TOP_K
    n_super = (n_assign + n_exp * (EXPERT_ROWS - 1)) // EXPERT_ROWS
    n_x_rows = (n_assign + n_exp * (EXPERT_SUB - 1)) // EXPERT_SUB * EXPERT_SUB
    dest_x, dest_y, ztile, total_rows, sb_expert, sb_rows, sb_first, n_active = _routing_tables(
        meta, counts.reshape(n_exp), n_super)
    xs = _dispatch(x1, dest_x, ztile, total_rows, n_x_rows)
    ys = _experts(xs, sb_expert, sb_rows, sb_first, n_active, n_super, w_gate_up, b_gate_up, w_down, b_down)
    out = _combine(alpha, dest_y, x1, gates, row(ln2_g), row(ln2_b), ys)
    return out.reshape(batch, seq, d)


def kernel(x, mem, w_in, w_spatial, b_spatial, a_ln_g, a_ln_b, w_mem_kv, norm_a_g, norm_b_g, norm_m_g, w_out, ln1_g, ln1_b, w_router, b_router, w_gate_up, b_gate_up, w_down, b_down, ln2_g, ln2_b):
    depth = w_in.shape[0]
    alpha = (2.0 * depth) ** 0.25
    for l in range(depth):
        x = _layer(alpha, x, mem, w_in[l], w_spatial[l], b_spatial[l], a_ln_g[l], a_ln_b[l], w_mem_kv[l],
                   norm_a_g[l], norm_b_g[l], norm_m_g[l], w_out[l], ln1_g[l], ln1_b[l], w_router[l],
                   b_router[l], w_gate_up[l], b_gate_up[l], w_down[l], b_down[l], ln2_g[l], ln2_b[l])
    return x
```

```python
import functools

import jax
import jax.numpy as jnp
from jax import lax
from jax.experimental import pallas as pl
from jax.experimental.pallas import tpu as pltpu

F32 = jnp.float32
BF16 = jnp.bfloat16
I32 = jnp.int32
U32 = jnp.uint32

HEAD_DIM = 128
CHUNK = 128
ATT_BLOCK = 128
DILATIONS = (1, 4, 16)
TOP_K = 4
SWIGLU_LIMIT = 7.0
SWIGLU_ALPHA = 1.702
LN_EPS = 1e-5
NEG_INF = -1e30

V7X_VMEM_LIMIT_BYTES = 56 * 1024 * 1024

IN_PROJ_ROWS = 256
LOCAL_ROWS = 512
OUT_ROWS = 512
OUT_SUB_ROWS = 256
DISPATCH_ROWS = 256
COMBINE_ROWS = 256
DILATED_FILL_ROWS = 256
DILATED_GROUP = 16
EXPERT_ROWS = 2304
EXPERT_SUB = 256
EXPERT_FF_TILE = 256
EXPERT_OUT_TILE = 256


def _params(semantics):
    return pltpu.CompilerParams(dimension_semantics=semantics,
                                vmem_limit_bytes=V7X_VMEM_LIMIT_BYTES)


def _full_vmem():
    return pl.BlockSpec(memory_space=pltpu.VMEM)


def _layer_norm(h, g, b):
    mu = jnp.mean(h, axis=-1, keepdims=True)
    hc = h - mu
    var = jnp.mean(hc * hc, axis=-1, keepdims=True)
    return hc * lax.rsqrt(var + LN_EPS) * g + b


def _rms_norm(y, g):
    return y * lax.rsqrt(jnp.mean(y * y, axis=-1, keepdims=True) + LN_EPS) * g


def _gelu_tanh(x):
    c = 0.7978845608028654
    return x * (0.5 * (1.0 + jnp.tanh(c * (x + 0.044715 * (x * x * x)))))


def _dot_nt(a, b):
    return lax.dot_general(a, b, (((1,), (1,)), ((), ())), preferred_element_type=F32)


def _in_proj_kernel(bounds, x_ref, w_ref, *out_refs):
    xb = x_ref[...].astype(BF16)
    for o_ref, (lo, hi) in zip(out_refs, bounds):
        o_ref[...] = jnp.dot(xb, w_ref[:, lo:hi], preferred_element_type=F32)


def _in_proj(x2, w_in_bf, widths):
    t, d = x2.shape
    tm = IN_PROJ_ROWS
    bounds, lo = [], 0
    for w in widths:
        bounds.append((lo, lo + w))
        lo += w
    return pl.pallas_call(
        functools.partial(_in_proj_kernel, tuple(bounds)),
        grid=(t // tm,),
        in_specs=[pl.BlockSpec((tm, d), lambda i: (i, 0)), _full_vmem()],
        out_specs=[pl.BlockSpec((tm, w), lambda i: (i, 0)) for w in widths],
        out_shape=[jax.ShapeDtypeStruct((t, w), F32) for w in widths],
        compiler_params=_params(("arbitrary",)),
        name="in_proj",
    )(x2, w_in_bf)


def _kv_proj_kernel(mem_ref, w_ref, kv_ref):
    kv_ref[...] = jnp.dot(mem_ref[...].astype(BF16), w_ref[...],
                          preferred_element_type=F32).astype(BF16)


def _kv_proj(mem, w_kv_bf):
    b, n_mem, d = mem.shape
    n = w_kv_bf.shape[1]
    return pl.pallas_call(
        _kv_proj_kernel,
        grid=(b,),
        in_specs=[pl.BlockSpec((None, n_mem, d), lambda i: (i, 0, 0)), _full_vmem()],
        out_specs=pl.BlockSpec((None, n_mem, n), lambda i: (i, 0, 0)),
        out_shape=jax.ShapeDtypeStruct((b, n_mem, n), BF16),
        compiler_params=_params(("arbitrary",)),
        name="kv_proj",
    )(mem, w_kv_bf)


def _local_mix_kernel(za_ref, zm_ref, kv_ref, wsp_ref, bsp_ref, lng_ref, lnb_ref, nag_ref, nmg_ref,
                      ya_ref, ym_ref, ya_acc, ym_acc):
    rows = za_ref.shape[0]
    a_width = za_ref.shape[1] // 2
    n_groups = a_width // HEAD_DIM
    n_chunks = rows // CHUNK
    m_width = zm_ref.shape[1]
    m_heads = m_width // HEAD_DIM

    act = _gelu_tanh(za_ref[...])
    u = act[:, :a_width]
    v = _layer_norm(act[:, a_width:], lng_ref[...], lnb_ref[...])
    t_idx = lax.broadcasted_iota(I32, (CHUNK, CHUNK), 0)
    s_idx = lax.broadcasted_iota(I32, (CHUNK, CHUNK), 1)
    causal = s_idx <= t_idx
    bsp = bsp_ref[...]
    for g in range(n_groups):
        cols = slice(g * HEAD_DIM, (g + 1) * HEAD_DIM)
        w_g = jnp.where(causal, wsp_ref[g], 0.0).astype(BF16)
        v_g = jnp.concatenate([v[c * CHUNK:(c + 1) * CHUNK, cols] for c in range(n_chunks)], axis=1)
        mixed = jnp.dot(w_g, v_g.astype(BF16), preferred_element_type=F32) + bsp[:, g:g + 1]
        for c in range(n_chunks):
            rs = slice(c * CHUNK, (c + 1) * CHUNK)
            ya_acc[rs, cols] = u[rs, cols] * mixed[:, c * HEAD_DIM:(c + 1) * HEAD_DIM]
    ya_ref[...] = _rms_norm(ya_acc[...], nag_ref[...]).astype(BF16)

    scale = HEAD_DIM ** -0.5
    q = zm_ref[...]
    for h in range(m_heads):
        cols = slice(h * HEAD_DIM, (h + 1) * HEAD_DIM)
        k_h = kv_ref[:, h * HEAD_DIM:(h + 1) * HEAD_DIM]
        v_h = kv_ref[:, m_width + h * HEAD_DIM:m_width + (h + 1) * HEAD_DIM]
        s = _dot_nt(q[:, cols].astype(BF16), k_h) * scale
        p = jnp.exp(s - jnp.max(s, axis=-1, keepdims=True))
        l = jnp.sum(p, axis=-1, keepdims=True)
        ym_acc[:, cols] = jnp.dot(p.astype(BF16), v_h, preferred_element_type=F32) / l
    ym_ref[...] = _rms_norm(ym_acc[...], nmg_ref[...]).astype(BF16)


def _local_mix(za, zm, kv, w_spatial, b_spatial_t, a_ln_g, a_ln_b, norm_a_g, norm_m_g, batch, seq):
    rows = LOCAL_ROWS
    a2 = za.shape[1]
    a_width = a2 // 2
    m_width = zm.shape[1]
    n_mem, kv_w = kv.shape[1], kv.shape[2]
    tiles = seq // rows
    return pl.pallas_call(
        _local_mix_kernel,
        grid=(batch, tiles),
        in_specs=[
            pl.BlockSpec((rows, a2), lambda b, i: (b * tiles + i, 0)),
            pl.BlockSpec((rows, m_width), lambda b, i: (b * tiles + i, 0)),
            pl.BlockSpec((None, n_mem, kv_w), lambda b, i: (b, 0, 0)),
            _full_vmem(), _full_vmem(), _full_vmem(), _full_vmem(), _full_vmem(), _full_vmem(),
        ],
        out_specs=[pl.BlockSpec((rows, a_width), lambda b, i: (b * tiles + i, 0)),
                   pl.BlockSpec((rows, m_width), lambda b, i: (b * tiles + i, 0))],
        out_shape=[jax.ShapeDtypeStruct((batch * seq, a_width), BF16),
                   jax.ShapeDtypeStruct((batch * seq, m_width), BF16)],
        scratch_shapes=[pltpu.VMEM((rows, a_width), F32), pltpu.VMEM((rows, m_width), F32)],
        compiler_params=_params(("arbitrary", "arbitrary")),
        name="local_mix",
    )(za, zm, kv, w_spatial, b_spatial_t, a_ln_g, a_ln_b, norm_a_g, norm_m_g)


def _dilated_kernel(q_ref, k_ref, v_ref, o_ref, qs, ks, vs, *state):
    seq = q_ref.shape[0]
    scale = HEAD_DIM ** -0.5
    blk = ATT_BLOCK
    n_pat = len(DILATIONS)
    m_s, l_s, a_s = state[:n_pat], state[n_pat:2 * n_pat], state[2 * n_pat:]
    qi = lax.broadcasted_iota(I32, (blk, 2 * blk), 0)
    kj = lax.broadcasted_iota(I32, (blk, 2 * blk), 1)
    band = (kj >= qi) & (kj <= qi + blk)
    zero_blk = jnp.zeros((blk, HEAD_DIM), BF16)
    vs[:, HEAD_DIM:] = jnp.ones((vs.shape[0], HEAD_DIM), BF16)

    for p, d in enumerate(DILATIONS):
        sub = seq // d
        n_blocks = sub // blk
        chunk = min(sub, DILATED_FILL_ROWS)

        def fill(i, c, d=d, sub=sub, chunk=chunk):
            r = i // (sub // chunk)
            j = i % (sub // chunk)
            first = r + j * (chunk * d)
            src = pl.ds(first, chunk) if d == 1 else pl.ds(first, chunk, stride=d)
            dst_q = pl.multiple_of(r * sub + j * chunk, blk)
            dst_kv = pl.multiple_of(r * (sub + blk) + blk + j * chunk, blk)
            qs[pl.ds(dst_q, chunk), :] = q_ref[src, :].astype(BF16)
            ks[pl.ds(dst_kv, chunk), :] = k_ref[src, :].astype(BF16)
            vs[pl.ds(dst_kv, chunk), :HEAD_DIM] = v_ref[src, :].astype(BF16)
            return c

        def pad(r, c, sub=sub):
            at = pl.multiple_of(r * (sub + blk), blk)
            ks[pl.ds(at, blk), :] = zero_blk
            vs[pl.ds(at, blk), :HEAD_DIM] = zero_blk
            return c

        lax.fori_loop(0, d, pad, 0)
        lax.fori_loop(0, d * (sub // chunk), fill, 0)

        def block(c, d=d, n_blocks=n_blocks, p=p):
            r = c // n_blocks
            n = c % n_blocks
            q = qs[pl.ds(pl.multiple_of(c * blk, blk), blk), :]
            win = pl.ds(pl.multiple_of((c + r) * blk, blk), 2 * blk)
            s = _dot_nt(q, ks[win, :]) * scale
            valid = band & (kj >= jnp.where(n > 0, 0, blk))
            s = jnp.where(valid, s, NEG_INF)
            m = jnp.max(s, axis=-1, keepdims=True)
            e = jnp.exp(s - m)
            acc = jnp.dot(e.astype(BF16), vs[win, :], preferred_element_type=F32)
            first = n * (blk * d) + r
            out = pl.ds(first, blk) if d == 1 else pl.ds(first, blk, stride=d)
            m_s[p][out, :] = jnp.broadcast_to(m, (blk, HEAD_DIM))
            l_s[p][out, :] = acc[:, HEAD_DIM:]
            a_s[p][out, :] = acc[:, :HEAD_DIM]

        def group(i, c, block=block):
            for g in range(DILATED_GROUP):
                block(i * DILATED_GROUP + g)
            return c

        lax.fori_loop(0, seq // blk // DILATED_GROUP, group, 0)

    def merge(j, c):
        rows = pl.ds(pl.multiple_of(j * blk, blk), blk)
        ms = [m[rows, :] for m in m_s]
        m_all = functools.reduce(jnp.maximum, ms)
        ws = [jnp.exp(m - m_all) for m in ms]
        den = sum(w * l[rows, :] for w, l in zip(ws, l_s))
        num = sum(w * a[rows, :] for w, a in zip(ws, a_s))
        o_ref[rows, :] = num / den
        return c

    lax.fori_loop(0, seq // blk, merge, 0)


def _dilated_attention(zq, zk, zv, batch, seq):
    width = zq.shape[1]
    heads = width // HEAD_DIM
    spec = pl.BlockSpec((seq, HEAD_DIM), lambda b, h: (b, h))
    kv_rows = seq + max(DILATIONS) * ATT_BLOCK
    return pl.pallas_call(
        _dilated_kernel,
        grid=(batch, heads),
        in_specs=[spec, spec, spec],
        out_specs=spec,
        out_shape=jax.ShapeDtypeStruct((batch * seq, width), F32),
        scratch_shapes=[pltpu.VMEM((seq, HEAD_DIM), BF16), pltpu.VMEM((kv_rows, HEAD_DIM), BF16),
                        pltpu.VMEM((kv_rows, 2 * HEAD_DIM), BF16)]
                       + [pltpu.VMEM((seq, HEAD_DIM), F32)] * (3 * len(DILATIONS)),
        compiler_params=_params(("arbitrary", "arbitrary")),
        name="dilated",
    )(zq, zk, zv)


def _out_router_kernel(alpha, x_ref, ya_ref, yb_ref, ym_ref, wo_ref, nbg_ref, g_ref, b_ref,
                       wrh_ref, wrl_ref, br_ref,
                       x1_ref, gate_ref, meta_ref, cnt_ref, base):
    step = pl.program_id(0)
    tm = OUT_SUB_ROWS
    a_w = ya_ref.shape[1]
    b_w = yb_ref.shape[1]
    n_exp = br_ref.shape[1]

    @pl.when(step == 0)
    def _():
        base[...] = jnp.zeros_like(base)

    lane = lax.broadcasted_iota(I32, (tm, n_exp), 1).astype(F32)
    t_i = lax.broadcasted_iota(I32, (tm, tm), 0)
    t_j = lax.broadcasted_iota(I32, (tm, tm), 1)
    earlier = (t_j < t_i).astype(BF16)
    k_lane = lax.broadcasted_iota(I32, (tm, TOP_K), 1)
    s_lane = lax.broadcasted_iota(I32, (tm, HEAD_DIM), 1)
    running = base[...]

    for h in range(x_ref.shape[0] // tm):
        rows = slice(h * tm, (h + 1) * tm)
        yb = _rms_norm(yb_ref[rows, :], nbg_ref[...]).astype(BF16)
        y = jnp.concatenate([ya_ref[rows, :], yb, ym_ref[rows, :]], axis=1)
        mix = jnp.dot(y, wo_ref[...], preferred_element_type=F32)
        x1 = _layer_norm(alpha * x_ref[rows, :] + mix, g_ref[...], b_ref[...])
        x1_ref[rows, :] = x1

        x_hi = x1.astype(BF16)
        x_lo = (x1 - x_hi.astype(F32)).astype(BF16)
        logits = (jnp.dot(x_hi, wrh_ref[...], preferred_element_type=F32)
                  + jnp.dot(x_lo, wrh_ref[...], preferred_element_type=F32)
                  + jnp.dot(x_hi, wrl_ref[...], preferred_element_type=F32)) + br_ref[...]

        work = logits
        vals, sels = [], []
        for _ in range(TOP_K):
            m = jnp.max(work, axis=-1, keepdims=True)
            sel = jnp.min(jnp.where(work == m, lane, float(n_exp)), axis=-1, keepdims=True)
            vals.append(m)
            sels.append(sel)
            work = jnp.where(lane == sel, -jnp.inf, work)
        exps = [jnp.exp(v - vals[0]) for v in vals]
        denom = exps[0] + exps[1] + exps[2] + exps[3]

        hot = [lane == s for s in sels]
        multi = sum(h_.astype(F32) for h_ in hot)
        before = jnp.dot(earlier, multi.astype(BF16), preferred_element_type=F32) + running
        gate_out = jnp.zeros((tm, TOP_K), F32)
        slots = jnp.zeros((tm, HEAD_DIM), F32)
        for k in range(TOP_K):
            rank_k = jnp.sum(jnp.where(hot[k], before, 0.0), axis=-1, keepdims=True)
            gate_out = jnp.where(k_lane == k, exps[k] / denom, gate_out)
            slots = jnp.where(s_lane == k, sels[k], slots)
            slots = jnp.where(s_lane == TOP_K + k, rank_k, slots)
        gate_ref[rows, :] = gate_out
        meta_ref[:, rows] = jnp.transpose(slots)[:2 * TOP_K, :].astype(I32)
        running = running + jnp.sum(multi, axis=0, keepdims=True)

    base[...] = running
    cnt_ref[...] = running.astype(I32)


def _out_router(alpha, x2, ya, yb, ym, w_out_bf, norm_b_g, ln_g, ln_b, wr_hi, wr_lo, b_router):
    t, d = x2.shape
    tm = OUT_ROWS
    n_exp = b_router.shape[1]
    row = lambda w: pl.BlockSpec((tm, w), lambda i: (i, 0))
    return pl.pallas_call(
        functools.partial(_out_router_kernel, alpha),
        grid=(t // tm,),
        in_specs=[row(d), row(ya.shape[1]), row(yb.shape[1]), row(ym.shape[1])] + [_full_vmem()] * 7,
        out_specs=[row(d), row(TOP_K), pl.BlockSpec((2 * TOP_K, tm), lambda i: (0, i)),
                   pl.BlockSpec((1, n_exp), lambda i: (0, 0))],
        out_shape=[jax.ShapeDtypeStruct((t, d), F32),
                   jax.ShapeDtypeStruct((t, TOP_K), F32),
                   jax.ShapeDtypeStruct((2 * TOP_K, t), I32),
                   jax.ShapeDtypeStruct((1, n_exp), I32)],
        scratch_shapes=[pltpu.VMEM((1, n_exp), F32)],
        compiler_params=_params(("arbitrary",)),
        name="out_router",
    )(x2, ya, yb, ym, w_out_bf, norm_b_g, ln_g, ln_b, wr_hi, wr_lo, b_router)


def _dispatch_kernel(ztile_ref, total_ref, dest_ref, x_ref, xs_hbm, zeros, rows, sem, zsem):
    tm, d = x_ref.shape
    sub = zeros.shape[0]
    n_tiles = xs_hbm.shape[0] // sub

    @pl.when(pl.program_id(0) == 0)
    def _():
        zeros[...] = jnp.zeros_like(zeros)

        def fill(first_row):
            return pltpu.make_async_copy(zeros, xs_hbm.at[pl.ds(pl.multiple_of(first_row, sub), sub), :], zsem)

        def partial_tiles(run):
            def step(e, c):
                @pl.when(ztile_ref[e] >= 0)
                def _():
                    run(fill(ztile_ref[e]))
                return c
            lax.fori_loop(0, ztile_ref.shape[0], step, 0)

        def tail_tiles(run):
            def step(j, c):
                run(fill(j * sub))
                return c
            lax.fori_loop(total_ref[0] // sub, n_tiles, step, 0)

        partial_tiles(lambda copy: copy.start())
        tail_tiles(lambda copy: copy.start())
        partial_tiles(lambda copy: copy.wait())
        tail_tiles(lambda copy: copy.wait())

    i = pl.program_id(0)
    buf = i % 2
    rows[buf] = x_ref[...]
    for t in range(tm):
        for k in range(TOP_K):
            pltpu.make_async_copy(rows.at[buf, pl.ds(t, 1), :], xs_hbm.at[pl.ds(dest_ref[k, t], 1), :],
                                  sem.at[buf]).start(priority=k % 2)

    def drain(b):
        for _ in range(TOP_K):
            pltpu.make_async_copy(rows.at[b], xs_hbm.at[pl.ds(0, tm), :], sem.at[b]).wait()

    @pl.when(i > 0)
    def _():
        drain(1 - buf)

    @pl.when(i == pl.num_programs(0) - 1)
    def _():
        drain(buf)


def _dispatch(x1, dest_flat, ztile, total_rows, n_rows):
    t, d = x1.shape
    tm = DISPATCH_ROWS
    grid_spec = pltpu.PrefetchScalarGridSpec(
        num_scalar_prefetch=2,
        grid=(t // tm,),
        in_specs=[pl.BlockSpec((TOP_K, tm), lambda i, z, n: (0, i), memory_space=pltpu.SMEM),
                  pl.BlockSpec((tm, d), lambda i, z, n: (i, 0))],
        out_specs=pl.BlockSpec(memory_space=pl.ANY),
        scratch_shapes=[pltpu.VMEM((EXPERT_SUB, d), F32), pltpu.VMEM((2, tm, d), F32),
                        pltpu.SemaphoreType.DMA((2,)), pltpu.SemaphoreType.DMA],
    )
    return pl.pallas_call(
        _dispatch_kernel,
        grid_spec=grid_spec,
        out_shape=jax.ShapeDtypeStruct((n_rows, d), F32),
        compiler_params=_params(("arbitrary",)),
        name="dispatch",
    )(ztile, total_rows, dest_flat, x1)


def _experts_kernel(nf, sbe_ref, sbr_ref, sbf_ref, na_ref, xs_hbm, wg_ref, wu_ref, bg_ref, bu_ref,
                    wd_ref, bd_ref, o_ref, xb, act, stage, sem):
    g = pl.program_id(0)
    s = pl.program_id(1)
    nn = pl.num_programs(1) - nf
    rows_max = xb.shape[0]
    sub = stage.shape[1]
    n_chunks = rows_max // sub
    n_sub = sbr_ref[g] // sub
    nxt = jnp.minimum(g + 1, pl.num_programs(0) - 1)
    n_next = jnp.where(g + 1 < pl.num_programs(0), sbr_ref[nxt], 0) // sub

    def chunk_copy(sb, c, slot):
        first = pl.multiple_of(sbf_ref[sb] + c * sub, sub)
        return pltpu.make_async_copy(xs_hbm.at[pl.ds(first, sub), :], stage.at[slot], sem.at[slot])

    def land(c, slot):
        chunk_copy(0, 0, slot).wait()
        xb[pl.ds(pl.multiple_of(c * sub, sub), sub), :] = stage[slot].astype(BF16)

    def row_tiles(body):
        n_big = n_sub // 8

        def big(j, c):
            body(pl.multiple_of(j * (8 * sub), 8 * sub), 8 * sub)
            return c

        lax.fori_loop(0, n_big, big, 0)
        done = n_big * 8
        for size in (4, 2, 1):
            take = (n_sub - done) >= size

            @pl.when(take)
            def _(done=done, size=size):
                body(pl.multiple_of(done * sub, size * sub), size * sub)

            done = done + jnp.where(take, size, 0)

    @pl.when(g >= na_ref[0])
    def _():
        o_ref[...] = jnp.zeros_like(o_ref)

    @pl.when(g < na_ref[0])
    def _():
        @pl.when((g == 0) & (s == 0))
        def _():
            def fetch(c, carry):
                chunk_copy(0, c, 0).start()
                land(c, 0)
                return carry
            lax.fori_loop(0, n_sub, fetch, 0)

        for slot in range(2):
            @pl.when((g > 0) & (s == 0) & (nn - 1 + slot * nn < n_sub))
            def _(slot=slot):
                land(nn - 1 + slot * nn, slot)

        @pl.when(s < nf)
        def _():
            def body(first, rows):
                x = xb[pl.ds(first, rows), :]
                gate = jnp.dot(x, wg_ref[...].astype(BF16), preferred_element_type=F32) + bg_ref[...]
                up = jnp.dot(x, wu_ref[...].astype(BF16), preferred_element_type=F32) + bu_ref[...]
                gate = jnp.minimum(gate, SWIGLU_LIMIT)
                up = jnp.clip(up, -SWIGLU_LIMIT, SWIGLU_LIMIT)
                a = (up + 1.0) * (gate * jax.nn.sigmoid(SWIGLU_ALPHA * gate))
                act[s, pl.ds(first, rows), :] = a.astype(BF16)

            row_tiles(body)

        @pl.when(s >= nf)
        def _():
            k = s - nf
            for slot in range(2):
                @pl.when((k >= 1) & (k - 1 + slot * nn < n_next))
                def _(slot=slot):
                    land(k - 1 + slot * nn, slot)

                @pl.when(k + slot * nn < n_next)
                def _(slot=slot):
                    chunk_copy(nxt, k + slot * nn, slot).start()

            def body(first, rows):
                a = jnp.concatenate([act[j, pl.ds(first, rows), :] for j in range(nf)], axis=1)
                o_ref[pl.ds(first, rows), :] = jnp.dot(a, wd_ref[...].astype(BF16),
                                                       preferred_element_type=F32) + bd_ref[...]

            row_tiles(body)

            def clear(c, carry):
                o_ref[pl.ds(pl.multiple_of(c * sub, sub), sub), :] = jnp.zeros((sub, o_ref.shape[1]), F32)
                return carry
            lax.fori_loop(n_sub, n_chunks, clear, 0)


def _experts(xs, sb_expert, sb_rows, sb_first, n_active, n_super, w_gate_up, b_gate_up, w_down, b_down):
    d = xs.shape[1]
    n_exp, _, two_ff = w_gate_up.shape
    d_ff = two_ff // 2
    tm, tf, tn, sub = EXPERT_ROWS, EXPERT_FF_TILE, EXPERT_OUT_TILE, EXPERT_SUB
    nf, nn = d_ff // tf, d // tn
    assert 2 * nn >= tm // sub, "two row chunks per down-projection step must cover a super-block"
    bgu = b_gate_up.reshape(n_exp, 1, two_ff)
    bdn = b_down.reshape(n_exp, 1, d)

    def expert(g, sbe, na):
        return sbe[jnp.minimum(g, na[0] - 1)]

    def ff_step(g, s, na):
        return jnp.where(g < na[0], jnp.minimum(s, nf - 1), nf - 1)

    def down_step(g, s, na):
        return jnp.where(g < na[0], jnp.maximum(s - nf, 0), nn - 1)

    in_specs = [
        pl.BlockSpec(memory_space=pl.ANY),
        pl.BlockSpec((None, d, tf), lambda g, s, sbe, sbr, sbf, na: (expert(g, sbe, na), 0, ff_step(g, s, na))),
        pl.BlockSpec((None, d, tf), lambda g, s, sbe, sbr, sbf, na: (expert(g, sbe, na), 0, nf + ff_step(g, s, na))),
        pl.BlockSpec((None, 1, tf), lambda g, s, sbe, sbr, sbf, na: (expert(g, sbe, na), 0, ff_step(g, s, na))),
        pl.BlockSpec((None, 1, tf), lambda g, s, sbe, sbr, sbf, na: (expert(g, sbe, na), 0, nf + ff_step(g, s, na))),
        pl.BlockSpec((None, d_ff, tn), lambda g, s, sbe, sbr, sbf, na: (expert(g, sbe, na), 0, down_step(g, s, na))),
        pl.BlockSpec((None, 1, tn), lambda g, s, sbe, sbr, sbf, na: (expert(g, sbe, na), 0, down_step(g, s, na))),
    ]
    grid_spec = pltpu.PrefetchScalarGridSpec(
        num_scalar_prefetch=4,
        grid=(n_super, nf + nn),
        in_specs=in_specs,
        out_specs=pl.BlockSpec((tm, tn), lambda g, s, sbe, sbr, sbf, na: (g, jnp.maximum(s - nf, 0))),
        scratch_shapes=[pltpu.VMEM((tm, d), BF16), pltpu.VMEM((nf, tm, tf), BF16),
                        pltpu.VMEM((2, sub, d), F32), pltpu.SemaphoreType.DMA((2,))],
    )
    return pl.pallas_call(
        functools.partial(_experts_kernel, nf),
        grid_spec=grid_spec,
        out_shape=jax.ShapeDtypeStruct((n_super * tm, d), F32),
        compiler_params=_params(("arbitrary", "arbitrary")),
        name="experts",
    )(sb_expert, sb_rows, sb_first, n_active, xs, w_gate_up, w_gate_up, bgu, bgu, w_down, bdn)


def _combine_kernel(alpha, dest_ref, next_ref, x1_ref, gate_ref, g_ref, b_ref, ys_hbm, o_ref, rows, sem):
    i = pl.program_id(0)
    tm, d = x1_ref.shape

    def gather(slots_ref, buf):
        for t in range(tm):
            for k in range(TOP_K):
                pltpu.make_async_copy(ys_hbm.at[pl.ds(slots_ref[k, t], 1), :],
                                      rows.at[buf, k, pl.ds(t, 1), :],
                                      sem.at[buf]).start(priority=k % 2)

    @pl.when(i == 0)
    def _():
        gather(dest_ref, 0)

    @pl.when(i + 1 < pl.num_programs(0))
    def _():
        gather(next_ref, (i + 1) % 2)

    buf = i % 2
    for k in range(TOP_K):
        pltpu.make_async_copy(ys_hbm.at[pl.ds(0, tm), :], rows.at[buf, k], sem.at[buf]).wait()

    gates = gate_ref[...]
    ffn = gates[:, 0:1] * rows[buf, 0]
    for k in range(1, TOP_K):
        ffn += gates[:, k:k + 1] * rows[buf, k]
    o_ref[...] = _layer_norm(alpha * x1_ref[...] + ffn, g_ref[...], b_ref[...])


def _combine(alpha, dest_flat, x1, gates, ln_g, ln_b, ys):
    t, d = x1.shape
    tm = COMBINE_ROWS
    return pl.pallas_call(
        functools.partial(_combine_kernel, alpha),
        grid=(t // tm,),
        in_specs=[pl.BlockSpec((TOP_K, tm), lambda i: (0, i), memory_space=pltpu.SMEM),
                  pl.BlockSpec((TOP_K, tm), lambda i: (0, jnp.minimum(i + 1, t // tm - 1)),
                               memory_space=pltpu.SMEM),
                  pl.BlockSpec((tm, d), lambda i: (i, 0)),
                  pl.BlockSpec((tm, TOP_K), lambda i: (i, 0)),
                  _full_vmem(), _full_vmem(),
                  pl.BlockSpec(memory_space=pl.ANY)],
        out_specs=pl.BlockSpec((tm, d), lambda i: (i, 0)),
        out_shape=jax.ShapeDtypeStruct((t, d), F32),
        scratch_shapes=[pltpu.VMEM((2, TOP_K, tm, d), F32), pltpu.SemaphoreType.DMA((2,))],
        compiler_params=_params(("arbitrary",)),
        name="combine",
    )(dest_flat, dest_flat, x1, gates, ln_g, ln_b, ys)


def _routing_tables(meta, counts, n_super):
    tm, sub = EXPERT_ROWS, EXPERT_SUB
    n_exp = counts.shape[0]
    tiles = (counts + sub - 1) // sub
    padded = tiles * sub
    row_end = jnp.cumsum(padded)
    row_start = row_end - padded
    per_expert = (tiles + tm // sub - 1) // (tm // sub)
    size = jnp.maximum((tiles + per_expert - 1) // jnp.maximum(per_expert, 1), 1) * sub
    sb_end = jnp.cumsum(per_expert)
    sb_start = sb_end - per_expert
    idx, rank = meta[:TOP_K], meta[TOP_K:]
    hit = idx[None] == jnp.arange(n_exp, dtype=I32)[:, None, None]

    def lookup(table):
        return jnp.sum(jnp.where(hit, table[:, None, None], 0), axis=0)

    size_t = lookup(size)
    dest_x = (lookup(row_start) + rank).astype(I32)
    dest_y = ((lookup(sb_start) + rank // size_t) * tm + rank % size_t).astype(I32)
    ztile = jnp.where(counts % sub != 0, row_start + counts // sub * sub, -1).astype(I32)
    g = jnp.arange(n_super, dtype=I32)
    owner = jnp.sum((g[:, None] >= sb_end[None, :]).astype(I32), axis=1)
    sb_expert = jnp.minimum(owner, n_exp - 1).astype(I32)
    local = (g - sb_start[sb_expert]) * size[sb_expert]
    active = g < sb_end[-1]
    sb_rows = jnp.where(active, jnp.clip(padded[sb_expert] - local, 0, size[sb_expert]), 0).astype(I32)
    sb_first = jnp.where(active, row_start[sb_expert] + local, 0).astype(I32)
    return (dest_x, dest_y, ztile, row_end[-1:].astype(I32), sb_expert, sb_rows, sb_first,
            sb_end[-1:].astype(I32))


def _layer(alpha, x, mem, w_in, w_spatial, b_spatial, a_ln_g, a_ln_b, w_mem_kv, norm_a_g, norm_b_g,
           norm_m_g, w_out, ln1_g, ln1_b, w_router, b_router, w_gate_up, b_gate_up, w_down, b_down,
           ln2_g, ln2_b):
    batch, seq, d = x.shape
    t = batch * seq
    m_width = w_mem_kv.shape[1] // 2
    a_width = a_ln_g.shape[0]
    b_width = norm_b_g.shape[0]
    n_exp = w_router.shape[1]
    row = lambda v: v.reshape(1, -1)

    x2 = x.reshape(t, d)
    za, zq, zk, zv, zm = _in_proj(x2, w_in.astype(BF16), (2 * a_width, b_width, b_width, b_width, m_width))
    kv = _kv_proj(mem, w_mem_kv.astype(BF16))
    ya, ym = _local_mix(za, zm, kv, w_spatial, b_spatial.T, row(a_ln_g), row(a_ln_b),
                        row(norm_a_g), row(norm_m_g), batch, seq)
    yb = _dilated_attention(zq, zk, zv, batch, seq)

    wr_hi = w_router.astype(BF16)
    wr_lo = (w_router - wr_hi.astype(F32)).astype(BF16)
    x1, gates, meta, counts = _out_router(
        alpha, x2, ya, yb, ym, w_out.astype(BF16), row(norm_b_g), row(ln1_g), row(ln1_b),
        wr_hi, wr_lo, row(b_router))

    n_assign = t * TOP_K
    n_super = (n_assign + n_exp * (EXPERT_ROWS - 1)) // EXPERT_ROWS
    n_x_rows = (n_assign + n_exp * (EXPERT_SUB - 1)) // EXPERT_SUB * EXPERT_SUB
    dest_x, dest_y, ztile, total_rows, sb_expert, sb_rows, sb_first, n_active = _routing_tables(
        meta, counts.reshape(n_exp), n_super)
    xs = _dispatch(x1, dest_x, ztile, total_rows, n_x_rows)
    ys = _experts(xs, sb_expert, sb_rows, sb_first, n_active, n_super, w_gate_up, b_gate_up, w_down, b_down)
    out = _combine(alpha, dest_y, x1, gates, row(ln2_g), row(ln2_b), ys)
    return out.reshape(batch, seq, d)


def kernel(x, mem, w_in, w_spatial, b_spatial, a_ln_g, a_ln_b, w_mem_kv, norm_a_g, norm_b_g, norm_m_g, w_out, ln1_g, ln1_b, w_router, b_router, w_gate_up, b_gate_up, w_down, b_down, ln2_g, ln2_b):
    depth = w_in.shape[0]
    alpha = (2.0 * depth) ** 0.25
    for l in range(depth):
        x = _layer(alpha, x, mem, w_in[l], w_spatial[l], b_spatial[l], a_ln_g[l], a_ln_b[l], w_mem_kv[l],
                   norm_a_g[l], norm_b_g[l], norm_m_g[l], w_out[l], ln1_g[l], ln1_b[l], w_router[l],
                   b_router[l], w_gate_up[l], b_gate_up[l], w_down[l], b_down[l], ln2_g[l], ln2_b[l])
    return x
```

```python
import functools

import jax
import jax.numpy as jnp
from jax import lax
from jax.experimental import pallas as pl
from jax.experimental.pallas import tpu as pltpu

F32 = jnp.float32
BF16 = jnp.bfloat16
I32 = jnp.int32
U32 = jnp.uint32

HEAD_DIM = 128
CHUNK = 128
ATT_BLOCK = 128
DILATIONS = (1, 4, 16)
TOP_K = 4
SWIGLU_LIMIT = 7.0
SWIGLU_ALPHA = 1.702
LN_EPS = 1e-5
NEG_INF = -1e30

V7X_VMEM_LIMIT_BYTES = 56 * 1024 * 1024

IN_PROJ_ROWS = 256
LOCAL_ROWS = 512
OUT_ROWS = 512
OUT_SUB_ROWS = 256
DISPATCH_ROWS = 256
COMBINE_ROWS = 256
DILATED_FILL_ROWS = 256
DILATED_GROUP = 16
EXPERT_ROWS = 2304
EXPERT_SUB = 256
EXPERT_FF_TILE = 256
EXPERT_OUT_TILE = 256


def _params(semantics):
    return pltpu.CompilerParams(dimension_semantics=semantics,
                                vmem_limit_bytes=V7X_VMEM_LIMIT_BYTES)


def _full_vmem():
    return pl.BlockSpec(memory_space=pltpu.VMEM)


def _layer_norm(h, g, b):
    mu = jnp.mean(h, axis=-1, keepdims=True)
    hc = h - mu
    var = jnp.mean(hc * hc, axis=-1, keepdims=True)
    return hc * lax.rsqrt(var + LN_EPS) * g + b


def _rms_norm(y, g):
    return y * lax.rsqrt(jnp.mean(y * y, axis=-1, keepdims=True) + LN_EPS) * g


def _gelu_tanh(x):
    c = 0.7978845608028654
    return x * (0.5 * (1.0 + jnp.tanh(c * (x + 0.044715 * (x * x * x)))))


def _dot_nt(a, b):
    return lax.dot_general(a, b, (((1,), (1,)), ((), ())), preferred_element_type=F32)


def _in_proj_kernel(bounds, x_ref, w_ref, *out_refs):
    xb = x_ref[...].astype(BF16)
    for o_ref, (lo, hi) in zip(out_refs, bounds):
        o_ref[...] = jnp.dot(xb, w_ref[:, lo:hi], preferred_element_type=F32)


def _in_proj(x2, w_in_bf, widths):
    t, d = x2.shape
    tm = IN_PROJ_ROWS
    bounds, lo = [], 0
    for w in widths:
        bounds.append((lo, lo + w))
        lo += w
    return pl.pallas_call(
        functools.partial(_in_proj_kernel, tuple(bounds)),
        grid=(t // tm,),
        in_specs=[pl.BlockSpec((tm, d), lambda i: (i, 0)), _full_vmem()],
        out_specs=[pl.BlockSpec((tm, w), lambda i: (i, 0)) for w in widths],
        out_shape=[jax.ShapeDtypeStruct((t, w), F32) for w in widths],
        compiler_params=_params(("arbitrary",)),
        name="in_proj",
    )(x2, w_in_bf)


def _kv_proj_kernel(mem_ref, w_ref, kv_ref):
    kv_ref[...] = jnp.dot(mem_ref[...].astype(BF16), w_ref[...],
                          preferred_element_type=F32).astype(BF16)


def _kv_proj(mem, w_kv_bf):
    b, n_mem, d = mem.shape
    n = w_kv_bf.shape[1]
    return pl.pallas_call(
        _kv_proj_kernel,
        grid=(b,),
        in_specs=[pl.BlockSpec((None, n_mem, d), lambda i: (i, 0, 0)), _full_vmem()],
        out_specs=pl.BlockSpec((None, n_mem, n), lambda i: (i, 0, 0)),
        out_shape=jax.ShapeDtypeStruct((b, n_mem, n), BF16),
        compiler_params=_params(("arbitrary",)),
        name="kv_proj",
    )(mem, w_kv_bf)


def _local_mix_kernel(za_ref, zm_ref, kv_ref, wsp_ref, bsp_ref, lng_ref, lnb_ref, nag_ref, nmg_ref,
                      ya_ref, ym_ref, ya_acc, ym_acc):
    rows = za_ref.shape[0]
    a_width = za_ref.shape[1] // 2
    n_groups = a_width // HEAD_DIM
    n_chunks = rows // CHUNK
    m_width = zm_ref.shape[1]
    m_heads = m_width // HEAD_DIM

    act = _gelu_tanh(za_ref[...])
    u = act[:, :a_width]
    v = _layer_norm(act[:, a_width:], lng_ref[...], lnb_ref[...])
    t_idx = lax.broadcasted_iota(I32, (CHUNK, CHUNK), 0)
    s_idx = lax.broadcasted_iota(I32, (CHUNK, CHUNK), 1)
    causal = s_idx <= t_idx
    bsp = bsp_ref[...]
    for g in range(n_groups):
        cols = slice(g * HEAD_DIM, (g + 1) * HEAD_DIM)
        w_g = jnp.where(causal, wsp_ref[g], 0.0).astype(BF16)
        v_g = jnp.concatenate([v[c * CHUNK:(c + 1) * CHUNK, cols] for c in range(n_chunks)], axis=1)
        mixed = jnp.dot(w_g, v_g.astype(BF16), preferred_element_type=F32) + bsp[:, g:g + 1]
        for c in range(n_chunks):
            rs = slice(c * CHUNK, (c + 1) * CHUNK)
            ya_acc[rs, cols] = u[rs, cols] * mixed[:, c * HEAD_DIM:(c + 1) * HEAD_DIM]
    ya_ref[...] = _rms_norm(ya_acc[...], nag_ref[...]).astype(BF16)

    scale = HEAD_DIM ** -0.5
    q = zm_ref[...]
    for h in range(m_heads):
        cols = slice(h * HEAD_DIM, (h + 1) * HEAD_DIM)
        k_h = kv_ref[:, h * HEAD_DIM:(h + 1) * HEAD_DIM]
        v_h = kv_ref[:, m_width + h * HEAD_DIM:m_width + (h + 1) * HEAD_DIM]
        s = _dot_nt(q[:, cols].astype(BF16), k_h) * scale
        p = jnp.exp(s - jnp.max(s, axis=-1, keepdims=True))
        l = jnp.sum(p, axis=-1, keepdims=True)
        ym_acc[:, cols] = jnp.dot(p.astype(BF16), v_h, preferred_element_type=F32) / l
    ym_ref[...] = _rms_norm(ym_acc[...], nmg_ref[...]).astype(BF16)


def _local_mix(za, zm, kv, w_spatial, b_spatial_t, a_ln_g, a_ln_b, norm_a_g, norm_m_g, batch, seq):
    rows = LOCAL_ROWS
    a2 = za.shape[1]
    a_width = a2 // 2
    m_width = zm.shape[1]
    n_mem, kv_w = kv.shape[1], kv.shape[2]
    tiles = seq // rows
    return pl.pallas_call(
        _local_mix_kernel,
        grid=(batch, tiles),
        in_specs=[
            pl.BlockSpec((rows, a2), lambda b, i: (b * tiles + i, 0)),
            pl.BlockSpec((rows, m_width), lambda b, i: (b * tiles + i, 0)),
            pl.BlockSpec((None, n_mem, kv_w), lambda b, i: (b, 0, 0)),
            _full_vmem(), _full_vmem(), _full_vmem(), _full_vmem(), _full_vmem(), _full_vmem(),
        ],
        out_specs=[pl.BlockSpec((rows, a_width), lambda b, i: (b * tiles + i, 0)),
                   pl.BlockSpec((rows, m_width), lambda b, i: (b * tiles + i, 0))],
        out_shape=[jax.ShapeDtypeStruct((batch * seq, a_width), BF16),
                   jax.ShapeDtypeStruct((batch * seq, m_width), BF16)],
        scratch_shapes=[pltpu.VMEM((rows, a_width), F32), pltpu.VMEM((rows, m_width), F32)],
        compiler_params=_params(("arbitrary", "arbitrary")),
        name="local_mix",
    )(za, zm, kv, w_spatial, b_spatial_t, a_ln_g, a_ln_b, norm_a_g, norm_m_g)


def _dilated_kernel(q_ref, k_ref, v_ref, o_ref, qs, ks, vs, q4, k4, v4, *state):
    seq = q_ref.shape[0]
    scale = HEAD_DIM ** -0.5
    blk = ATT_BLOCK
    step = DILATIONS[1]
    quarter = seq // step
    n_pat = len(DILATIONS)
    m_s, l_s, a_s = state[:n_pat], state[n_pat:2 * n_pat], state[2 * n_pat:]
    qi = lax.broadcasted_iota(I32, (blk, 2 * blk), 0)
    kj = lax.broadcasted_iota(I32, (blk, 2 * blk), 1)
    band = (kj >= qi) & (kj <= qi + blk)
    zero_blk = jnp.zeros((blk, HEAD_DIM), BF16)
    vs[:, HEAD_DIM:] = jnp.ones((vs.shape[0], HEAD_DIM), BF16)

    for p, d in enumerate(DILATIONS):
        sub = seq // d
        n_blocks = sub // blk
        chunk = min(sub, DILATED_FILL_ROWS)

        def fill(i, c, d=d, sub=sub, chunk=chunk):
            r = i // (sub // chunk)
            j = i % (sub // chunk)
            if d == 1:
                src = pl.ds(pl.multiple_of(j * chunk, chunk), chunk)
                q, k, v = q_ref[src, :], k_ref[src, :], v_ref[src, :]
            elif d == step:
                src = pl.ds(r + j * (chunk * step), chunk, stride=step)
                q, k, v = q_ref[src, :], k_ref[src, :], v_ref[src, :]
                keep = pl.ds(pl.multiple_of(r * sub + j * chunk, chunk), chunk)
                q4[keep, :], k4[keep, :], v4[keep, :] = q, k, v
            else:
                src = pl.ds((r % step) * quarter + r // step + j * (chunk * step), chunk, stride=step)
                q, k, v = q4[src, :], k4[src, :], v4[src, :]
            dst_q = pl.multiple_of(r * sub + j * chunk, blk)
            dst_kv = pl.multiple_of(r * (sub + blk) + blk + j * chunk, blk)
            qs[pl.ds(dst_q, chunk), :] = q.astype(BF16)
            ks[pl.ds(dst_kv, chunk), :] = k.astype(BF16)
            vs[pl.ds(dst_kv, chunk), :HEAD_DIM] = v.astype(BF16)
            return c

        def pad(r, c, sub=sub):
            at = pl.multiple_of(r * (sub + blk), blk)
            ks[pl.ds(at, blk), :] = zero_blk
            vs[pl.ds(at, blk), :HEAD_DIM] = zero_blk
            return c

        lax.fori_loop(0, d, pad, 0)
        lax.fori_loop(0, d * (sub // chunk), fill, 0)

        def block(c, d=d, n_blocks=n_blocks, p=p):
            r = c // n_blocks
            n = c % n_blocks
            q = qs[pl.ds(pl.multiple_of(c * blk, blk), blk), :]
            win = pl.ds(pl.multiple_of((c + r) * blk, blk), 2 * blk)
            s = _dot_nt(q, ks[win, :]) * scale
            valid = band & (kj >= jnp.where(n > 0, 0, blk))
            s = jnp.where(valid, s, NEG_INF)
            m = jnp.max(s, axis=-1, keepdims=True)
            e = jnp.exp(s - m)
            acc = jnp.dot(e.astype(BF16), vs[win, :], preferred_element_type=F32)
            if d <= step:
                out = pl.ds(pl.multiple_of(c * blk, blk), blk)
            else:
                out = pl.ds((r % step) * quarter + n * (blk * step) + r // step, blk, stride=step)
            m_s[p][out, :] = jnp.broadcast_to(m, (blk, HEAD_DIM))
            l_s[p][out, :] = acc[:, HEAD_DIM:]
            a_s[p][out, :] = acc[:, :HEAD_DIM]

        def group(i, c, block=block):
            for g in range(DILATED_GROUP):
                block(i * DILATED_GROUP + g)
            return c

        lax.fori_loop(0, seq // blk // DILATED_GROUP, group, 0)

    def merge(j, c):
        rows = pl.ds(pl.multiple_of(j * blk, blk), blk)
        res = j // (quarter // blk)
        orig = pl.ds((j % (quarter // blk)) * (blk * step) + res, blk, stride=step)
        sel = [orig] + [rows] * (n_pat - 1)
        ms = [m[r_, :] for m, r_ in zip(m_s, sel)]
        m_all = functools.reduce(jnp.maximum, ms)
        ws = [jnp.exp(m - m_all) for m in ms]
        den = sum(w * l[r_, :] for w, l, r_ in zip(ws, l_s, sel))
        num = sum(w * a[r_, :] for w, a, r_ in zip(ws, a_s, sel))
        o_ref[orig, :] = num / den
        return c

    lax.fori_loop(0, seq // blk, merge, 0)


def _dilated_attention(zq, zk, zv, batch, seq):
    width = zq.shape[1]
    heads = width // HEAD_DIM
    assert DILATIONS == (1, DILATIONS[1], DILATIONS[1] ** 2)
    spec = pl.BlockSpec((seq, HEAD_DIM), lambda b, h: (b, h))
    kv_rows = seq + max(DILATIONS) * ATT_BLOCK
    return pl.pallas_call(
        _dilated_kernel,
        grid=(batch, heads),
        in_specs=[spec, spec, spec],
        out_specs=spec,
        out_shape=jax.ShapeDtypeStruct((batch * seq, width), F32),
        scratch_shapes=[pltpu.VMEM((seq, HEAD_DIM), BF16), pltpu.VMEM((kv_rows, HEAD_DIM), BF16),
                        pltpu.VMEM((kv_rows, 2 * HEAD_DIM), BF16)]
                       + [pltpu.VMEM((seq, HEAD_DIM), F32)] * (3 + 3 * len(DILATIONS)),
        compiler_params=_params(("arbitrary", "arbitrary")),
        name="dilated",
    )(zq, zk, zv)


def _out_router_kernel(alpha, x_ref, ya_ref, yb_ref, ym_ref, wo_ref, nbg_ref, g_ref, b_ref,
                       wrh_ref, wrl_ref, br_ref,
                       x1_ref, gate_ref, meta_ref, cnt_ref, base):
    step = pl.program_id(0)
    tm = OUT_SUB_ROWS
    a_w = ya_ref.shape[1]
    b_w = yb_ref.shape[1]
    n_exp = br_ref.shape[1]

    @pl.when(step == 0)
    def _():
        base[...] = jnp.zeros_like(base)

    lane = lax.broadcasted_iota(I32, (tm, n_exp), 1).astype(F32)
    t_i = lax.broadcasted_iota(I32, (tm, tm), 0)
    t_j = lax.broadcasted_iota(I32, (tm, tm), 1)
    earlier = (t_j < t_i).astype(BF16)
    k_lane = lax.broadcasted_iota(I32, (tm, TOP_K), 1)
    s_lane = lax.broadcasted_iota(I32, (tm, HEAD_DIM), 1)
    running = base[...]

    for h in range(x_ref.shape[0] // tm):
        rows = slice(h * tm, (h + 1) * tm)
        yb = _rms_norm(yb_ref[rows, :], nbg_ref[...]).astype(BF16)
        y = jnp.concatenate([ya_ref[rows, :], yb, ym_ref[rows, :]], axis=1)
        mix = jnp.dot(y, wo_ref[...], preferred_element_type=F32)
        x1 = _layer_norm(alpha * x_ref[rows, :] + mix, g_ref[...], b_ref[...])
        x1_ref[rows, :] = x1

        x_hi = x1.astype(BF16)
        x_lo = (x1 - x_hi.astype(F32)).astype(BF16)
        logits = (jnp.dot(x_hi, wrh_ref[...], preferred_element_type=F32)
                  + jnp.dot(x_lo, wrh_ref[...], preferred_element_type=F32)
                  + jnp.dot(x_hi, wrl_ref[...], preferred_element_type=F32)) + br_ref[...]

        work = logits
        vals, sels = [], []
        for _ in range(TOP_K):
            m = jnp.max(work, axis=-1, keepdims=True)
            sel = jnp.min(jnp.where(work == m, lane, float(n_exp)), axis=-1, keepdims=True)
            vals.append(m)
            sels.append(sel)
            work = jnp.where(lane == sel, -jnp.inf, work)
        exps = [jnp.exp(v - vals[0]) for v in vals]
        denom = exps[0] + exps[1] + exps[2] + exps[3]

        hot = [lane == s for s in sels]
        multi = sum(h_.astype(F32) for h_ in hot)
        before = jnp.dot(earlier, multi.astype(BF16), preferred_element_type=F32) + running
        gate_out = jnp.zeros((tm, TOP_K), F32)
        slots = jnp.zeros((tm, HEAD_DIM), F32)
        for k in range(TOP_K):
            rank_k = jnp.sum(jnp.where(hot[k], before, 0.0), axis=-1, keepdims=True)
            gate_out = jnp.where(k_lane == k, exps[k] / denom, gate_out)
            slots = jnp.where(s_lane == k, sels[k], slots)
            slots = jnp.where(s_lane == TOP_K + k, rank_k, slots)
        gate_ref[rows, :] = gate_out
        meta_ref[:, rows] = jnp.transpose(slots)[:2 * TOP_K, :].astype(I32)
        running = running + jnp.sum(multi, axis=0, keepdims=True)

    base[...] = running
    cnt_ref[...] = running.astype(I32)


def _out_router(alpha, x2, ya, yb, ym, w_out_bf, norm_b_g, ln_g, ln_b, wr_hi, wr_lo, b_router):
    t, d = x2.shape
    tm = OUT_ROWS
    n_exp = b_router.shape[1]
    row = lambda w: pl.BlockSpec((tm, w), lambda i: (i, 0))
    return pl.pallas_call(
        functools.partial(_out_router_kernel, alpha),
        grid=(t // tm,),
        in_specs=[row(d), row(ya.shape[1]), row(yb.shape[1]), row(ym.shape[1])] + [_full_vmem()] * 7,
        out_specs=[row(d), row(TOP_K), pl.BlockSpec((2 * TOP_K, tm), lambda i: (0, i)),
                   pl.BlockSpec((1, n_exp), lambda i: (0, 0))],
        out_shape=[jax.ShapeDtypeStruct((t, d), F32),
                   jax.ShapeDtypeStruct((t, TOP_K), F32),
                   jax.ShapeDtypeStruct((2 * TOP_K, t), I32),
                   jax.ShapeDtypeStruct((1, n_exp), I32)],
        scratch_shapes=[pltpu.VMEM((1, n_exp), F32)],
        compiler_params=_params(("arbitrary",)),
        name="out_router",
    )(x2, ya, yb, ym, w_out_bf, norm_b_g, ln_g, ln_b, wr_hi, wr_lo, b_router)


def _dispatch_kernel(ztile_ref, total_ref, dest_ref, x_ref, xs_hbm, zeros, rows, sem, zsem):
    tm, d = x_ref.shape
    sub = zeros.shape[0]
    n_tiles = xs_hbm.shape[0] // sub

    @pl.when(pl.program_id(0) == 0)
    def _():
        zeros[...] = jnp.zeros_like(zeros)

        def fill(first_row):
            return pltpu.make_async_copy(zeros, xs_hbm.at[pl.ds(pl.multiple_of(first_row, sub), sub), :], zsem)

        def partial_tiles(run):
            def step(e, c):
                @pl.when(ztile_ref[e] >= 0)
                def _():
                    run(fill(ztile_ref[e]))
                return c
            lax.fori_loop(0, ztile_ref.shape[0], step, 0)

        def tail_tiles(run):
            def step(j, c):
                run(fill(j * sub))
                return c
            lax.fori_loop(total_ref[0] // sub, n_tiles, step, 0)

        partial_tiles(lambda copy: copy.start())
        tail_tiles(lambda copy: copy.start())
        partial_tiles(lambda copy: copy.wait())
        tail_tiles(lambda copy: copy.wait())

    i = pl.program_id(0)
    buf = i % 2
    rows[buf] = x_ref[...]
    for t in range(tm):
        for k in range(TOP_K):
            pltpu.make_async_copy(rows.at[buf, pl.ds(t, 1), :], xs_hbm.at[pl.ds(dest_ref[k, t], 1), :],
                                  sem.at[buf]).start(priority=k % 2)

    def drain(b):
        for _ in range(TOP_K):
            pltpu.make_async_copy(rows.at[b], xs_hbm.at[pl.ds(0, tm), :], sem.at[b]).wait()

    @pl.when(i > 0)
    def _():
        drain(1 - buf)

    @pl.when(i == pl.num_programs(0) - 1)
    def _():
        drain(buf)


def _dispatch(x1, dest_flat, ztile, total_rows, n_rows):
    t, d = x1.shape
    tm = DISPATCH_ROWS
    grid_spec = pltpu.PrefetchScalarGridSpec(
        num_scalar_prefetch=2,
        grid=(t // tm,),
        in_specs=[pl.BlockSpec((TOP_K, tm), lambda i, z, n: (0, i), memory_space=pltpu.SMEM),
                  pl.BlockSpec((tm, d), lambda i, z, n: (i, 0))],
        out_specs=pl.BlockSpec(memory_space=pl.ANY),
        scratch_shapes=[pltpu.VMEM((EXPERT_SUB, d), F32), pltpu.VMEM((2, tm, d), F32),
                        pltpu.SemaphoreType.DMA((2,)), pltpu.SemaphoreType.DMA],
    )
    return pl.pallas_call(
        _dispatch_kernel,
        grid_spec=grid_spec,
        out_shape=jax.ShapeDtypeStruct((n_rows, d), F32),
        compiler_params=_params(("arbitrary",)),
        name="dispatch",
    )(ztile, total_rows, dest_flat, x1)


def _experts_kernel(nf, sbe_ref, sbr_ref, sbf_ref, na_ref, xs_hbm, wg_ref, wu_ref, bg_ref, bu_ref,
                    wd_ref, bd_ref, o_ref, xb, act, stage, sem):
    g = pl.program_id(0)
    s = pl.program_id(1)
    nn = pl.num_programs(1) - nf
    rows_max = xb.shape[0]
    sub = stage.shape[1]
    n_chunks = rows_max // sub
    n_sub = sbr_ref[g] // sub
    nxt = jnp.minimum(g + 1, pl.num_programs(0) - 1)
    n_next = jnp.where(g + 1 < pl.num_programs(0), sbr_ref[nxt], 0) // sub

    def chunk_copy(sb, c, slot):
        first = pl.multiple_of(sbf_ref[sb] + c * sub, sub)
        return pltpu.make_async_copy(xs_hbm.at[pl.ds(first, sub), :], stage.at[slot], sem.at[slot])

    def land(c, slot):
        chunk_copy(0, 0, slot).wait()
        xb[pl.ds(pl.multiple_of(c * sub, sub), sub), :] = stage[slot].astype(BF16)

    def row_tiles(body):
        n_big = n_sub // 8

        def big(j, c):
            body(pl.multiple_of(j * (8 * sub), 8 * sub), 8 * sub)
            return c

        lax.fori_loop(0, n_big, big, 0)
        done = n_big * 8
        for size in (4, 2, 1):
            take = (n_sub - done) >= size

            @pl.when(take)
            def _(done=done, size=size):
                body(pl.multiple_of(done * sub, size * sub), size * sub)

            done = done + jnp.where(take, size, 0)

    @pl.when(g >= na_ref[0])
    def _():
        o_ref[...] = jnp.zeros_like(o_ref)

    @pl.when(g < na_ref[0])
    def _():
        @pl.when((g == 0) & (s == 0))
        def _():
            def fetch(c, carry):
                chunk_copy(0, c, 0).start()
                land(c, 0)
                return carry
            lax.fori_loop(0, n_sub, fetch, 0)

        for slot in range(2):
            @pl.when((g > 0) & (s == 0) & (nn - 1 + slot * nn < n_sub))
            def _(slot=slot):
                land(nn - 1 + slot * nn, slot)

        @pl.when(s < nf)
        def _():
            def body(first, rows):
                x = xb[pl.ds(first, rows), :]
                gate = jnp.dot(x, wg_ref[...].astype(BF16), preferred_element_type=F32) + bg_ref[...]
                up = jnp.dot(x, wu_ref[...].astype(BF16), preferred_element_type=F32) + bu_ref[...]
                gate = jnp.minimum(gate, SWIGLU_LIMIT)
                up = jnp.clip(up, -SWIGLU_LIMIT, SWIGLU_LIMIT)
                a = (up + 1.0) * (gate * jax.nn.sigmoid(SWIGLU_ALPHA * gate))
                act[s, pl.ds(first, rows), :] = a.astype(BF16)

            row_tiles(body)

        @pl.when(s >= nf)
        def _():
            k = s - nf
            for slot in range(2):
                @pl.when((k >= 1) & (k - 1 + slot * nn < n_next))
                def _(slot=slot):
                    land(k - 1 + slot * nn, slot)

                @pl.when(k + slot * nn < n_next)
                def _(slot=slot):
                    chunk_copy(nxt, k + slot * nn, slot).start()

            def body(first, rows):
                a = jnp.concatenate([act[j, pl.ds(first, rows), :] for j in range(nf)], axis=1)
                o_ref[pl.ds(first, rows), :] = jnp.dot(a, wd_ref[...].astype(BF16),
                                                       preferred_element_type=F32) + bd_ref[...]

            row_tiles(body)

            def clear(c, carry):
                o_ref[pl.ds(pl.multiple_of(c * sub, sub), sub), :] = jnp.zeros((sub, o_ref.shape[1]), F32)
                return carry
            lax.fori_loop(n_sub, n_chunks, clear, 0)


def _experts(xs, sb_expert, sb_rows, sb_first, n_active, n_super, w_gate_up, b_gate_up, w_down, b_down):
    d = xs.shape[1]
    n_exp, _, two_ff = w_gate_up.shape
    d_ff = two_ff // 2
    tm, tf, tn, sub = EXPERT_ROWS, EXPERT_FF_TILE, EXPERT_OUT_TILE, EXPERT_SUB
    nf, nn = d_ff // tf, d // tn
    assert 2 * nn >= tm // sub, "two row chunks per down-projection step must cover a super-block"
    bgu = b_gate_up.reshape(n_exp, 1, two_ff)
    bdn = b_down.reshape(n_exp, 1, d)

    def expert(g, sbe, na):
        return sbe[jnp.minimum(g, na[0] - 1)]

    def ff_step(g, s, na):
        return jnp.where(g < na[0], jnp.minimum(s, nf - 1), nf - 1)

    def down_step(g, s, na):
        return jnp.where(g < na[0], jnp.maximum(s - nf, 0), nn - 1)

    in_specs = [
        pl.BlockSpec(memory_space=pl.ANY),
        pl.BlockSpec((None, d, tf), lambda g, s, sbe, sbr, sbf, na: (expert(g, sbe, na), 0, ff_step(g, s, na))),
        pl.BlockSpec((None, d, tf), lambda g, s, sbe, sbr, sbf, na: (expert(g, sbe, na), 0, nf + ff_step(g, s, na))),
        pl.BlockSpec((None, 1, tf), lambda g, s, sbe, sbr, sbf, na: (expert(g, sbe, na), 0, ff_step(g, s, na))),
        pl.BlockSpec((None, 1, tf), lambda g, s, sbe, sbr, sbf, na: (expert(g, sbe, na), 0, nf + ff_step(g, s, na))),
        pl.BlockSpec((None, d_ff, tn), lambda g, s, sbe, sbr, sbf, na: (expert(g, sbe, na), 0, down_step(g, s, na))),
        pl.BlockSpec((None, 1, tn), lambda g, s, sbe, sbr, sbf, na: (expert(g, sbe, na), 0, down_step(g, s, na))),
    ]
    grid_spec = pltpu.PrefetchScalarGridSpec(
        num_scalar_prefetch=4,
        grid=(n_super, nf + nn),
        in_specs=in_specs,
        out_specs=pl.BlockSpec((tm, tn), lambda g, s, sbe, sbr, sbf, na: (g, jnp.maximum(s - nf, 0))),
        scratch_shapes=[pltpu.VMEM((tm, d), BF16), pltpu.VMEM((nf, tm, tf), BF16),
                        pltpu.VMEM((2, sub, d), F32), pltpu.SemaphoreType.DMA((2,))],
    )
    return pl.pallas_call(
        functools.partial(_experts_kernel, nf),
        grid_spec=grid_spec,
        out_shape=jax.ShapeDtypeStruct((n_super * tm, d), F32),
        compiler_params=_params(("arbitrary", "arbitrary")),
        name="experts",
    )(sb_expert, sb_rows, sb_first, n_active, xs, w_gate_up, w_gate_up, bgu, bgu, w_down, bdn)


def _combine_kernel(alpha, dest_ref, next_ref, x1_ref, gate_ref, g_ref, b_ref, ys_hbm, o_ref, rows, sem):
    i = pl.program_id(0)
    tm, d = x1_ref.shape

    def gather(slots_ref, buf):
        for t in range(tm):
            for k in range(TOP_K):
                pltpu.make_async_copy(ys_hbm.at[pl.ds(slots_ref[k, t], 1), :],
                                      rows.at[buf, k, pl.ds(t, 1), :],
                                      sem.at[buf]).start(priority=k % 2)

    @pl.when(i == 0)
    def _():
        gather(dest_ref, 0)

    @pl.when(i + 1 < pl.num_programs(0))
    def _():
        gather(next_ref, (i + 1) % 2)

    buf = i % 2
    for k in range(TOP_K):
        pltpu.make_async_copy(ys_hbm.at[pl.ds(0, tm), :], rows.at[buf, k], sem.at[buf]).wait()

    gates = gate_ref[...]
    ffn = gates[:, 0:1] * rows[buf, 0]
    for k in range(1, TOP_K):
        ffn += gates[:, k:k + 1] * rows[buf, k]
    o_ref[...] = _layer_norm(alpha * x1_ref[...] + ffn, g_ref[...], b_ref[...])


def _combine(alpha, dest_flat, x1, gates, ln_g, ln_b, ys):
    t, d = x1.shape
    tm = COMBINE_ROWS
    return pl.pallas_call(
        functools.partial(_combine_kernel, alpha),
        grid=(t // tm,),
        in_specs=[pl.BlockSpec((TOP_K, tm), lambda i: (0, i), memory_space=pltpu.SMEM),
                  pl.BlockSpec((TOP_K, tm), lambda i: (0, jnp.minimum(i + 1, t // tm - 1)),
                               memory_space=pltpu.SMEM),
                  pl.BlockSpec((tm, d), lambda i: (i, 0)),
                  pl.BlockSpec((tm, TOP_K), lambda i: (i, 0)),
                  _full_vmem(), _full_vmem(),
                  pl.BlockSpec(memory_space=pl.ANY)],
        out_specs=pl.BlockSpec((tm, d), lambda i: (i, 0)),
        out_shape=jax.ShapeDtypeStruct((t, d), F32),
        scratch_shapes=[pltpu.VMEM((2, TOP_K, tm, d), F32), pltpu.SemaphoreType.DMA((2,))],
        compiler_params=_params(("arbitrary",)),
        name="combine",
    )(dest_flat, dest_flat, x1, gates, ln_g, ln_b, ys)


def _routing_tables(meta, counts, n_super):
    tm, sub = EXPERT_ROWS, EXPERT_SUB
    n_exp = counts.shape[0]
    tiles = (counts + sub - 1) // sub
    padded = tiles * sub
    row_end = jnp.cumsum(padded)
    row_start = row_end - padded
    per_expert = (tiles + tm // sub - 1) // (tm // sub)
    size = jnp.maximum((tiles + per_expert - 1) // jnp.maximum(per_expert, 1), 1) * sub
    sb_end = jnp.cumsum(per_expert)
    sb_start = sb_end - per_expert
    idx, rank = meta[:TOP_K], meta[TOP_K:]
    hit = idx[None] == jnp.arange(n_exp, dtype=I32)[:, None, None]

    def lookup(table):
        return jnp.sum(jnp.where(hit, table[:, None, None], 0), axis=0)

    size_t = lookup(size)
    dest_x = (lookup(row_start) + rank).astype(I32)
    dest_y = ((lookup(sb_start) + rank // size_t) * tm + rank % size_t).astype(I32)
    ztile = jnp.where(counts % sub != 0, row_start + counts // sub * sub, -1).astype(I32)
    g = jnp.arange(n_super, dtype=I32)
    owner = jnp.sum((g[:, None] >= sb_end[None, :]).astype(I32), axis=1)
    sb_expert = jnp.minimum(owner, n_exp - 1).astype(I32)
    local = (g - sb_start[sb_expert]) * size[sb_expert]
    active = g < sb_end[-1]
    sb_rows = jnp.where(active, jnp.clip(padded[sb_expert] - local, 0, size[sb_expert]), 0).astype(I32)
    sb_first = jnp.where(active, row_start[sb_expert] + local, 0).astype(I32)
    return (dest_x, dest_y, ztile, row_end[-1:].astype(I32), sb_expert, sb_rows, sb_first,
            sb_end[-1:].astype(I32))


def _layer(alpha, x, mem, w_in, w_spatial, b_spatial, a_ln_g, a_ln_b, w_mem_kv, norm_a_g, norm_b_g,
           norm_m_g, w_out, ln1_g, ln1_b, w_router, b_router, w_gate_up, b_gate_up, w_down, b_down,
           ln2_g, ln2_b):
    batch, seq, d = x.shape
    t = batch * seq
    m_width = w_mem_kv.shape[1] // 2
    a_width = a_ln_g.shape[0]
    b_width = norm_b_g.shape[0]
    n_exp = w_router.shape[1]
    row = lambda v: v.reshape(1, -1)

    x2 = x.reshape(t, d)
    za, zq, zk, zv, zm = _in_proj(x2, w_in.astype(BF16), (2 * a_width, b_width, b_width, b_width, m_width))
    kv = _kv_proj(mem, w_mem_kv.astype(BF16))
    ya, ym = _local_mix(za, zm, kv, w_spatial, b_spatial.T, row(a_ln_g), row(a_ln_b),
                        row(norm_a_g), row(norm_m_g), batch, seq)
    yb = _dilated_attention(zq, zk, zv, batch, seq)

    wr_hi = w_router.astype(BF16)
    wr_lo = (w_router - wr_hi.astype(F32)).astype(BF16)
    x1, gates, meta, counts = _out_router(
        alpha, x2, ya, yb, ym, w_out.astype(BF16), row(norm_b_g), row(ln1_g), row(ln1_b),
        wr_hi, wr_lo, row(b_router))

    n_assign = t * TOP_K
    n_super = (n_assign + n_exp * (EXPERT_ROWS - 1)) // EXPERT_ROWS
    n_x_rows = (n_assign + n_exp * (EXPERT_SUB - 1)) // EXPERT_SUB * EXPERT_SUB
    dest_x, dest_y, ztile, total_rows, sb_expert, sb_rows, sb_first, n_active = _routing_tables(
        meta, counts.reshape(n_exp), n_super)
    xs = _dispatch(x1, dest_x, ztile, total_rows, n_x_rows)
    ys = _experts(xs, sb_expert, sb_rows, sb_first, n_active, n_super, w_gate_up, b_gate_up, w_down, b_down)
    out = _combine(alpha, dest_y, x1, gates, row(ln2_g), row(ln2_b), ys)
    return out.reshape(batch, seq, d)


def kernel(x, mem, w_in, w_spatial, b_spatial, a_ln_g, a_ln_b, w_mem_kv, norm_a_g, norm_b_g, norm_m_g, w_out, ln1_g, ln1_b, w_router, b_router, w_gate_up, b_gate_up, w_down, b_down, ln2_g, ln2_b):
    depth = w_in.shape[0]
    alpha = (2.0 * depth) ** 0.25
    for l in range(depth):
        x = _layer(alpha, x, mem, w_in[l], w_spatial[l], b_spatial[l], a_ln_g[l], a_ln_b[l], w_mem_kv[l],
                   norm_a_g[l], norm_b_g[l], norm_m_g[l], w_out[l], ln1_g[l], ln1_b[l], w_router[l],
                   b_router[l], w_gate_up[l], b_gate_up[l], w_down[l], b_down[l], ln2_g[l], ln2_b[l])
    return x
```

```python
import functools

import jax
import jax.numpy as jnp
from jax import lax
from jax.experimental import pallas as pl
from jax.experimental.pallas import tpu as pltpu

F32 = jnp.float32
BF16 = jnp.bfloat16
I32 = jnp.int32
U32 = jnp.uint32

HEAD_DIM = 128
CHUNK = 128
ATT_BLOCK = 128
DILATIONS = (1, 4, 16)
TOP_K = 4
SWIGLU_LIMIT = 7.0
SWIGLU_ALPHA = 1.702
LN_EPS = 1e-5
NEG_INF = -1e30

V7X_VMEM_LIMIT_BYTES = 56 * 1024 * 1024

IN_PROJ_ROWS = 256
LOCAL_ROWS = 512
OUT_ROWS = 512
OUT_SUB_ROWS = 256
DISPATCH_ROWS = 256
COMBINE_ROWS = 256
DILATED_FILL_ROWS = 256
DILATED_GROUP = 16
EXPERT_ROWS = 2304
EXPERT_SUB = 256
EXPERT_FF_TILE = 256
EXPERT_OUT_TILE = 512


def _params(semantics):
    return pltpu.CompilerParams(dimension_semantics=semantics,
                                vmem_limit_bytes=V7X_VMEM_LIMIT_BYTES)


def _full_vmem():
    return pl.BlockSpec(memory_space=pltpu.VMEM)


def _layer_norm(h, g, b):
    mu = jnp.mean(h, axis=-1, keepdims=True)
    hc = h - mu
    var = jnp.mean(hc * hc, axis=-1, keepdims=True)
    return hc * lax.rsqrt(var + LN_EPS) * g + b


def _rms_norm(y, g):
    return y * lax.rsqrt(jnp.mean(y * y, axis=-1, keepdims=True) + LN_EPS) * g


def _gelu_tanh(x):
    c = 0.7978845608028654
    return x * (0.5 * (1.0 + jnp.tanh(c * (x + 0.044715 * (x * x * x)))))


def _dot_nt(a, b):
    return lax.dot_general(a, b, (((1,), (1,)), ((), ())), preferred_element_type=F32)


def _in_proj_kernel(bounds, x_ref, w_ref, *out_refs):
    xb = x_ref[...].astype(BF16)
    for o_ref, (lo, hi) in zip(out_refs, bounds):
        o_ref[...] = jnp.dot(xb, w_ref[:, lo:hi], preferred_element_type=F32)


def _in_proj(x2, w_in_bf, widths):
    t, d = x2.shape
    tm = IN_PROJ_ROWS
    bounds, lo = [], 0
    for w in widths:
        bounds.append((lo, lo + w))
        lo += w
    return pl.pallas_call(
        functools.partial(_in_proj_kernel, tuple(bounds)),
        grid=(t // tm,),
        in_specs=[pl.BlockSpec((tm, d), lambda i: (i, 0)), _full_vmem()],
        out_specs=[pl.BlockSpec((tm, w), lambda i: (i, 0)) for w in widths],
        out_shape=[jax.ShapeDtypeStruct((t, w), F32) for w in widths],
        compiler_params=_params(("arbitrary",)),
        name="in_proj",
    )(x2, w_in_bf)


def _kv_proj_kernel(mem_ref, w_ref, kv_ref):
    kv_ref[...] = jnp.dot(mem_ref[...].astype(BF16), w_ref[...],
                          preferred_element_type=F32).astype(BF16)


def _kv_proj(mem, w_kv_bf):
    b, n_mem, d = mem.shape
    n = w_kv_bf.shape[1]
    return pl.pallas_call(
        _kv_proj_kernel,
        grid=(b,),
        in_specs=[pl.BlockSpec((None, n_mem, d), lambda i: (i, 0, 0)), _full_vmem()],
        out_specs=pl.BlockSpec((None, n_mem, n), lambda i: (i, 0, 0)),
        out_shape=jax.ShapeDtypeStruct((b, n_mem, n), BF16),
        compiler_params=_params(("arbitrary",)),
        name="kv_proj",
    )(mem, w_kv_bf)


def _local_mix_kernel(za_ref, zm_ref, kv_ref, wsp_ref, bsp_ref, lng_ref, lnb_ref, nag_ref, nmg_ref,
                      ya_ref, ym_ref, ya_acc, ym_acc):
    rows = za_ref.shape[0]
    a_width = za_ref.shape[1] // 2
    n_groups = a_width // HEAD_DIM
    n_chunks = rows // CHUNK
    m_width = zm_ref.shape[1]
    m_heads = m_width // HEAD_DIM

    act = _gelu_tanh(za_ref[...])
    u = act[:, :a_width]
    v = _layer_norm(act[:, a_width:], lng_ref[...], lnb_ref[...])
    t_idx = lax.broadcasted_iota(I32, (CHUNK, CHUNK), 0)
    s_idx = lax.broadcasted_iota(I32, (CHUNK, CHUNK), 1)
    causal = s_idx <= t_idx
    bsp = bsp_ref[...]
    for g in range(n_groups):
        cols = slice(g * HEAD_DIM, (g + 1) * HEAD_DIM)
        w_g = jnp.where(causal, wsp_ref[g], 0.0).astype(BF16)
        v_g = jnp.concatenate([v[c * CHUNK:(c + 1) * CHUNK, cols] for c in range(n_chunks)], axis=1)
        mixed = jnp.dot(w_g, v_g.astype(BF16), preferred_element_type=F32) + bsp[:, g:g + 1]
        for c in range(n_chunks):
            rs = slice(c * CHUNK, (c + 1) * CHUNK)
            ya_acc[rs, cols] = u[rs, cols] * mixed[:, c * HEAD_DIM:(c + 1) * HEAD_DIM]
    ya_ref[...] = _rms_norm(ya_acc[...], nag_ref[...]).astype(BF16)

    scale = HEAD_DIM ** -0.5
    q = zm_ref[...]
    for h in range(m_heads):
        cols = slice(h * HEAD_DIM, (h + 1) * HEAD_DIM)
        k_h = kv_ref[:, h * HEAD_DIM:(h + 1) * HEAD_DIM]
        v_h = kv_ref[:, m_width + h * HEAD_DIM:m_width + (h + 1) * HEAD_DIM]
        s = _dot_nt(q[:, cols].astype(BF16), k_h) * scale
        p = jnp.exp(s - jnp.max(s, axis=-1, keepdims=True))
        l = jnp.sum(p, axis=-1, keepdims=True)
        ym_acc[:, cols] = jnp.dot(p.astype(BF16), v_h, preferred_element_type=F32) / l
    ym_ref[...] = _rms_norm(ym_acc[...], nmg_ref[...]).astype(BF16)


def _local_mix(za, zm, kv, w_spatial, b_spatial_t, a_ln_g, a_ln_b, norm_a_g, norm_m_g, batch, seq):
    rows = LOCAL_ROWS
    a2 = za.shape[1]
    a_width = a2 // 2
    m_width = zm.shape[1]
    n_mem, kv_w = kv.shape[1], kv.shape[2]
    tiles = seq // rows
    return pl.pallas_call(
        _local_mix_kernel,
        grid=(batch, tiles),
        in_specs=[
            pl.BlockSpec((rows, a2), lambda b, i: (b * tiles + i, 0)),
            pl.BlockSpec((rows, m_width), lambda b, i: (b * tiles + i, 0)),
            pl.BlockSpec((None, n_mem, kv_w), lambda b, i: (b, 0, 0)),
            _full_vmem(), _full_vmem(), _full_vmem(), _full_vmem(), _full_vmem(), _full_vmem(),
        ],
        out_specs=[pl.BlockSpec((rows, a_width), lambda b, i: (b * tiles + i, 0)),
                   pl.BlockSpec((rows, m_width), lambda b, i: (b * tiles + i, 0))],
        out_shape=[jax.ShapeDtypeStruct((batch * seq, a_width), BF16),
                   jax.ShapeDtypeStruct((batch * seq, m_width), BF16)],
        scratch_shapes=[pltpu.VMEM((rows, a_width), F32), pltpu.VMEM((rows, m_width), F32)],
        compiler_params=_params(("arbitrary", "arbitrary")),
        name="local_mix",
    )(za, zm, kv, w_spatial, b_spatial_t, a_ln_g, a_ln_b, norm_a_g, norm_m_g)


def _dilated_kernel(q_ref, k_ref, v_ref, o_ref, qs, ks, vs, q4, k4, v4, *state):
    seq = q_ref.shape[0]
    scale = HEAD_DIM ** -0.5
    blk = ATT_BLOCK
    step = DILATIONS[1]
    quarter = seq // step
    n_pat = len(DILATIONS)
    m_s, l_s, a_s = state[:n_pat], state[n_pat:2 * n_pat], state[2 * n_pat:]
    qi = lax.broadcasted_iota(I32, (blk, 2 * blk), 0)
    kj = lax.broadcasted_iota(I32, (blk, 2 * blk), 1)
    band = (kj >= qi) & (kj <= qi + blk)
    zero_blk = jnp.zeros((blk, HEAD_DIM), BF16)
    vs[:, HEAD_DIM:] = jnp.ones((vs.shape[0], HEAD_DIM), BF16)

    for p, d in enumerate(DILATIONS):
        sub = seq // d
        n_blocks = sub // blk
        chunk = min(sub, DILATED_FILL_ROWS)

        def fill(i, c, d=d, sub=sub, chunk=chunk):
            r = i // (sub // chunk)
            j = i % (sub // chunk)
            if d == 1:
                src = pl.ds(pl.multiple_of(j * chunk, chunk), chunk)
                q, k, v = q_ref[src, :], k_ref[src, :], v_ref[src, :]
            elif d == step:
                src = pl.ds(r + j * (chunk * step), chunk, stride=step)
                q, k, v = q_ref[src, :], k_ref[src, :], v_ref[src, :]
                keep = pl.ds(pl.multiple_of(r * sub + j * chunk, chunk), chunk)
                q4[keep, :], k4[keep, :], v4[keep, :] = q, k, v
            else:
                src = pl.ds((r % step) * quarter + r // step + j * (chunk * step), chunk, stride=step)
                q, k, v = q4[src, :], k4[src, :], v4[src, :]
            dst_q = pl.multiple_of(r * sub + j * chunk, blk)
            dst_kv = pl.multiple_of(r * (sub + blk) + blk + j * chunk, blk)
            qs[pl.ds(dst_q, chunk), :] = q.astype(BF16)
            ks[pl.ds(dst_kv, chunk), :] = k.astype(BF16)
            vs[pl.ds(dst_kv, chunk), :HEAD_DIM] = v.astype(BF16)
            return c

        def pad(r, c, sub=sub):
            at = pl.multiple_of(r * (sub + blk), blk)
            ks[pl.ds(at, blk), :] = zero_blk
            vs[pl.ds(at, blk), :HEAD_DIM] = zero_blk
            return c

        lax.fori_loop(0, d, pad, 0)
        lax.fori_loop(0, d * (sub // chunk), fill, 0)

        def block(c, d=d, n_blocks=n_blocks, p=p):
            r = c // n_blocks
            n = c % n_blocks
            q = qs[pl.ds(pl.multiple_of(c * blk, blk), blk), :]
            win = pl.ds(pl.multiple_of((c + r) * blk, blk), 2 * blk)
            s = _dot_nt(q, ks[win, :]) * scale
            valid = band & (kj >= jnp.where(n > 0, 0, blk))
            s = jnp.where(valid, s, NEG_INF)
            m = jnp.max(s, axis=-1, keepdims=True)
            e = jnp.exp(s - m)
            acc = jnp.dot(e.astype(BF16), vs[win, :], preferred_element_type=F32)
            if d <= step:
                out = pl.ds(pl.multiple_of(c * blk, blk), blk)
            else:
                out = pl.ds((r % step) * quarter + n * (blk * step) + r // step, blk, stride=step)
            m_s[p][out, :] = jnp.broadcast_to(m, (blk, HEAD_DIM))
            l_s[p][out, :] = acc[:, HEAD_DIM:]
            a_s[p][out, :] = acc[:, :HEAD_DIM]

        def group(i, c, block=block):
            for g in range(DILATED_GROUP):
                block(i * DILATED_GROUP + g)
            return c

        lax.fori_loop(0, seq // blk // DILATED_GROUP, group, 0)

    def merge(j, c):
        rows = pl.ds(pl.multiple_of(j * blk, blk), blk)
        res = j // (quarter // blk)
        orig = pl.ds((j % (quarter // blk)) * (blk * step) + res, blk, stride=step)
        sel = [orig] + [rows] * (n_pat - 1)
        ms = [m[r_, :] for m, r_ in zip(m_s, sel)]
        m_all = functools.reduce(jnp.maximum, ms)
        ws = [jnp.exp(m - m_all) for m in ms]
        den = sum(w * l[r_, :] for w, l, r_ in zip(ws, l_s, sel))
        num = sum(w * a[r_, :] for w, a, r_ in zip(ws, a_s, sel))
        o_ref[orig, :] = num / den
        return c

    lax.fori_loop(0, seq // blk, merge, 0)


def _dilated_attention(zq, zk, zv, batch, seq):
    width = zq.shape[1]
    heads = width // HEAD_DIM
    assert DILATIONS == (1, DILATIONS[1], DILATIONS[1] ** 2)
    spec = pl.BlockSpec((seq, HEAD_DIM), lambda b, h: (b, h))
    kv_rows = seq + max(DILATIONS) * ATT_BLOCK
    return pl.pallas_call(
        _dilated_kernel,
        grid=(batch, heads),
        in_specs=[spec, spec, spec],
        out_specs=spec,
        out_shape=jax.ShapeDtypeStruct((batch * seq, width), F32),
        scratch_shapes=[pltpu.VMEM((seq, HEAD_DIM), BF16), pltpu.VMEM((kv_rows, HEAD_DIM), BF16),
                        pltpu.VMEM((kv_rows, 2 * HEAD_DIM), BF16)]
                       + [pltpu.VMEM((seq, HEAD_DIM), F32)] * (3 + 3 * len(DILATIONS)),
        compiler_params=_params(("arbitrary", "arbitrary")),
        name="dilated",
    )(zq, zk, zv)


def _out_router_kernel(alpha, x_ref, ya_ref, yb_ref, ym_ref, wo_ref, nbg_ref, g_ref, b_ref,
                       wrh_ref, wrl_ref, br_ref,
                       x1_ref, gate_ref, meta_ref, cnt_ref, base):
    step = pl.program_id(0)
    tm = OUT_SUB_ROWS
    a_w = ya_ref.shape[1]
    b_w = yb_ref.shape[1]
    n_exp = br_ref.shape[1]

    @pl.when(step == 0)
    def _():
        base[...] = jnp.zeros_like(base)

    lane = lax.broadcasted_iota(I32, (tm, n_exp), 1).astype(F32)
    t_i = lax.broadcasted_iota(I32, (tm, tm), 0)
    t_j = lax.broadcasted_iota(I32, (tm, tm), 1)
    earlier = (t_j < t_i).astype(BF16)
    k_lane = lax.broadcasted_iota(I32, (tm, TOP_K), 1)
    s_lane = lax.broadcasted_iota(I32, (tm, HEAD_DIM), 1)
    running = base[...]

    for h in range(x_ref.shape[0] // tm):
        rows = slice(h * tm, (h + 1) * tm)
        yb = _rms_norm(yb_ref[rows, :], nbg_ref[...]).astype(BF16)
        y = jnp.concatenate([ya_ref[rows, :], yb, ym_ref[rows, :]], axis=1)
        mix = jnp.dot(y, wo_ref[...], preferred_element_type=F32)
        x1 = _layer_norm(alpha * x_ref[rows, :] + mix, g_ref[...], b_ref[...])
        x1_ref[rows, :] = x1

        x_hi = x1.astype(BF16)
        x_lo = (x1 - x_hi.astype(F32)).astype(BF16)
        logits = (jnp.dot(x_hi, wrh_ref[...], preferred_element_type=F32)
                  + jnp.dot(x_lo, wrh_ref[...], preferred_element_type=F32)
                  + jnp.dot(x_hi, wrl_ref[...], preferred_element_type=F32)) + br_ref[...]

        work = logits
        vals, sels = [], []
        for _ in range(TOP_K):
            m = jnp.max(work, axis=-1, keepdims=True)
            sel = jnp.min(jnp.where(work == m, lane, float(n_exp)), axis=-1, keepdims=True)
            vals.append(m)
            sels.append(sel)
            work = jnp.where(lane == sel, -jnp.inf, work)
        exps = [jnp.exp(v - vals[0]) for v in vals]
        denom = exps[0] + exps[1] + exps[2] + exps[3]

        hot = [lane == s for s in sels]
        multi = sum(h_.astype(F32) for h_ in hot)
        before = jnp.dot(earlier, multi.astype(BF16), preferred_element_type=F32) + running
        gate_out = jnp.zeros((tm, TOP_K), F32)
        slots = jnp.zeros((tm, HEAD_DIM), F32)
        for k in range(TOP_K):
            rank_k = jnp.sum(jnp.where(hot[k], before, 0.0), axis=-1, keepdims=True)
            gate_out = jnp.where(k_lane == k, exps[k] / denom, gate_out)
            slots = jnp.where(s_lane == k, sels[k], slots)
            slots = jnp.where(s_lane == TOP_K + k, rank_k, slots)
        gate_ref[rows, :] = gate_out
        meta_ref[:, rows] = jnp.transpose(slots)[:2 * TOP_K, :].astype(I32)
        running = running + jnp.sum(multi, axis=0, keepdims=True)

    base[...] = running
    cnt_ref[...] = running.astype(I32)


def _out_router(alpha, x2, ya, yb, ym, w_out_bf, norm_b_g, ln_g, ln_b, wr_hi, wr_lo, b_router):
    t, d = x2.shape
    tm = OUT_ROWS
    n_exp = b_router.shape[1]
    row = lambda w: pl.BlockSpec((tm, w), lambda i: (i, 0))
    return pl.pallas_call(
        functools.partial(_out_router_kernel, alpha),
        grid=(t // tm,),
        in_specs=[row(d), row(ya.shape[1]), row(yb.shape[1]), row(ym.shape[1])] + [_full_vmem()] * 7,
        out_specs=[row(d), row(TOP_K), pl.BlockSpec((2 * TOP_K, tm), lambda i: (0, i)),
                   pl.BlockSpec((1, n_exp), lambda i: (0, 0))],
        out_shape=[jax.ShapeDtypeStruct((t, d), F32),
                   jax.ShapeDtypeStruct((t, TOP_K), F32),
                   jax.ShapeDtypeStruct((2 * TOP_K, t), I32),
                   jax.ShapeDtypeStruct((1, n_exp), I32)],
        scratch_shapes=[pltpu.VMEM((1, n_exp), F32)],
        compiler_params=_params(("arbitrary",)),
        name="out_router",
    )(x2, ya, yb, ym, w_out_bf, norm_b_g, ln_g, ln_b, wr_hi, wr_lo, b_router)


def _dispatch_kernel(ztile_ref, total_ref, dest_ref, x_ref, xs_hbm, zeros, rows, sem, zsem):
    tm, d = x_ref.shape
    sub = zeros.shape[0]
    n_tiles = xs_hbm.shape[0] // sub

    @pl.when(pl.program_id(0) == 0)
    def _():
        zeros[...] = jnp.zeros_like(zeros)

        def fill(first_row):
            return pltpu.make_async_copy(zeros, xs_hbm.at[pl.ds(pl.multiple_of(first_row, sub), sub), :], zsem)

        def partial_tiles(run):
            def step(e, c):
                @pl.when(ztile_ref[e] >= 0)
                def _():
                    run(fill(ztile_ref[e]))
                return c
            lax.fori_loop(0, ztile_ref.shape[0], step, 0)

        def tail_tiles(run):
            def step(j, c):
                run(fill(j * sub))
                return c
            lax.fori_loop(total_ref[0] // sub, n_tiles, step, 0)

        partial_tiles(lambda copy: copy.start())
        tail_tiles(lambda copy: copy.start())
        partial_tiles(lambda copy: copy.wait())
        tail_tiles(lambda copy: copy.wait())

    i = pl.program_id(0)
    buf = i % 2
    rows[buf] = x_ref[...]
    for t in range(tm):
        for k in range(TOP_K):
            pltpu.make_async_copy(rows.at[buf, pl.ds(t, 1), :], xs_hbm.at[pl.ds(dest_ref[k, t], 1), :],
                                  sem.at[buf]).start(priority=k % 2)

    def drain(b):
        for _ in range(TOP_K):
            pltpu.make_async_copy(rows.at[b], xs_hbm.at[pl.ds(0, tm), :], sem.at[b]).wait()

    @pl.when(i > 0)
    def _():
        drain(1 - buf)

    @pl.when(i == pl.num_programs(0) - 1)
    def _():
        drain(buf)


def _dispatch(x1, dest_flat, ztile, total_rows, n_rows):
    t, d = x1.shape
    tm = DISPATCH_ROWS
    grid_spec = pltpu.PrefetchScalarGridSpec(
        num_scalar_prefetch=2,
        grid=(t // tm,),
        in_specs=[pl.BlockSpec((TOP_K, tm), lambda i, z, n: (0, i), memory_space=pltpu.SMEM),
                  pl.BlockSpec((tm, d), lambda i, z, n: (i, 0))],
        out_specs=pl.BlockSpec(memory_space=pl.ANY),
        scratch_shapes=[pltpu.VMEM((EXPERT_SUB, d), F32), pltpu.VMEM((2, tm, d), F32),
                        pltpu.SemaphoreType.DMA((2,)), pltpu.SemaphoreType.DMA],
    )
    return pl.pallas_call(
        _dispatch_kernel,
        grid_spec=grid_spec,
        out_shape=jax.ShapeDtypeStruct((n_rows, d), F32),
        compiler_params=_params(("arbitrary",)),
        name="dispatch",
    )(ztile, total_rows, dest_flat, x1)


def _experts_kernel(nf, sbe_ref, sbr_ref, sbf_ref, na_ref, xs_hbm, wg_ref, wu_ref, bg_ref, bu_ref,
                    wd_ref, bd_ref, o_ref, xb, act, stage, sem):
    g = pl.program_id(0)
    s = pl.program_id(1)
    nn = pl.num_programs(1) - nf
    rows_max = xb.shape[0]
    n_slots, sub = stage.shape[0], stage.shape[1]
    n_chunks = rows_max // sub
    n_sub = sbr_ref[g] // sub
    nxt = jnp.minimum(g + 1, pl.num_programs(0) - 1)
    n_next = jnp.where(g + 1 < pl.num_programs(0), sbr_ref[nxt], 0) // sub

    def chunk_copy(sb, c, slot):
        first = pl.multiple_of(sbf_ref[sb] + c * sub, sub)
        return pltpu.make_async_copy(xs_hbm.at[pl.ds(first, sub), :], stage.at[slot], sem.at[slot])

    def land(c, slot):
        chunk_copy(0, 0, slot).wait()
        xb[pl.ds(pl.multiple_of(c * sub, sub), sub), :] = stage[slot].astype(BF16)

    def row_tiles(body):
        n_big = n_sub // 8

        def big(j, c):
            body(pl.multiple_of(j * (8 * sub), 8 * sub), 8 * sub)
            return c

        lax.fori_loop(0, n_big, big, 0)
        done = n_big * 8
        for size in (4, 2, 1):
            take = (n_sub - done) >= size

            @pl.when(take)
            def _(done=done, size=size):
                body(pl.multiple_of(done * sub, size * sub), size * sub)

            done = done + jnp.where(take, size, 0)

    @pl.when(g >= na_ref[0])
    def _():
        o_ref[...] = jnp.zeros_like(o_ref)

    @pl.when(g < na_ref[0])
    def _():
        @pl.when((g == 0) & (s == 0))
        def _():
            def fetch(c, carry):
                chunk_copy(0, c, 0).start()
                land(c, 0)
                return carry
            lax.fori_loop(0, n_sub, fetch, 0)

        for slot in range(n_slots):
            @pl.when((g > 0) & (s == 0) & (nn - 1 + slot * nn < n_sub))
            def _(slot=slot):
                land(nn - 1 + slot * nn, slot)

        @pl.when(s < nf)
        def _():
            def body(first, rows):
                x = xb[pl.ds(first, rows), :]
                gate = jnp.dot(x, wg_ref[...].astype(BF16), preferred_element_type=F32) + bg_ref[...]
                up = jnp.dot(x, wu_ref[...].astype(BF16), preferred_element_type=F32) + bu_ref[...]
                gate = jnp.minimum(gate, SWIGLU_LIMIT)
                up = jnp.clip(up, -SWIGLU_LIMIT, SWIGLU_LIMIT)
                a = (up + 1.0) * (gate * jax.nn.sigmoid(SWIGLU_ALPHA * gate))
                act[s, pl.ds(first, rows), :] = a.astype(BF16)

            row_tiles(body)

        @pl.when(s >= nf)
        def _():
            k = s - nf
            for slot in range(n_slots):
                @pl.when((k >= 1) & (k - 1 + slot * nn < n_next))
                def _(slot=slot):
                    land(k - 1 + slot * nn, slot)

                @pl.when(k + slot * nn < n_next)
                def _(slot=slot):
                    chunk_copy(nxt, k + slot * nn, slot).start()

            def body(first, rows):
                a = jnp.concatenate([act[j, pl.ds(first, rows), :] for j in range(nf)], axis=1)
                o_ref[pl.ds(first, rows), :] = jnp.dot(a, wd_ref[...].astype(BF16),
                                                       preferred_element_type=F32) + bd_ref[...]

            row_tiles(body)

            def clear(c, carry):
                o_ref[pl.ds(pl.multiple_of(c * sub, sub), sub), :] = jnp.zeros((sub, o_ref.shape[1]), F32)
                return carry
            lax.fori_loop(n_sub, n_chunks, clear, 0)


def _experts(xs, sb_expert, sb_rows, sb_first, n_active, n_super, w_gate_up, b_gate_up, w_down, b_down):
    d = xs.shape[1]
    n_exp, _, two_ff = w_gate_up.shape
    d_ff = two_ff // 2
    tm, tf, tn, sub = EXPERT_ROWS, EXPERT_FF_TILE, EXPERT_OUT_TILE, EXPERT_SUB
    nf, nn = d_ff // tf, d // tn
    n_slots = -(-(tm // sub) // nn)
    bgu = b_gate_up.reshape(n_exp, 1, two_ff)
    bdn = b_down.reshape(n_exp, 1, d)

    def expert(g, sbe, na):
        return sbe[jnp.minimum(g, na[0] - 1)]

    def ff_step(g, s, na):
        return jnp.where(g < na[0], jnp.minimum(s, nf - 1), nf - 1)

    def down_step(g, s, na):
        return jnp.where(g < na[0], jnp.maximum(s - nf, 0), nn - 1)

    in_specs = [
        pl.BlockSpec(memory_space=pl.ANY),
        pl.BlockSpec((None, d, tf), lambda g, s, sbe, sbr, sbf, na: (expert(g, sbe, na), 0, ff_step(g, s, na))),
        pl.BlockSpec((None, d, tf), lambda g, s, sbe, sbr, sbf, na: (expert(g, sbe, na), 0, nf + ff_step(g, s, na))),
        pl.BlockSpec((None, 1, tf), lambda g, s, sbe, sbr, sbf, na: (expert(g, sbe, na), 0, ff_step(g, s, na))),
        pl.BlockSpec((None, 1, tf), lambda g, s, sbe, sbr, sbf, na: (expert(g, sbe, na), 0, nf + ff_step(g, s, na))),
        pl.BlockSpec((None, d_ff, tn), lambda g, s, sbe, sbr, sbf, na: (expert(g, sbe, na), 0, down_step(g, s, na))),
        pl.BlockSpec((None, 1, tn), lambda g, s, sbe, sbr, sbf, na: (expert(g, sbe, na), 0, down_step(g, s, na))),
    ]
    grid_spec = pltpu.PrefetchScalarGridSpec(
        num_scalar_prefetch=4,
        grid=(n_super, nf + nn),
        in_specs=in_specs,
        out_specs=pl.BlockSpec((tm, tn), lambda g, s, sbe, sbr, sbf, na: (g, jnp.maximum(s - nf, 0))),
        scratch_shapes=[pltpu.VMEM((tm, d), BF16), pltpu.VMEM((nf, tm, tf), BF16),
                        pltpu.VMEM((n_slots, sub, d), F32), pltpu.SemaphoreType.DMA((n_slots,))],
    )
    return pl.pallas_call(
        functools.partial(_experts_kernel, nf),
        grid_spec=grid_spec,
        out_shape=jax.ShapeDtypeStruct((n_super * tm, d), F32),
        compiler_params=_params(("arbitrary", "arbitrary")),
        name="experts",
    )(sb_expert, sb_rows, sb_first, n_active, xs, w_gate_up, w_gate_up, bgu, bgu, w_down, bdn)


def _combine_kernel(alpha, dest_ref, next_ref, x1_ref, gate_ref, g_ref, b_ref, ys_hbm, o_ref, rows, sem):
    i = pl.program_id(0)
    tm, d = x1_ref.shape

    def gather(slots_ref, buf):
        for t in range(tm):
            for k in range(TOP_K):
                pltpu.make_async_copy(ys_hbm.at[pl.ds(slots_ref[k, t], 1), :],
                                      rows.at[buf, k, pl.ds(t, 1), :],
                                      sem.at[buf]).start(priority=k % 2)

    @pl.when(i == 0)
    def _():
        gather(dest_ref, 0)

    @pl.when(i + 1 < pl.num_programs(0))
    def _():
        gather(next_ref, (i + 1) % 2)

    buf = i % 2
    for k in range(TOP_K):
        pltpu.make_async_copy(ys_hbm.at[pl.ds(0, tm), :], rows.at[buf, k], sem.at[buf]).wait()

    gates = gate_ref[...]
    ffn = gates[:, 0:1] * rows[buf, 0]
    for k in range(1, TOP_K):
        ffn += gates[:, k:k + 1] * rows[buf, k]
    o_ref[...] = _layer_norm(alpha * x1_ref[...] + ffn, g_ref[...], b_ref[...])


def _combine(alpha, dest_flat, x1, gates, ln_g, ln_b, ys):
    t, d = x1.shape
    tm = COMBINE_ROWS
    return pl.pallas_call(
        functools.partial(_combine_kernel, alpha),
        grid=(t // tm,),
        in_specs=[pl.BlockSpec((TOP_K, tm), lambda i: (0, i), memory_space=pltpu.SMEM),
                  pl.BlockSpec((TOP_K, tm), lambda i: (0, jnp.minimum(i + 1, t // tm - 1)),
                               memory_space=pltpu.SMEM),
                  pl.BlockSpec((tm, d), lambda i: (i, 0)),
                  pl.BlockSpec((tm, TOP_K), lambda i: (i, 0)),
                  _full_vmem(), _full_vmem(),
                  pl.BlockSpec(memory_space=pl.ANY)],
        out_specs=pl.BlockSpec((tm, d), lambda i: (i, 0)),
        out_shape=jax.ShapeDtypeStruct((t, d), F32),
        scratch_shapes=[pltpu.VMEM((2, TOP_K, tm, d), F32), pltpu.SemaphoreType.DMA((2,))],
        compiler_params=_params(("arbitrary",)),
        name="combine",
    )(dest_flat, dest_flat, x1, gates, ln_g, ln_b, ys)


def _routing_tables(meta, counts, n_super):
    tm, sub = EXPERT_ROWS, EXPERT_SUB
    n_exp = counts.shape[0]
    tiles = (counts + sub - 1) // sub
    padded = tiles * sub
    row_end = jnp.cumsum(padded)
    row_start = row_end - padded
    per_expert = (tiles + tm // sub - 1) // (tm // sub)
    size = jnp.maximum((tiles + per_expert - 1) // jnp.maximum(per_expert, 1), 1) * sub
    sb_end = jnp.cumsum(per_expert)
    sb_start = sb_end - per_expert
    idx, rank = meta[:TOP_K], meta[TOP_K:]
    hit = idx[None] == jnp.arange(n_exp, dtype=I32)[:, None, None]

    def lookup(table):
        return jnp.sum(jnp.where(hit, table[:, None, None], 0), axis=0)

    size_t = lookup(size)
    dest_x = (lookup(row_start) + rank).astype(I32)
    dest_y = ((lookup(sb_start) + rank // size_t) * tm + rank % size_t).astype(I32)
    ztile = jnp.where(counts % sub != 0, row_start + counts // sub * sub, -1).astype(I32)
    g = jnp.arange(n_super, dtype=I32)
    owner = jnp.sum((g[:, None] >= sb_end[None, :]).astype(I32), axis=1)
    sb_expert = jnp.minimum(owner, n_exp - 1).astype(I32)
    local = (g - sb_start[sb_expert]) * size[sb_expert]
    active = g < sb_end[-1]
    sb_rows = jnp.where(active, jnp.clip(padded[sb_expert] - local, 0, size[sb_expert]), 0).astype(I32)
    sb_first = jnp.where(active, row_start[sb_expert] + local, 0).astype(I32)
    return (dest_x, dest_y, ztile, row_end[-1:].astype(I32), sb_expert, sb_rows, sb_first,
            sb_end[-1:].astype(I32))


def _layer(alpha, x, mem, w_in, w_spatial, b_spatial, a_ln_g, a_ln_b, w_mem_kv, norm_a_g, norm_b_g,
           norm_m_g, w_out, ln1_g, ln1_b, w_router, b_router, w_gate_up, b_gate_up, w_down, b_down,
           ln2_g, ln2_b):
    batch, seq, d = x.shape
    t = batch * seq
    m_width = w_mem_kv.shape[1] // 2
    a_width = a_ln_g.shape[0]
    b_width = norm_b_g.shape[0]
    n_exp = w_router.shape[1]
    row = lambda v: v.reshape(1, -1)

    x2 = x.reshape(t, d)
    za, zq, zk, zv, zm = _in_proj(x2, w_in.astype(BF16), (2 * a_width, b_width, b_width, b_width, m_width))
    kv = _kv_proj(mem, w_mem_kv.astype(BF16))
    ya, ym = _local_mix(za, zm, kv, w_spatial, b_spatial.T, row(a_ln_g), row(a_ln_b),
                        row(norm_a_g), row(norm_m_g), batch, seq)
    yb = _dilated_attention(zq, zk, zv, batch, seq)

    wr_hi = w_router.astype(BF16)
    wr_lo = (w_router - wr_hi.astype(F32)).astype(BF16)
    x1, gates, meta, counts = _out_router(
        alpha, x2, ya, yb, ym, w_out.astype(BF16), row(norm_b_g), row(ln1_g), row(ln1_b),
        wr_hi, wr_lo, row(b_router))

    n_assign = t * TOP_K
    n_super = (n_assign + n_exp * (EXPERT_ROWS - 1)) // EXPERT_ROWS
    n_x_rows = (n_assign + n_exp * (EXPERT_SUB - 1)) // EXPERT_SUB * EXPERT_SUB
    dest_x, dest_y, ztile, total_rows, sb_expert, sb_rows, sb_first, n_active = _routing_tables(
        meta, counts.reshape(n_exp), n_super)
    xs = _dispatch(x1, dest_x, ztile, total_rows, n_x_rows)
    ys = _experts(xs, sb_expert, sb_rows, sb_first, n_active, n_super, w_gate_up, b_gate_up, w_down, b_down)
    out = _combine(alpha, dest_y, x1, gates, row(ln2_g), row(ln2_b), ys)
    return out.reshape(batch, seq, d)


def kernel(x, mem, w_in, w_spatial, b_spatial, a_ln_g, a_ln_b, w_mem_kv, norm_a_g, norm_b_g, norm_m_g, w_out, ln1_g, ln1_b, w_router, b_router, w_gate_up, b_gate_up, w_down, b_down, ln2_g, ln2_b):
    depth = w_in.shape[0]
    alpha = (2.0 * depth) ** 0.25
    for l in range(depth):
        x = _layer(alpha, x, mem, w_in[l], w_spatial[l], b_spatial[l], a_ln_g[l], a_ln_b[l], w_mem_kv[l],
                   norm_a_g[l], norm_b_g[l], norm_m_g[l], w_out[l], ln1_g[l], ln1_b[l], w_router[l],
                   b_router[l], w_gate_up[l], b_gate_up[l], w_down[l], b_down[l], ln2_g[l], ln2_b[l])
    return x
```

```python
import functools

import jax
import jax.numpy as jnp
from jax import lax
from jax.experimental import pallas as pl
from jax.experimental.pallas import tpu as pltpu

F32 = jnp.float32
BF16 = jnp.bfloat16
I32 = jnp.int32
U32 = jnp.uint32

HEAD_DIM = 128
CHUNK = 128
ATT_BLOCK = 128
DILATIONS = (1, 4, 16)
TOP_K = 4
SWIGLU_LIMIT = 7.0
SWIGLU_ALPHA = 1.702
LN_EPS = 1e-5
NEG_INF = -1e30

V7X_VMEM_LIMIT_BYTES = 60 * 1024 * 1024

IN_PROJ_ROWS = 256
LOCAL_ROWS = 512
OUT_ROWS = 512
OUT_SUB_ROWS = 256
DISPATCH_ROWS = 256
COMBINE_ROWS = 256
DILATED_FILL_ROWS = 256
DILATED_GROUP = 16
EXPERT_ROWS = 2048
EXPERT_SUB = 256
EXPERT_FF_TILE = 512
EXPERT_OUT_TILE = 512
EXPERT_FF_BODY_TILES = 4
EXPERT_OUT_BODY_TILES = 8


def _params(semantics):
    return pltpu.CompilerParams(dimension_semantics=semantics,
                                vmem_limit_bytes=V7X_VMEM_LIMIT_BYTES)


def _full_vmem():
    return pl.BlockSpec(memory_space=pltpu.VMEM)


def _layer_norm(h, g, b):
    mu = jnp.mean(h, axis=-1, keepdims=True)
    hc = h - mu
    var = jnp.mean(hc * hc, axis=-1, keepdims=True)
    return hc * lax.rsqrt(var + LN_EPS) * g + b


def _rms_norm(y, g):
    return y * lax.rsqrt(jnp.mean(y * y, axis=-1, keepdims=True) + LN_EPS) * g


def _gelu_tanh(x):
    c = 0.7978845608028654
    return x * (0.5 * (1.0 + jnp.tanh(c * (x + 0.044715 * (x * x * x)))))


def _dot_nt(a, b):
    return lax.dot_general(a, b, (((1,), (1,)), ((), ())), preferred_element_type=F32)


def _in_proj_kernel(bounds, x_ref, w_ref, *out_refs):
    xb = x_ref[...].astype(BF16)
    for o_ref, (lo, hi) in zip(out_refs, bounds):
        o_ref[...] = jnp.dot(xb, w_ref[:, lo:hi], preferred_element_type=F32)


def _in_proj(x2, w_in_bf, widths):
    t, d = x2.shape
    tm = IN_PROJ_ROWS
    bounds, lo = [], 0
    for w in widths:
        bounds.append((lo, lo + w))
        lo += w
    return pl.pallas_call(
        functools.partial(_in_proj_kernel, tuple(bounds)),
        grid=(t // tm,),
        in_specs=[pl.BlockSpec((tm, d), lambda i: (i, 0)), _full_vmem()],
        out_specs=[pl.BlockSpec((tm, w), lambda i: (i, 0)) for w in widths],
        out_shape=[jax.ShapeDtypeStruct((t, w), F32) for w in widths],
        compiler_params=_params(("arbitrary",)),
        name="in_proj",
    )(x2, w_in_bf)


def _kv_proj_kernel(mem_ref, w_ref, kv_ref):
    kv_ref[...] = jnp.dot(mem_ref[...].astype(BF16), w_ref[...],
                          preferred_element_type=F32).astype(BF16)


def _kv_proj(mem, w_kv_bf):
    b, n_mem, d = mem.shape
    n = w_kv_bf.shape[1]
    return pl.pallas_call(
        _kv_proj_kernel,
        grid=(b,),
        in_specs=[pl.BlockSpec((None, n_mem, d), lambda i: (i, 0, 0)), _full_vmem()],
        out_specs=pl.BlockSpec((None, n_mem, n), lambda i: (i, 0, 0)),
        out_shape=jax.ShapeDtypeStruct((b, n_mem, n), BF16),
        compiler_params=_params(("arbitrary",)),
        name="kv_proj",
    )(mem, w_kv_bf)


def _local_mix_kernel(za_ref, zm_ref, kv_ref, wsp_ref, bsp_ref, lng_ref, lnb_ref, nag_ref, nmg_ref,
                      ya_ref, ym_ref, ya_acc, ym_acc):
    rows = za_ref.shape[0]
    a_width = za_ref.shape[1] // 2
    n_groups = a_width // HEAD_DIM
    n_chunks = rows // CHUNK
    m_width = zm_ref.shape[1]
    m_heads = m_width // HEAD_DIM

    act = _gelu_tanh(za_ref[...])
    u = act[:, :a_width]
    v = _layer_norm(act[:, a_width:], lng_ref[...], lnb_ref[...])
    t_idx = lax.broadcasted_iota(I32, (CHUNK, CHUNK), 0)
    s_idx = lax.broadcasted_iota(I32, (CHUNK, CHUNK), 1)
    causal = s_idx <= t_idx
    bsp = bsp_ref[...]
    for g in range(n_groups):
        cols = slice(g * HEAD_DIM, (g + 1) * HEAD_DIM)
        w_g = jnp.where(causal, wsp_ref[g], 0.0).astype(BF16)
        v_g = jnp.concatenate([v[c * CHUNK:(c + 1) * CHUNK, cols] for c in range(n_chunks)], axis=1)
        mixed = jnp.dot(w_g, v_g.astype(BF16), preferred_element_type=F32) + bsp[:, g:g + 1]
        for c in range(n_chunks):
            rs = slice(c * CHUNK, (c + 1) * CHUNK)
            ya_acc[rs, cols] = u[rs, cols] * mixed[:, c * HEAD_DIM:(c + 1) * HEAD_DIM]
    ya_ref[...] = _rms_norm(ya_acc[...], nag_ref[...]).astype(BF16)

    scale = HEAD_DIM ** -0.5
    q = zm_ref[...]
    for h in range(m_heads):
        cols = slice(h * HEAD_DIM, (h + 1) * HEAD_DIM)
        k_h = kv_ref[:, h * HEAD_DIM:(h + 1) * HEAD_DIM]
        v_h = kv_ref[:, m_width + h * HEAD_DIM:m_width + (h + 1) * HEAD_DIM]
        s = _dot_nt(q[:, cols].astype(BF16), k_h) * scale
        p = jnp.exp(s - jnp.max(s, axis=-1, keepdims=True))
        l = jnp.sum(p, axis=-1, keepdims=True)
        ym_acc[:, cols] = jnp.dot(p.astype(BF16), v_h, preferred_element_type=F32) / l
    ym_ref[...] = _rms_norm(ym_acc[...], nmg_ref[...]).astype(BF16)


def _local_mix(za, zm, kv, w_spatial, b_spatial_t, a_ln_g, a_ln_b, norm_a_g, norm_m_g, batch, seq):
    rows = LOCAL_ROWS
    a2 = za.shape[1]
    a_width = a2 // 2
    m_width = zm.shape[1]
    n_mem, kv_w = kv.shape[1], kv.shape[2]
    tiles = seq // rows
    return pl.pallas_call(
        _local_mix_kernel,
        grid=(batch, tiles),
        in_specs=[
            pl.BlockSpec((rows, a2), lambda b, i: (b * tiles + i, 0)),
            pl.BlockSpec((rows, m_width), lambda b, i: (b * tiles + i, 0)),
            pl.BlockSpec((None, n_mem, kv_w), lambda b, i: (b, 0, 0)),
            _full_vmem(), _full_vmem(), _full_vmem(), _full_vmem(), _full_vmem(), _full_vmem(),
        ],
        out_specs=[pl.BlockSpec((rows, a_width), lambda b, i: (b * tiles + i, 0)),
                   pl.BlockSpec((rows, m_width), lambda b, i: (b * tiles + i, 0))],
        out_shape=[jax.ShapeDtypeStruct((batch * seq, a_width), BF16),
                   jax.ShapeDtypeStruct((batch * seq, m_width), BF16)],
        scratch_shapes=[pltpu.VMEM((rows, a_width), F32), pltpu.VMEM((rows, m_width), F32)],
        compiler_params=_params(("arbitrary", "arbitrary")),
        name="local_mix",
    )(za, zm, kv, w_spatial, b_spatial_t, a_ln_g, a_ln_b, norm_a_g, norm_m_g)


def _dilated_kernel(q_ref, k_ref, v_ref, o_ref, qs, ks, vs, q4, k4, v4, *state):
    seq = q_ref.shape[0]
    scale = HEAD_DIM ** -0.5
    blk = ATT_BLOCK
    step = DILATIONS[1]
    quarter = seq // step
    n_pat = len(DILATIONS)
    m_s, l_s, a_s = state[:n_pat], state[n_pat:2 * n_pat], state[2 * n_pat:]
    qi = lax.broadcasted_iota(I32, (blk, 2 * blk), 0)
    kj = lax.broadcasted_iota(I32, (blk, 2 * blk), 1)
    band = (kj >= qi) & (kj <= qi + blk)
    zero_blk = jnp.zeros((blk, HEAD_DIM), BF16)
    vs[:, HEAD_DIM:] = jnp.ones((vs.shape[0], HEAD_DIM), BF16)

    for p, d in enumerate(DILATIONS):
        sub = seq // d
        n_blocks = sub // blk
        chunk = min(sub, DILATED_FILL_ROWS)

        def fill(i, c, d=d, sub=sub, chunk=chunk):
            r = i // (sub // chunk)
            j = i % (sub // chunk)
            if d == 1:
                src = pl.ds(pl.multiple_of(j * chunk, chunk), chunk)
                q, k, v = q_ref[src, :], k_ref[src, :], v_ref[src, :]
            elif d == step:
                src = pl.ds(r + j * (chunk * step), chunk, stride=step)
                q, k, v = q_ref[src, :], k_ref[src, :], v_ref[src, :]
                keep = pl.ds(pl.multiple_of(r * sub + j * chunk, chunk), chunk)
                q4[keep, :], k4[keep, :], v4[keep, :] = q, k, v
            else:
                src = pl.ds((r % step) * quarter + r // step + j * (chunk * step), chunk, stride=step)
                q, k, v = q4[src, :], k4[src, :], v4[src, :]
            dst_q = pl.multiple_of(r * sub + j * chunk, blk)
            dst_kv = pl.multiple_of(r * (sub + blk) + blk + j * chunk, blk)
            qs[pl.ds(dst_q, chunk), :] = q.astype(BF16)
            ks[pl.ds(dst_kv, chunk), :] = k.astype(BF16)
            vs[pl.ds(dst_kv, chunk), :HEAD_DIM] = v.astype(BF16)
            return c

        def pad(r, c, sub=sub):
            at = pl.multiple_of(r * (sub + blk), blk)
            ks[pl.ds(at, blk), :] = zero_blk
            vs[pl.ds(at, blk), :HEAD_DIM] = zero_blk
            return c

        lax.fori_loop(0, d, pad, 0)
        lax.fori_loop(0, d * (sub // chunk), fill, 0)

        def block(c, d=d, n_blocks=n_blocks, p=p):
            r = c // n_blocks
            n = c % n_blocks
            q = qs[pl.ds(pl.multiple_of(c * blk, blk), blk), :]
            win = pl.ds(pl.multiple_of((c + r) * blk, blk), 2 * blk)
            s = _dot_nt(q, ks[win, :]) * scale
            valid = band & (kj >= jnp.where(n > 0, 0, blk))
            s = jnp.where(valid, s, NEG_INF)
            m = jnp.max(s, axis=-1, keepdims=True)
            e = jnp.exp(s - m)
            acc = jnp.dot(e.astype(BF16), vs[win, :], preferred_element_type=F32)
            if d <= step:
                out = pl.ds(pl.multiple_of(c * blk, blk), blk)
            else:
                out = pl.ds((r % step) * quarter + n * (blk * step) + r // step, blk, stride=step)
            m_s[p][out, :] = jnp.broadcast_to(m, (blk, HEAD_DIM))
            l_s[p][out, :] = acc[:, HEAD_DIM:]
            a_s[p][out, :] = acc[:, :HEAD_DIM]

        def group(i, c, block=block):
            for g in range(DILATED_GROUP):
                block(i * DILATED_GROUP + g)
            return c

        lax.fori_loop(0, seq // blk // DILATED_GROUP, group, 0)

    def merge(j, c):
        rows = pl.ds(pl.multiple_of(j * blk, blk), blk)
        res = j // (quarter // blk)
        orig = pl.ds((j % (quarter // blk)) * (blk * step) + res, blk, stride=step)
        sel = [orig] + [rows] * (n_pat - 1)
        ms = [m[r_, :] for m, r_ in zip(m_s, sel)]
        m_all = functools.reduce(jnp.maximum, ms)
        ws = [jnp.exp(m - m_all) for m in ms]
        den = sum(w * l[r_, :] for w, l, r_ in zip(ws, l_s, sel))
        num = sum(w * a[r_, :] for w, a, r_ in zip(ws, a_s, sel))
        o_ref[orig, :] = num / den
        return c

    lax.fori_loop(0, seq // blk, merge, 0)


def _dilated_attention(zq, zk, zv, batch, seq):
    width = zq.shape[1]
    heads = width // HEAD_DIM
    assert DILATIONS == (1, DILATIONS[1], DILATIONS[1] ** 2)
    spec = pl.BlockSpec((seq, HEAD_DIM), lambda b, h: (b, h))
    kv_rows = seq + max(DILATIONS) * ATT_BLOCK
    return pl.pallas_call(
        _dilated_kernel,
        grid=(batch, heads),
        in_specs=[spec, spec, spec],
        out_specs=spec,
        out_shape=jax.ShapeDtypeStruct((batch * seq, width), F32),
        scratch_shapes=[pltpu.VMEM((seq, HEAD_DIM), BF16), pltpu.VMEM((kv_rows, HEAD_DIM), BF16),
                        pltpu.VMEM((kv_rows, 2 * HEAD_DIM), BF16)]
                       + [pltpu.VMEM((seq, HEAD_DIM), F32)] * (3 + 3 * len(DILATIONS)),
        compiler_params=_params(("arbitrary", "arbitrary")),
        name="dilated",
    )(zq, zk, zv)


def _out_router_kernel(alpha, x_ref, ya_ref, yb_ref, ym_ref, wo_ref, nbg_ref, g_ref, b_ref,
                       wrh_ref, wrl_ref, br_ref,
                       x1_ref, gate_ref, meta_ref, cnt_ref, base):
    step = pl.program_id(0)
    tm = OUT_SUB_ROWS
    a_w = ya_ref.shape[1]
    b_w = yb_ref.shape[1]
    n_exp = br_ref.shape[1]

    @pl.when(step == 0)
    def _():
        base[...] = jnp.zeros_like(base)

    lane = lax.broadcasted_iota(I32, (tm, n_exp), 1).astype(F32)
    t_i = lax.broadcasted_iota(I32, (tm, tm), 0)
    t_j = lax.broadcasted_iota(I32, (tm, tm), 1)
    earlier = (t_j < t_i).astype(BF16)
    k_lane = lax.broadcasted_iota(I32, (tm, TOP_K), 1)
    s_lane = lax.broadcasted_iota(I32, (tm, HEAD_DIM), 1)
    running = base[...]

    for h in range(x_ref.shape[0] // tm):
        rows = slice(h * tm, (h + 1) * tm)
        yb = _rms_norm(yb_ref[rows, :], nbg_ref[...]).astype(BF16)
        y = jnp.concatenate([ya_ref[rows, :], yb, ym_ref[rows, :]], axis=1)
        mix = jnp.dot(y, wo_ref[...], preferred_element_type=F32)
        x1 = _layer_norm(alpha * x_ref[rows, :] + mix, g_ref[...], b_ref[...])
        x1_ref[rows, :] = x1

        x_hi = x1.astype(BF16)
        x_lo = (x1 - x_hi.astype(F32)).astype(BF16)
        logits = (jnp.dot(x_hi, wrh_ref[...], preferred_element_type=F32)
                  + jnp.dot(x_lo, wrh_ref[...], preferred_element_type=F32)
                  + jnp.dot(x_hi, wrl_ref[...], preferred_element_type=F32)) + br_ref[...]

        work = logits
        vals, sels = [], []
        for _ in range(TOP_K):
            m = jnp.max(work, axis=-1, keepdims=True)
            sel = jnp.min(jnp.where(work == m, lane, float(n_exp)), axis=-1, keepdims=True)
            vals.append(m)
            sels.append(sel)
            work = jnp.where(lane == sel, -jnp.inf, work)
        exps = [jnp.exp(v - vals[0]) for v in vals]
        denom = exps[0] + exps[1] + exps[2] + exps[3]

        hot = [lane == s for s in sels]
        multi = sum(h_.astype(F32) for h_ in hot)
        before = jnp.dot(earlier, multi.astype(BF16), preferred_element_type=F32) + running
        gate_out = jnp.zeros((tm, TOP_K), F32)
        slots = jnp.zeros((tm, HEAD_DIM), F32)
        for k in range(TOP_K):
            rank_k = jnp.sum(jnp.where(hot[k], before, 0.0), axis=-1, keepdims=True)
            gate_out = jnp.where(k_lane == k, exps[k] / denom, gate_out)
            slots = jnp.where(s_lane == k, sels[k], slots)
            slots = jnp.where(s_lane == TOP_K + k, rank_k, slots)
        gate_ref[rows, :] = gate_out
        meta_ref[:, rows] = jnp.transpose(slots)[:2 * TOP_K, :].astype(I32)
        running = running + jnp.sum(multi, axis=0, keepdims=True)

    base[...] = running
    cnt_ref[...] = running.astype(I32)


def _out_router(alpha, x2, ya, yb, ym, w_out_bf, norm_b_g, ln_g, ln_b, wr_hi, wr_lo, b_router):
    t, d = x2.shape
    tm = OUT_ROWS
    n_exp = b_router.shape[1]
    row = lambda w: pl.BlockSpec((tm, w), lambda i: (i, 0))
    return pl.pallas_call(
        functools.partial(_out_router_kernel, alpha),
        grid=(t // tm,),
        in_specs=[row(d), row(ya.shape[1]), row(yb.shape[1]), row(ym.shape[1])] + [_full_vmem()] * 7,
        out_specs=[row(d), row(TOP_K), pl.BlockSpec((2 * TOP_K, tm), lambda i: (0, i)),
                   pl.BlockSpec((1, n_exp), lambda i: (0, 0))],
        out_shape=[jax.ShapeDtypeStruct((t, d), F32),
                   jax.ShapeDtypeStruct((t, TOP_K), F32),
                   jax.ShapeDtypeStruct((2 * TOP_K, t), I32),
                   jax.ShapeDtypeStruct((1, n_exp), I32)],
        scratch_shapes=[pltpu.VMEM((1, n_exp), F32)],
        compiler_params=_params(("arbitrary",)),
        name="out_router",
    )(x2, ya, yb, ym, w_out_bf, norm_b_g, ln_g, ln_b, wr_hi, wr_lo, b_router)


def _dispatch_kernel(ztile_ref, total_ref, dest_ref, x_ref, xs_hbm, zeros, rows, sem, zsem):
    tm, d = x_ref.shape
    sub = zeros.shape[0]
    n_tiles = xs_hbm.shape[0] // sub

    @pl.when(pl.program_id(0) == 0)
    def _():
        zeros[...] = jnp.zeros_like(zeros)

        def fill(first_row):
            return pltpu.make_async_copy(zeros, xs_hbm.at[pl.ds(pl.multiple_of(first_row, sub), sub), :], zsem)

        def partial_tiles(run):
            def step(e, c):
                @pl.when(ztile_ref[e] >= 0)
                def _():
                    run(fill(ztile_ref[e]))
                return c
            lax.fori_loop(0, ztile_ref.shape[0], step, 0)

        def tail_tiles(run):
            def step(j, c):
                run(fill(j * sub))
                return c
            lax.fori_loop(total_ref[0] // sub, n_tiles, step, 0)

        partial_tiles(lambda copy: copy.start())
        tail_tiles(lambda copy: copy.start())
        partial_tiles(lambda copy: copy.wait())
        tail_tiles(lambda copy: copy.wait())

    i = pl.program_id(0)
    buf = i % 2
    rows[buf] = x_ref[...]
    for t in range(tm):
        for k in range(TOP_K):
            pltpu.make_async_copy(rows.at[buf, pl.ds(t, 1), :], xs_hbm.at[pl.ds(dest_ref[k, t], 1), :],
                                  sem.at[buf]).start(priority=k % 2)

    def drain(b):
        for _ in range(TOP_K):
            pltpu.make_async_copy(rows.at[b], xs_hbm.at[pl.ds(0, tm), :], sem.at[b]).wait()

    @pl.when(i > 0)
    def _():
        drain(1 - buf)

    @pl.when(i == pl.num_programs(0) - 1)
    def _():
        drain(buf)


def _dispatch(x1, dest_flat, ztile, total_rows, n_rows):
    t, d = x1.shape
    tm = DISPATCH_ROWS
    grid_spec = pltpu.PrefetchScalarGridSpec(
        num_scalar_prefetch=2,
        grid=(t // tm,),
        in_specs=[pl.BlockSpec((TOP_K, tm), lambda i, z, n: (0, i), memory_space=pltpu.SMEM),
                  pl.BlockSpec((tm, d), lambda i, z, n: (i, 0))],
        out_specs=pl.BlockSpec(memory_space=pl.ANY),
        scratch_shapes=[pltpu.VMEM((EXPERT_SUB, d), F32), pltpu.VMEM((2, tm, d), F32),
                        pltpu.SemaphoreType.DMA((2,)), pltpu.SemaphoreType.DMA],
    )
    return pl.pallas_call(
        _dispatch_kernel,
        grid_spec=grid_spec,
        out_shape=jax.ShapeDtypeStruct((n_rows, d), F32),
        compiler_params=_params(("arbitrary",)),
        name="dispatch",
    )(ztile, total_rows, dest_flat, x1)


def _experts_kernel(nf, sbe_ref, sbr_ref, sbf_ref, na_ref, xs_hbm, wg_ref, wu_ref, bg_ref, bu_ref,
                    wd_ref, bd_ref, o_ref, xb, act, stage, sem):
    g = pl.program_id(0)
    s = pl.program_id(1)
    nn = pl.num_programs(1) - nf
    rows_max = xb.shape[0]
    n_slots, sub = stage.shape[0], stage.shape[1]
    n_chunks = rows_max // sub
    n_sub = sbr_ref[g] // sub
    nxt = jnp.minimum(g + 1, pl.num_programs(0) - 1)
    n_next = jnp.where(g + 1 < pl.num_programs(0), sbr_ref[nxt], 0) // sub

    def chunk_copy(sb, c, slot):
        first = pl.multiple_of(sbf_ref[sb] + c * sub, sub)
        return pltpu.make_async_copy(xs_hbm.at[pl.ds(first, sub), :], stage.at[slot], sem.at[slot])

    def land(c, slot):
        chunk_copy(0, 0, slot).wait()
        xb[pl.ds(pl.multiple_of(c * sub, sub), sub), :] = stage[slot].astype(BF16)

    def row_tiles(body, big_tiles):
        n_big = n_sub // big_tiles

        def big(j, c):
            body(pl.multiple_of(j * (big_tiles * sub), big_tiles * sub), big_tiles * sub)
            return c

        lax.fori_loop(0, n_big, big, 0)
        done = n_big * big_tiles
        for size in [big_tiles >> i for i in range(1, big_tiles.bit_length())]:
            take = (n_sub - done) >= size

            @pl.when(take)
            def _(done=done, size=size):
                body(pl.multiple_of(done * sub, size * sub), size * sub)

            done = done + jnp.where(take, size, 0)

    @pl.when(g >= na_ref[0])
    def _():
        o_ref[...] = jnp.zeros_like(o_ref)

    @pl.when(g < na_ref[0])
    def _():
        @pl.when((g == 0) & (s == 0))
        def _():
            def fetch(c, carry):
                chunk_copy(0, c, 0).start()
                land(c, 0)
                return carry
            lax.fori_loop(0, n_sub, fetch, 0)

        for slot in range(n_slots):
            @pl.when((g > 0) & (s == 0) & (nn - 1 + slot * nn < n_sub))
            def _(slot=slot):
                land(nn - 1 + slot * nn, slot)

        @pl.when(s < nf)
        def _():
            def body(first, rows):
                x = xb[pl.ds(first, rows), :]
                gate = jnp.dot(x, wg_ref[...].astype(BF16), preferred_element_type=F32) + bg_ref[...]
                up = jnp.dot(x, wu_ref[...].astype(BF16), preferred_element_type=F32) + bu_ref[...]
                gate = jnp.minimum(gate, SWIGLU_LIMIT)
                up = jnp.clip(up, -SWIGLU_LIMIT, SWIGLU_LIMIT)
                a = (up + 1.0) * (gate * jax.nn.sigmoid(SWIGLU_ALPHA * gate))
                act[s, pl.ds(first, rows), :] = a.astype(BF16)

            row_tiles(body, EXPERT_FF_BODY_TILES)

        @pl.when(s >= nf)
        def _():
            k = s - nf
            for slot in range(n_slots):
                @pl.when((k >= 1) & (k - 1 + slot * nn < n_next))
                def _(slot=slot):
                    land(k - 1 + slot * nn, slot)

                @pl.when(k + slot * nn < n_next)
                def _(slot=slot):
                    chunk_copy(nxt, k + slot * nn, slot).start()

            def body(first, rows):
                a = jnp.concatenate([act[j, pl.ds(first, rows), :] for j in range(nf)], axis=1)
                o_ref[pl.ds(first, rows), :] = jnp.dot(a, wd_ref[...].astype(BF16),
                                                       preferred_element_type=F32) + bd_ref[...]

            row_tiles(body, EXPERT_OUT_BODY_TILES)

            def clear(c, carry):
                o_ref[pl.ds(pl.multiple_of(c * sub, sub), sub), :] = jnp.zeros((sub, o_ref.shape[1]), F32)
                return carry
            lax.fori_loop(n_sub, n_chunks, clear, 0)


def _experts(xs, sb_expert, sb_rows, sb_first, n_active, n_super, w_gate_up, b_gate_up, w_down, b_down):
    d = xs.shape[1]
    n_exp, _, two_ff = w_gate_up.shape
    d_ff = two_ff // 2
    tm, tf, tn, sub = EXPERT_ROWS, EXPERT_FF_TILE, EXPERT_OUT_TILE, EXPERT_SUB
    nf, nn = d_ff // tf, d // tn
    n_slots = -(-(tm // sub) // nn)
    bgu = b_gate_up.reshape(n_exp, 1, two_ff)
    bdn = b_down.reshape(n_exp, 1, d)

    def expert(g, sbe, na):
        return sbe[jnp.minimum(g, na[0] - 1)]

    def ff_step(g, s, na):
        return jnp.where(g < na[0], jnp.minimum(s, nf - 1), nf - 1)

    def down_step(g, s, na):
        return jnp.where(g < na[0], jnp.maximum(s - nf, 0), nn - 1)

    in_specs = [
        pl.BlockSpec(memory_space=pl.ANY),
        pl.BlockSpec((None, d, tf), lambda g, s, sbe, sbr, sbf, na: (expert(g, sbe, na), 0, ff_step(g, s, na))),
        pl.BlockSpec((None, d, tf), lambda g, s, sbe, sbr, sbf, na: (expert(g, sbe, na), 0, nf + ff_step(g, s, na))),
        pl.BlockSpec((None, 1, tf), lambda g, s, sbe, sbr, sbf, na: (expert(g, sbe, na), 0, ff_step(g, s, na))),
        pl.BlockSpec((None, 1, tf), lambda g, s, sbe, sbr, sbf, na: (expert(g, sbe, na), 0, nf + ff_step(g, s, na))),
        pl.BlockSpec((None, d_ff, tn), lambda g, s, sbe, sbr, sbf, na: (expert(g, sbe, na), 0, down_step(g, s, na))),
        pl.BlockSpec((None, 1, tn), lambda g, s, sbe, sbr, sbf, na: (expert(g, sbe, na), 0, down_step(g, s, na))),
    ]
    grid_spec = pltpu.PrefetchScalarGridSpec(
        num_scalar_prefetch=4,
        grid=(n_super, nf + nn),
        in_specs=in_specs,
        out_specs=pl.BlockSpec((tm, tn), lambda g, s, sbe, sbr, sbf, na: (g, jnp.maximum(s - nf, 0))),
        scratch_shapes=[pltpu.VMEM((tm, d), BF16), pltpu.VMEM((nf, tm, tf), BF16),
                        pltpu.VMEM((n_slots, sub, d), F32), pltpu.SemaphoreType.DMA((n_slots,))],
    )
    return pl.pallas_call(
        functools.partial(_experts_kernel, nf),
        grid_spec=grid_spec,
        out_shape=jax.ShapeDtypeStruct((n_super * tm, d), F32),
        compiler_params=_params(("arbitrary", "arbitrary")),
        name="experts",
    )(sb_expert, sb_rows, sb_first, n_active, xs, w_gate_up, w_gate_up, bgu, bgu, w_down, bdn)


def _combine_kernel(alpha, dest_ref, next_ref, x1_ref, gate_ref, g_ref, b_ref, ys_hbm, o_ref, rows, sem):
    i = pl.program_id(0)
    tm, d = x1_ref.shape

    def gather(slots_ref, buf):
        for t in range(tm):
            for k in range(TOP_K):
                pltpu.make_async_copy(ys_hbm.at[pl.ds(slots_ref[k, t], 1), :],
                                      rows.at[buf, k, pl.ds(t, 1), :],
                                      sem.at[buf]).start(priority=k % 2)

    @pl.when(i == 0)
    def _():
        gather(dest_ref, 0)

    @pl.when(i + 1 < pl.num_programs(0))
    def _():
        gather(next_ref, (i + 1) % 2)

    buf = i % 2
    for k in range(TOP_K):
        pltpu.make_async_copy(ys_hbm.at[pl.ds(0, tm), :], rows.at[buf, k], sem.at[buf]).wait()

    gates = gate_ref[...]
    ffn = gates[:, 0:1] * rows[buf, 0]
    for k in range(1, TOP_K):
        ffn += gates[:, k:k + 1] * rows[buf, k]
    o_ref[...] = _layer_norm(alpha * x1_ref[...] + ffn, g_ref[...], b_ref[...])


def _combine(alpha, dest_flat, x1, gates, ln_g, ln_b, ys):
    t, d = x1.shape
    tm = COMBINE_ROWS
    return pl.pallas_call(
        functools.partial(_combine_kernel, alpha),
        grid=(t // tm,),
        in_specs=[pl.BlockSpec((TOP_K, tm), lambda i: (0, i), memory_space=pltpu.SMEM),
                  pl.BlockSpec((TOP_K, tm), lambda i: (0, jnp.minimum(i + 1, t // tm - 1)),
                               memory_space=pltpu.SMEM),
                  pl.BlockSpec((tm, d), lambda i: (i, 0)),
                  pl.BlockSpec((tm, TOP_K), lambda i: (i, 0)),
                  _full_vmem(), _full_vmem(),
                  pl.BlockSpec(memory_space=pl.ANY)],
        out_specs=pl.BlockSpec((tm, d), lambda i: (i, 0)),
        out_shape=jax.ShapeDtypeStruct((t, d), F32),
        scratch_shapes=[pltpu.VMEM((2, TOP_K, tm, d), F32), pltpu.SemaphoreType.DMA((2,))],
        compiler_params=_params(("arbitrary",)),
        name="combine",
    )(dest_flat, dest_flat, x1, gates, ln_g, ln_b, ys)


def _routing_tables(meta, counts, n_super):
    tm, sub = EXPERT_ROWS, EXPERT_SUB
    n_exp = counts.shape[0]
    tiles = (counts + sub - 1) // sub
    padded = tiles * sub
    row_end = jnp.cumsum(padded)
    row_start = row_end - padded
    per_expert = (tiles + tm // sub - 1) // (tm // sub)
    size = jnp.maximum((tiles + per_expert - 1) // jnp.maximum(per_expert, 1), 1) * sub
    sb_end = jnp.cumsum(per_expert)
    sb_start = sb_end - per_expert
    idx, rank = meta[:TOP_K], meta[TOP_K:]
    hit = idx[None] == jnp.arange(n_exp, dtype=I32)[:, None, None]

    def lookup(table):
        return jnp.sum(jnp.where(hit, table[:, None, None], 0), axis=0)

    size_t = lookup(size)
    dest_x = (lookup(row_start) + rank).astype(I32)
    dest_y = ((lookup(sb_start) + rank // size_t) * tm + rank % size_t).astype(I32)
    ztile = jnp.where(counts % sub != 0, row_start + counts // sub * sub, -1).astype(I32)
    g = jnp.arange(n_super, dtype=I32)
    owner = jnp.sum((g[:, None] >= sb_end[None, :]).astype(I32), axis=1)
    sb_expert = jnp.minimum(owner, n_exp - 1).astype(I32)
    local = (g - sb_start[sb_expert]) * size[sb_expert]
    active = g < sb_end[-1]
    sb_rows = jnp.where(active, jnp.clip(padded[sb_expert] - local, 0, size[sb_expert]), 0).astype(I32)
    sb_first = jnp.where(active, row_start[sb_expert] + local, 0).astype(I32)
    return (dest_x, dest_y, ztile, row_end[-1:].astype(I32), sb_expert, sb_rows, sb_first,
            sb_end[-1:].astype(I32))


def _layer(alpha, x, mem, w_in, w_spatial, b_spatial, a_ln_g, a_ln_b, w_mem_kv, norm_a_g, norm_b_g,
           norm_m_g, w_out, ln1_g, ln1_b, w_router, b_router, w_gate_up, b_gate_up, w_down, b_down,
           ln2_g, ln2_b):
    batch, seq, d = x.shape
    t = batch * seq
    m_width = w_mem_kv.shape[1] // 2
    a_width = a_ln_g.shape[0]
    b_width = norm_b_g.shape[0]
    n_exp = w_router.shape[1]
    row = lambda v: v.reshape(1, -1)

    x2 = x.reshape(t, d)
    za, zq, zk, zv, zm = _in_proj(x2, w_in.astype(BF16), (2 * a_width, b_width, b_width, b_width, m_width))
    kv = _kv_proj(mem, w_mem_kv.astype(BF16))
    ya, ym = _local_mix(za, zm, kv, w_spatial, b_spatial.T, row(a_ln_g), row(a_ln_b),
                        row(norm_a_g), row(norm_m_g), batch, seq)
    yb = _dilated_attention(zq, zk, zv, batch, seq)

    wr_hi = w_router.astype(BF16)
    wr_lo = (w_router - wr_hi.astype(F32)).astype(BF16)
    x1, gates, meta, counts = _out_router(
        alpha, x2, ya, yb, ym, w_out.astype(BF16), row(norm_b_g), row(ln1_g), row(ln1_b),
        wr_hi, wr_lo, row(b_router))

    n_assign = t * TOP_K
    n_super = (n_assign + n_exp * (EXPERT_ROWS - 1)) // EXPERT_ROWS
    n_x_rows = (n_assign + n_exp * (EXPERT_SUB - 1)) // EXPERT_SUB * EXPERT_SUB
    dest_x, dest_y, ztile, total_rows, sb_expert, sb_rows, sb_first, n_active = _routing_tables(
        meta, counts.reshape(n_exp), n_super)
    xs = _dispatch(x1, dest_x, ztile, total_rows, n_x_rows)
    ys = _experts(xs, sb_expert, sb_rows, sb_first, n_active, n_super, w_gate_up, b_gate_up, w_down, b_down)
    out = _combine(alpha, dest_y, x1, gates, row(ln2_g), row(ln2_b), ys)
    return out.reshape(batch, seq, d)


def kernel(x, mem, w_in, w_spatial, b_spatial, a_ln_g, a_ln_b, w_mem_kv, norm_a_g, norm_b_g, norm_m_g, w_out, ln1_g, ln1_b, w_router, b_router, w_gate_up, b_gate_up, w_down, b_down, ln2_g, ln2_b):
    depth = w_in.shape[0]
    alpha = (2.0 * depth) ** 0.25
    for l in range(depth):
        x = _layer(alpha, x, mem, w_in[l], w_spatial[l], b_spatial[l], a_ln_g[l], a_ln_b[l], w_mem_kv[l],
                   norm_a_g[l], norm_b_g[l], norm_m_g[l], w_out[l], ln1_g[l], ln1_b[l], w_router[l],
                   b_router[l], w_gate_up[l], b_gate_up[l], w_down[l], b_down[l], ln2_g[l], ln2_b[l])
    return x
```

```python
import functools

import jax
import jax.numpy as jnp
from jax import lax
from jax.experimental import pallas as pl
from jax.experimental.pallas import tpu as pltpu

F32 = jnp.float32
BF16 = jnp.bfloat16
I32 = jnp.int32
U32 = jnp.uint32

HEAD_DIM = 128
CHUNK = 128
ATT_BLOCK = 128
DILATIONS = (1, 4, 16)
TOP_K = 4
SWIGLU_LIMIT = 7.0
SWIGLU_ALPHA = 1.702
LN_EPS = 1e-5
NEG_INF = -1e30

V7X_VMEM_LIMIT_BYTES = 60 * 1024 * 1024

IN_PROJ_ROWS = 256
LOCAL_ROWS = 512
OUT_ROWS = 512
OUT_SUB_ROWS = 256
DISPATCH_ROWS = 512
COMBINE_ROWS = 512
DILATED_FILL_ROWS = 256
DILATED_GROUP = 16
EXPERT_ROWS = 2048
EXPERT_SUB = 256
EXPERT_FF_TILE = 512
EXPERT_OUT_TILE = 512
EXPERT_FF_BODY_TILES = 4
EXPERT_OUT_BODY_TILES = 8


def _params(semantics):
    return pltpu.CompilerParams(dimension_semantics=semantics,
                                vmem_limit_bytes=V7X_VMEM_LIMIT_BYTES)


def _full_vmem():
    return pl.BlockSpec(memory_space=pltpu.VMEM)


def _layer_norm(h, g, b):
    mu = jnp.mean(h, axis=-1, keepdims=True)
    hc = h - mu
    var = jnp.mean(hc * hc, axis=-1, keepdims=True)
    return hc * lax.rsqrt(var + LN_EPS) * g + b


def _rms_norm(y, g):
    return y * lax.rsqrt(jnp.mean(y * y, axis=-1, keepdims=True) + LN_EPS) * g


def _gelu_tanh(x):
    c = 0.7978845608028654
    return x * (0.5 * (1.0 + jnp.tanh(c * (x + 0.044715 * (x * x * x)))))


def _dot_nt(a, b):
    return lax.dot_general(a, b, (((1,), (1,)), ((), ())), preferred_element_type=F32)


def _in_proj_kernel(bounds, x_ref, w_ref, *out_refs):
    xb = x_ref[...].astype(BF16)
    for o_ref, (lo, hi) in zip(out_refs, bounds):
        o_ref[...] = jnp.dot(xb, w_ref[:, lo:hi], preferred_element_type=F32)


def _in_proj(x2, w_in_bf, widths):
    t, d = x2.shape
    tm = IN_PROJ_ROWS
    bounds, lo = [], 0
    for w in widths:
        bounds.append((lo, lo + w))
        lo += w
    return pl.pallas_call(
        functools.partial(_in_proj_kernel, tuple(bounds)),
        grid=(t // tm,),
        in_specs=[pl.BlockSpec((tm, d), lambda i: (i, 0)), _full_vmem()],
        out_specs=[pl.BlockSpec((tm, w), lambda i: (i, 0)) for w in widths],
        out_shape=[jax.ShapeDtypeStruct((t, w), F32) for w in widths],
        compiler_params=_params(("arbitrary",)),
        name="in_proj",
    )(x2, w_in_bf)


def _kv_proj_kernel(mem_ref, w_ref, kv_ref):
    kv_ref[...] = jnp.dot(mem_ref[...].astype(BF16), w_ref[...],
                          preferred_element_type=F32).astype(BF16)


def _kv_proj(mem, w_kv_bf):
    b, n_mem, d = mem.shape
    n = w_kv_bf.shape[1]
    return pl.pallas_call(
        _kv_proj_kernel,
        grid=(b,),
        in_specs=[pl.BlockSpec((None, n_mem, d), lambda i: (i, 0, 0)), _full_vmem()],
        out_specs=pl.BlockSpec((None, n_mem, n), lambda i: (i, 0, 0)),
        out_shape=jax.ShapeDtypeStruct((b, n_mem, n), BF16),
        compiler_params=_params(("arbitrary",)),
        name="kv_proj",
    )(mem, w_kv_bf)


def _local_mix_kernel(za_ref, zm_ref, kv_ref, wsp_ref, bsp_ref, lng_ref, lnb_ref, nag_ref, nmg_ref,
                      ya_ref, ym_ref, ya_acc, ym_acc):
    rows = za_ref.shape[0]
    a_width = za_ref.shape[1] // 2
    n_groups = a_width // HEAD_DIM
    n_chunks = rows // CHUNK
    m_width = zm_ref.shape[1]
    m_heads = m_width // HEAD_DIM

    act = _gelu_tanh(za_ref[...])
    u = act[:, :a_width]
    v = _layer_norm(act[:, a_width:], lng_ref[...], lnb_ref[...])
    t_idx = lax.broadcasted_iota(I32, (CHUNK, CHUNK), 0)
    s_idx = lax.broadcasted_iota(I32, (CHUNK, CHUNK), 1)
    causal = s_idx <= t_idx
    bsp = bsp_ref[...]
    for g in range(n_groups):
        cols = slice(g * HEAD_DIM, (g + 1) * HEAD_DIM)
        w_g = jnp.where(causal, wsp_ref[g], 0.0).astype(BF16)
        v_g = jnp.concatenate([v[c * CHUNK:(c + 1) * CHUNK, cols] for c in range(n_chunks)], axis=1)
        mixed = jnp.dot(w_g, v_g.astype(BF16), preferred_element_type=F32) + bsp[:, g:g + 1]
        for c in range(n_chunks):
            rs = slice(c * CHUNK, (c + 1) * CHUNK)
            ya_acc[rs, cols] = u[rs, cols] * mixed[:, c * HEAD_DIM:(c + 1) * HEAD_DIM]
    ya_ref[...] = _rms_norm(ya_acc[...], nag_ref[...]).astype(BF16)

    scale = HEAD_DIM ** -0.5
    q = zm_ref[...]
    for h in range(m_heads):
        cols = slice(h * HEAD_DIM, (h + 1) * HEAD_DIM)
        k_h = kv_ref[:, h * HEAD_DIM:(h + 1) * HEAD_DIM]
        v_h = kv_ref[:, m_width + h * HEAD_DIM:m_width + (h + 1) * HEAD_DIM]
        s = _dot_nt(q[:, cols].astype(BF16), k_h) * scale
        p = jnp.exp(s - jnp.max(s, axis=-1, keepdims=True))
        l = jnp.sum(p, axis=-1, keepdims=True)
        ym_acc[:, cols] = jnp.dot(p.astype(BF16), v_h, preferred_element_type=F32) / l
    ym_ref[...] = _rms_norm(ym_acc[...], nmg_ref[...]).astype(BF16)


def _local_mix(za, zm, kv, w_spatial, b_spatial_t, a_ln_g, a_ln_b, norm_a_g, norm_m_g, batch, seq):
    rows = LOCAL_ROWS
    a2 = za.shape[1]
    a_width = a2 // 2
    m_width = zm.shape[1]
    n_mem, kv_w = kv.shape[1], kv.shape[2]
    tiles = seq // rows
    return pl.pallas_call(
        _local_mix_kernel,
        grid=(batch, tiles),
        in_specs=[
            pl.BlockSpec((rows, a2), lambda b, i: (b * tiles + i, 0)),
            pl.BlockSpec((rows, m_width), lambda b, i: (b * tiles + i, 0)),
            pl.BlockSpec((None, n_mem, kv_w), lambda b, i: (b, 0, 0)),
            _full_vmem(), _full_vmem(), _full_vmem(), _full_vmem(), _full_vmem(), _full_vmem(),
        ],
        out_specs=[pl.BlockSpec((rows, a_width), lambda b, i: (b * tiles + i, 0)),
                   pl.BlockSpec((rows, m_width), lambda b, i: (b * tiles + i, 0))],
        out_shape=[jax.ShapeDtypeStruct((batch * seq, a_width), BF16),
                   jax.ShapeDtypeStruct((batch * seq, m_width), BF16)],
        scratch_shapes=[pltpu.VMEM((rows, a_width), F32), pltpu.VMEM((rows, m_width), F32)],
        compiler_params=_params(("arbitrary", "arbitrary")),
        name="local_mix",
    )(za, zm, kv, w_spatial, b_spatial_t, a_ln_g, a_ln_b, norm_a_g, norm_m_g)


def _dilated_kernel(q_ref, k_ref, v_ref, o_ref, qs, ks, vs, q4, k4, v4, *state):
    seq = q_ref.shape[0]
    scale = HEAD_DIM ** -0.5
    blk = ATT_BLOCK
    step = DILATIONS[1]
    quarter = seq // step
    n_pat = len(DILATIONS)
    m_s, l_s, a_s = state[:n_pat], state[n_pat:2 * n_pat], state[2 * n_pat:]
    qi = lax.broadcasted_iota(I32, (blk, 2 * blk), 0)
    kj = lax.broadcasted_iota(I32, (blk, 2 * blk), 1)
    band = (kj >= qi) & (kj <= qi + blk)
    zero_blk = jnp.zeros((blk, HEAD_DIM), BF16)
    vs[:, HEAD_DIM:] = jnp.ones((vs.shape[0], HEAD_DIM), BF16)

    for p, d in enumerate(DILATIONS):
        sub = seq // d
        n_blocks = sub // blk
        chunk = min(sub, DILATED_FILL_ROWS)

        def fill(i, c, d=d, sub=sub, chunk=chunk):
            r = i // (sub // chunk)
            j = i % (sub // chunk)
            if d == 1:
                src = pl.ds(pl.multiple_of(j * chunk, chunk), chunk)
                q, k, v = q_ref[src, :], k_ref[src, :], v_ref[src, :]
            elif d == step:
                src = pl.ds(r + j * (chunk * step), chunk, stride=step)
                q, k, v = q_ref[src, :], k_ref[src, :], v_ref[src, :]
                keep = pl.ds(pl.multiple_of(r * sub + j * chunk, chunk), chunk)
                q4[keep, :], k4[keep, :], v4[keep, :] = q, k, v
            else:
                src = pl.ds((r % step) * quarter + r // step + j * (chunk * step), chunk, stride=step)
                q, k, v = q4[src, :], k4[src, :], v4[src, :]
            dst_q = pl.multiple_of(r * sub + j * chunk, blk)
            dst_kv = pl.multiple_of(r * (sub + blk) + blk + j * chunk, blk)
            qs[pl.ds(dst_q, chunk), :] = q.astype(BF16)
            ks[pl.ds(dst_kv, chunk), :] = k.astype(BF16)
            vs[pl.ds(dst_kv, chunk), :HEAD_DIM] = v.astype(BF16)
            return c

        def pad(r, c, sub=sub):
            at = pl.multiple_of(r * (sub + blk), blk)
            ks[pl.ds(at, blk), :] = zero_blk
            vs[pl.ds(at, blk), :HEAD_DIM] = zero_blk
            return c

        lax.fori_loop(0, d, pad, 0)
        lax.fori_loop(0, d * (sub // chunk), fill, 0)

        def block(c, d=d, n_blocks=n_blocks, p=p):
            r = c // n_blocks
            n = c % n_blocks
            q = qs[pl.ds(pl.multiple_of(c * blk, blk), blk), :]
            win = pl.ds(pl.multiple_of((c + r) * blk, blk), 2 * blk)
            s = _dot_nt(q, ks[win, :]) * scale
            valid = band & (kj >= jnp.where(n > 0, 0, blk))
            s = jnp.where(valid, s, NEG_INF)
            m = jnp.max(s, axis=-1, keepdims=True)
            e = jnp.exp(s - m)
            acc = jnp.dot(e.astype(BF16), vs[win, :], preferred_element_type=F32)
            if d <= step:
                out = pl.ds(pl.multiple_of(c * blk, blk), blk)
            else:
                out = pl.ds((r % step) * quarter + n * (blk * step) + r // step, blk, stride=step)
            m_s[p][out, :] = jnp.broadcast_to(m, (blk, HEAD_DIM))
            l_s[p][out, :] = acc[:, HEAD_DIM:]
            a_s[p][out, :] = acc[:, :HEAD_DIM]

        def group(i, c, block=block):
            for g in range(DILATED_GROUP):
                block(i * DILATED_GROUP + g)
            return c

        lax.fori_loop(0, seq // blk // DILATED_GROUP, group, 0)

    def merge(j, c):
        rows = pl.ds(pl.multiple_of(j * blk, blk), blk)
        res = j // (quarter // blk)
        orig = pl.ds((j % (quarter // blk)) * (blk * step) + res, blk, stride=step)
        sel = [orig] + [rows] * (n_pat - 1)
        ms = [m[r_, :] for m, r_ in zip(m_s, sel)]
        m_all = functools.reduce(jnp.maximum, ms)
        ws = [jnp.exp(m - m_all) for m in ms]
        den = sum(w * l[r_, :] for w, l, r_ in zip(ws, l_s, sel))
        num = sum(w * a[r_, :] for w, a, r_ in zip(ws, a_s, sel))
        o_ref[orig, :] = num / den
        return c

    lax.fori_loop(0, seq // blk, merge, 0)


def _dilated_attention(zq, zk, zv, batch, seq):
    width = zq.shape[1]
    heads = width // HEAD_DIM
    assert DILATIONS == (1, DILATIONS[1], DILATIONS[1] ** 2)
    spec = pl.BlockSpec((seq, HEAD_DIM), lambda b, h: (b, h))
    kv_rows = seq + max(DILATIONS) * ATT_BLOCK
    return pl.pallas_call(
        _dilated_kernel,
        grid=(batch, heads),
        in_specs=[spec, spec, spec],
        out_specs=spec,
        out_shape=jax.ShapeDtypeStruct((batch * seq, width), F32),
        scratch_shapes=[pltpu.VMEM((seq, HEAD_DIM), BF16), pltpu.VMEM((kv_rows, HEAD_DIM), BF16),
                        pltpu.VMEM((kv_rows, 2 * HEAD_DIM), BF16)]
                       + [pltpu.VMEM((seq, HEAD_DIM), F32)] * (3 + 3 * len(DILATIONS)),
        compiler_params=_params(("arbitrary", "arbitrary")),
        name="dilated",
    )(zq, zk, zv)


def _out_router_kernel(alpha, x_ref, ya_ref, yb_ref, ym_ref, wo_ref, nbg_ref, g_ref, b_ref,
                       wrh_ref, wrl_ref, br_ref,
                       x1_ref, gate_ref, meta_ref, cnt_ref, base):
    step = pl.program_id(0)
    tm = OUT_SUB_ROWS
    a_w = ya_ref.shape[1]
    b_w = yb_ref.shape[1]
    n_exp = br_ref.shape[1]

    @pl.when(step == 0)
    def _():
        base[...] = jnp.zeros_like(base)

    lane = lax.broadcasted_iota(I32, (tm, n_exp), 1).astype(F32)
    t_i = lax.broadcasted_iota(I32, (tm, tm), 0)
    t_j = lax.broadcasted_iota(I32, (tm, tm), 1)
    earlier = (t_j < t_i).astype(BF16)
    k_lane = lax.broadcasted_iota(I32, (tm, TOP_K), 1)
    s_lane = lax.broadcasted_iota(I32, (tm, HEAD_DIM), 1)
    running = base[...]

    for h in range(x_ref.shape[0] // tm):
        rows = slice(h * tm, (h + 1) * tm)
        yb = _rms_norm(yb_ref[rows, :], nbg_ref[...]).astype(BF16)
        y = jnp.concatenate([ya_ref[rows, :], yb, ym_ref[rows, :]], axis=1)
        mix = jnp.dot(y, wo_ref[...], preferred_element_type=F32)
        x1 = _layer_norm(alpha * x_ref[rows, :] + mix, g_ref[...], b_ref[...])
        x1_ref[rows, :] = x1

        x_hi = x1.astype(BF16)
        x_lo = (x1 - x_hi.astype(F32)).astype(BF16)
        logits = (jnp.dot(x_hi, wrh_ref[...], preferred_element_type=F32)
                  + jnp.dot(x_lo, wrh_ref[...], preferred_element_type=F32)
                  + jnp.dot(x_hi, wrl_ref[...], preferred_element_type=F32)) + br_ref[...]

        work = logits
        vals, sels = [], []
        for _ in range(TOP_K):
            m = jnp.max(work, axis=-1, keepdims=True)
            sel = jnp.min(jnp.where(work == m, lane, float(n_exp)), axis=-1, keepdims=True)
            vals.append(m)
            sels.append(sel)
            work = jnp.where(lane == sel, -jnp.inf, work)
        exps = [jnp.exp(v - vals[0]) for v in vals]
        denom = exps[0] + exps[1] + exps[2] + exps[3]

        hot = [lane == s for s in sels]
        multi = sum(h_.astype(F32) for h_ in hot)
        before = jnp.dot(earlier, multi.astype(BF16), preferred_element_type=F32) + running
        gate_out = jnp.zeros((tm, TOP_K), F32)
        slots = jnp.zeros((tm, HEAD_DIM), F32)
        for k in range(TOP_K):
            rank_k = jnp.sum(jnp.where(hot[k], before, 0.0), axis=-1, keepdims=True)
            gate_out = jnp.where(k_lane == k, exps[k] / denom, gate_out)
            slots = jnp.where(s_lane == k, sels[k], slots)
            slots = jnp.where(s_lane == TOP_K + k, rank_k, slots)
        gate_ref[rows, :] = gate_out
        meta_ref[:, rows] = jnp.transpose(slots)[:2 * TOP_K, :].astype(I32)
        running = running + jnp.sum(multi, axis=0, keepdims=True)

    base[...] = running
    cnt_ref[...] = running.astype(I32)


def _out_router(alpha, x2, ya, yb, ym, w_out_bf, norm_b_g, ln_g, ln_b, wr_hi, wr_lo, b_router):
    t, d = x2.shape
    tm = OUT_ROWS
    n_exp = b_router.shape[1]
    row = lambda w: pl.BlockSpec((tm, w), lambda i: (i, 0))
    return pl.pallas_call(
        functools.partial(_out_router_kernel, alpha),
        grid=(t // tm,),
        in_specs=[row(d), row(ya.shape[1]), row(yb.shape[1]), row(ym.shape[1])] + [_full_vmem()] * 7,
        out_specs=[row(d), row(TOP_K), pl.BlockSpec((2 * TOP_K, tm), lambda i: (0, i)),
                   pl.BlockSpec((1, n_exp), lambda i: (0, 0))],
        out_shape=[jax.ShapeDtypeStruct((t, d), F32),
                   jax.ShapeDtypeStruct((t, TOP_K), F32),
                   jax.ShapeDtypeStruct((2 * TOP_K, t), I32),
                   jax.ShapeDtypeStruct((1, n_exp), I32)],
        scratch_shapes=[pltpu.VMEM((1, n_exp), F32)],
        compiler_params=_params(("arbitrary",)),
        name="out_router",
    )(x2, ya, yb, ym, w_out_bf, norm_b_g, ln_g, ln_b, wr_hi, wr_lo, b_router)


def _dispatch_kernel(ztile_ref, total_ref, dest_ref, x_ref, xs_hbm, zeros, rows, sem, zsem):
    tm, d = x_ref.shape
    sub = zeros.shape[0]
    n_tiles = xs_hbm.shape[0] // sub

    @pl.when(pl.program_id(0) == 0)
    def _():
        zeros[...] = jnp.zeros_like(zeros)

        def fill(first_row):
            return pltpu.make_async_copy(zeros, xs_hbm.at[pl.ds(pl.multiple_of(first_row, sub), sub), :], zsem)

        def partial_tiles(run):
            def step(e, c):
                @pl.when(ztile_ref[e] >= 0)
                def _():
                    run(fill(ztile_ref[e]))
                return c
            lax.fori_loop(0, ztile_ref.shape[0], step, 0)

        def tail_tiles(run):
            def step(j, c):
                run(fill(j * sub))
                return c
            lax.fori_loop(total_ref[0] // sub, n_tiles, step, 0)

        partial_tiles(lambda copy: copy.start())
        tail_tiles(lambda copy: copy.start())
        partial_tiles(lambda copy: copy.wait())
        tail_tiles(lambda copy: copy.wait())

    i = pl.program_id(0)
    buf = i % 2
    rows[buf] = x_ref[...]
    for t in range(tm):
        for k in range(TOP_K):
            pltpu.make_async_copy(rows.at[buf, pl.ds(t, 1), :], xs_hbm.at[pl.ds(dest_ref[k, t], 1), :],
                                  sem.at[buf]).start(priority=k % 2)

    def drain(b):
        for _ in range(TOP_K):
            pltpu.make_async_copy(rows.at[b], xs_hbm.at[pl.ds(0, tm), :], sem.at[b]).wait()

    @pl.when(i > 0)
    def _():
        drain(1 - buf)

    @pl.when(i == pl.num_programs(0) - 1)
    def _():
        drain(buf)


def _dispatch(x1, dest_flat, ztile, total_rows, n_rows):
    t, d = x1.shape
    tm = DISPATCH_ROWS
    grid_spec = pltpu.PrefetchScalarGridSpec(
        num_scalar_prefetch=2,
        grid=(t // tm,),
        in_specs=[pl.BlockSpec((TOP_K, tm), lambda i, z, n: (0, i), memory_space=pltpu.SMEM),
                  pl.BlockSpec((tm, d), lambda i, z, n: (i, 0))],
        out_specs=pl.BlockSpec(memory_space=pl.ANY),
        scratch_shapes=[pltpu.VMEM((EXPERT_SUB, d), F32), pltpu.VMEM((2, tm, d), F32),
                        pltpu.SemaphoreType.DMA((2,)), pltpu.SemaphoreType.DMA],
    )
    return pl.pallas_call(
        _dispatch_kernel,
        grid_spec=grid_spec,
        out_shape=jax.ShapeDtypeStruct((n_rows, d), F32),
        compiler_params=_params(("arbitrary",)),
        name="dispatch",
    )(ztile, total_rows, dest_flat, x1)


def _experts_kernel(nf, sbe_ref, sbr_ref, sbf_ref, na_ref, xs_hbm, wg_ref, wu_ref, bg_ref, bu_ref,
                    wd_ref, bd_ref, o_ref, xb, act, stage, sem):
    g = pl.program_id(0)
    s = pl.program_id(1)
    nn = pl.num_programs(1) - nf
    rows_max = xb.shape[0]
    n_slots, sub = stage.shape[0], stage.shape[1]
    n_chunks = rows_max // sub
    n_sub = sbr_ref[g] // sub
    nxt = jnp.minimum(g + 1, pl.num_programs(0) - 1)
    n_next = jnp.where(g + 1 < pl.num_programs(0), sbr_ref[nxt], 0) // sub

    def chunk_copy(sb, c, slot):
        first = pl.multiple_of(sbf_ref[sb] + c * sub, sub)
        return pltpu.make_async_copy(xs_hbm.at[pl.ds(first, sub), :], stage.at[slot], sem.at[slot])

    def land(c, slot):
        chunk_copy(0, 0, slot).wait()
        xb[pl.ds(pl.multiple_of(c * sub, sub), sub), :] = stage[slot].astype(BF16)

    def row_tiles(body, big_tiles):
        n_big = n_sub // big_tiles

        def big(j, c):
            body(pl.multiple_of(j * (big_tiles * sub), big_tiles * sub), big_tiles * sub)
            return c

        lax.fori_loop(0, n_big, big, 0)
        done = n_big * big_tiles
        for size in [big_tiles >> i for i in range(1, big_tiles.bit_length())]:
            take = (n_sub - done) >= size

            @pl.when(take)
            def _(done=done, size=size):
                body(pl.multiple_of(done * sub, size * sub), size * sub)

            done = done + jnp.where(take, size, 0)

    @pl.when(g >= na_ref[0])
    def _():
        o_ref[...] = jnp.zeros_like(o_ref)

    @pl.when(g < na_ref[0])
    def _():
        @pl.when((g == 0) & (s == 0))
        def _():
            def fetch(c, carry):
                chunk_copy(0, c, 0).start()
                land(c, 0)
                return carry
            lax.fori_loop(0, n_sub, fetch, 0)

        for slot in range(n_slots):
            @pl.when((g > 0) & (s == 0) & (nn - 1 + slot * nn < n_sub))
            def _(slot=slot):
                land(nn - 1 + slot * nn, slot)

        @pl.when(s < nf)
        def _():
            def body(first, rows):
                x = xb[pl.ds(first, rows), :]
                gate = jnp.dot(x, wg_ref[...].astype(BF16), preferred_element_type=F32) + bg_ref[...]
                up = jnp.dot(x, wu_ref[...].astype(BF16), preferred_element_type=F32) + bu_ref[...]
                gate = jnp.minimum(gate, SWIGLU_LIMIT)
                up = jnp.clip(up, -SWIGLU_LIMIT, SWIGLU_LIMIT)
                a = (up + 1.0) * (gate * jax.nn.sigmoid(SWIGLU_ALPHA * gate))
                act[s, pl.ds(first, rows), :] = a.astype(BF16)

            row_tiles(body, EXPERT_FF_BODY_TILES)

        @pl.when(s >= nf)
        def _():
            k = s - nf
            for slot in range(n_slots):
                @pl.when((k >= 1) & (k - 1 + slot * nn < n_next))
                def _(slot=slot):
                    land(k - 1 + slot * nn, slot)

                @pl.when(k + slot * nn < n_next)
                def _(slot=slot):
                    chunk_copy(nxt, k + slot * nn, slot).start()

            def body(first, rows):
                a = jnp.concatenate([act[j, pl.ds(first, rows), :] for j in range(nf)], axis=1)
                o_ref[pl.ds(first, rows), :] = jnp.dot(a, wd_ref[...].astype(BF16),
                                                       preferred_element_type=F32) + bd_ref[...]

            row_tiles(body, EXPERT_OUT_BODY_TILES)

            def clear(c, carry):
                o_ref[pl.ds(pl.multiple_of(c * sub, sub), sub), :] = jnp.zeros((sub, o_ref.shape[1]), F32)
                return carry
            lax.fori_loop(n_sub, n_chunks, clear, 0)


def _experts(xs, sb_expert, sb_rows, sb_first, n_active, n_super, w_gate_up, b_gate_up, w_down, b_down):
    d = xs.shape[1]
    n_exp, _, two_ff = w_gate_up.shape
    d_ff = two_ff // 2
    tm, tf, tn, sub = EXPERT_ROWS, EXPERT_FF_TILE, EXPERT_OUT_TILE, EXPERT_SUB
    nf, nn = d_ff // tf, d // tn
    n_slots = -(-(tm // sub) // nn)
    bgu = b_gate_up.reshape(n_exp, 1, two_ff)
    bdn = b_down.reshape(n_exp, 1, d)

    def expert(g, sbe, na):
        return sbe[jnp.minimum(g, na[0] - 1)]

    def ff_step(g, s, na):
        return jnp.where(g < na[0], jnp.minimum(s, nf - 1), nf - 1)

    def down_step(g, s, na):
        return jnp.where(g < na[0], jnp.maximum(s - nf, 0), nn - 1)

    in_specs = [
        pl.BlockSpec(memory_space=pl.ANY),
        pl.BlockSpec((None, d, tf), lambda g, s, sbe, sbr, sbf, na: (expert(g, sbe, na), 0, ff_step(g, s, na))),
        pl.BlockSpec((None, d, tf), lambda g, s, sbe, sbr, sbf, na: (expert(g, sbe, na), 0, nf + ff_step(g, s, na))),
        pl.BlockSpec((None, 1, tf), lambda g, s, sbe, sbr, sbf, na: (expert(g, sbe, na), 0, ff_step(g, s, na))),
        pl.BlockSpec((None, 1, tf), lambda g, s, sbe, sbr, sbf, na: (expert(g, sbe, na), 0, nf + ff_step(g, s, na))),
        pl.BlockSpec((None, d_ff, tn), lambda g, s, sbe, sbr, sbf, na: (expert(g, sbe, na), 0, down_step(g, s, na))),
        pl.BlockSpec((None, 1, tn), lambda g, s, sbe, sbr, sbf, na: (expert(g, sbe, na), 0, down_step(g, s, na))),
    ]
    grid_spec = pltpu.PrefetchScalarGridSpec(
        num_scalar_prefetch=4,
        grid=(n_super, nf + nn),
        in_specs=in_specs,
        out_specs=pl.BlockSpec((tm, tn), lambda g, s, sbe, sbr, sbf, na: (g, jnp.maximum(s - nf, 0))),
        scratch_shapes=[pltpu.VMEM((tm, d), BF16), pltpu.VMEM((nf, tm, tf), BF16),
                        pltpu.VMEM((n_slots, sub, d), F32), pltpu.SemaphoreType.DMA((n_slots,))],
    )
    return pl.pallas_call(
        functools.partial(_experts_kernel, nf),
        grid_spec=grid_spec,
        out_shape=jax.ShapeDtypeStruct((n_super * tm, d), F32),
        compiler_params=_params(("arbitrary", "arbitrary")),
        name="experts",
    )(sb_expert, sb_rows, sb_first, n_active, xs, w_gate_up, w_gate_up, bgu, bgu, w_down, bdn)


def _combine_kernel(alpha, dest_ref, next_ref, x1_ref, gate_ref, g_ref, b_ref, ys_hbm, o_ref, rows, sem):
    i = pl.program_id(0)
    tm, d = x1_ref.shape

    def gather(slots_ref, buf):
        for t in range(tm):
            for k in range(TOP_K):
                pltpu.make_async_copy(ys_hbm.at[pl.ds(slots_ref[k, t], 1), :],
                                      rows.at[buf, k, pl.ds(t, 1), :],
                                      sem.at[buf]).start(priority=k % 2)

    @pl.when(i == 0)
    def _():
        gather(dest_ref, 0)

    @pl.when(i + 1 < pl.num_programs(0))
    def _():
        gather(next_ref, (i + 1) % 2)

    buf = i % 2
    for k in range(TOP_K):
        pltpu.make_async_copy(ys_hbm.at[pl.ds(0, tm), :], rows.at[buf, k], sem.at[buf]).wait()

    gates = gate_ref[...]
    ffn = gates[:, 0:1] * rows[buf, 0]
    for k in range(1, TOP_K):
        ffn += gates[:, k:k + 1] * rows[buf, k]
    o_ref[...] = _layer_norm(alpha * x1_ref[...] + ffn, g_ref[...], b_ref[...])


def _combine(alpha, dest_flat, x1, gates, ln_g, ln_b, ys):
    t, d = x1.shape
    tm = COMBINE_ROWS
    return pl.pallas_call(
        functools.partial(_combine_kernel, alpha),
        grid=(t // tm,),
        in_specs=[pl.BlockSpec((TOP_K, tm), lambda i: (0, i), memory_space=pltpu.SMEM),
                  pl.BlockSpec((TOP_K, tm), lambda i: (0, jnp.minimum(i + 1, t // tm - 1)),
                               memory_space=pltpu.SMEM),
                  pl.BlockSpec((tm, d), lambda i: (i, 0)),
                  pl.BlockSpec((tm, TOP_K), lambda i: (i, 0)),
                  _full_vmem(), _full_vmem(),
                  pl.BlockSpec(memory_space=pl.ANY)],
        out_specs=pl.BlockSpec((tm, d), lambda i: (i, 0)),
        out_shape=jax.ShapeDtypeStruct((t, d), F32),
        scratch_shapes=[pltpu.VMEM((2, TOP_K, tm, d), F32), pltpu.SemaphoreType.DMA((2,))],
        compiler_params=_params(("arbitrary",)),
        name="combine",
    )(dest_flat, dest_flat, x1, gates, ln_g, ln_b, ys)


def _routing_tables(meta, counts, n_super):
    tm, sub = EXPERT_ROWS, EXPERT_SUB
    n_exp = counts.shape[0]
    tiles = (counts + sub - 1) // sub
    padded = tiles * sub
    row_end = jnp.cumsum(padded)
    row_start = row_end - padded
    per_expert = (tiles + tm // sub - 1) // (tm // sub)
    size = jnp.maximum((tiles + per_expert - 1) // jnp.maximum(per_expert, 1), 1) * sub
    sb_end = jnp.cumsum(per_expert)
    sb_start = sb_end - per_expert
    idx, rank = meta[:TOP_K], meta[TOP_K:]
    hit = idx[None] == jnp.arange(n_exp, dtype=I32)[:, None, None]

    def lookup(table):
        return jnp.sum(jnp.where(hit, table[:, None, None], 0), axis=0)

    size_t = lookup(size)
    dest_x = (lookup(row_start) + rank).astype(I32)
    dest_y = ((lookup(sb_start) + rank // size_t) * tm + rank % size_t).astype(I32)
    ztile = jnp.where(counts % sub != 0, row_start + counts // sub * sub, -1).astype(I32)
    g = jnp.arange(n_super, dtype=I32)
    owner = jnp.sum((g[:, None] >= sb_end[None, :]).astype(I32), axis=1)
    sb_expert = jnp.minimum(owner, n_exp - 1).astype(I32)
    local = (g - sb_start[sb_expert]) * size[sb_expert]
    active = g < sb_end[-1]
    sb_rows = jnp.where(active, jnp.clip(padded[sb_expert] - local, 0, size[sb_expert]), 0).astype(I32)
    sb_first = jnp.where(active, row_start[sb_expert] + local, 0).astype(I32)
    return (dest_x, dest_y, ztile, row_end[-1:].astype(I32), sb_expert, sb_rows, sb_first,
            sb_end[-1:].astype(I32))


def _layer(alpha, x, mem, w_in, w_spatial, b_spatial, a_ln_g, a_ln_b, w_mem_kv, norm_a_g, norm_b_g,
           norm_m_g, w_out, ln1_g, ln1_b, w_router, b_router, w_gate_up, b_gate_up, w_down, b_down,
           ln2_g, ln2_b):
    batch, seq, d = x.shape
    t = batch * seq
    m_width = w_mem_kv.shape[1] // 2
    a_width = a_ln_g.shape[0]
    b_width = norm_b_g.shape[0]
    n_exp = w_router.shape[1]
    row = lambda v: v.reshape(1, -1)

    x2 = x.reshape(t, d)
    za, zq, zk, zv, zm = _in_proj(x2, w_in.astype(BF16), (2 * a_width, b_width, b_width, b_width, m_width))
    kv = _kv_proj(mem, w_mem_kv.astype(BF16))
    ya, ym = _local_mix(za, zm, kv, w_spatial, b_spatial.T, row(a_ln_g), row(a_ln_b),
                        row(norm_a_g), row(norm_m_g), batch, seq)
    yb = _dilated_attention(zq, zk, zv, batch, seq)

    wr_hi = w_router.astype(BF16)
    wr_lo = (w_router - wr_hi.astype(F32)).astype(BF16)
    x1, gates, meta, counts = _out_router(
        alpha, x2, ya, yb, ym, w_out.astype(BF16), row(norm_b_g), row(ln1_g), row(ln1_b),
        wr_hi, wr_lo, row(b_router))

    n_assign = t * TOP_K
    n_super = (n_assign + n_exp * (EXPERT_ROWS - 1)) // EXPERT_ROWS
    n_x_rows = (n_assign + n_exp * (EXPERT_SUB - 1)) // EXPERT_SUB * EXPERT_SUB
    dest_x, dest_y, ztile, total_rows, sb_expert, sb_rows, sb_first, n_active = _routing_tables(
        meta, counts.reshape(n_exp), n_super)
    xs = _dispatch(x1, dest_x, ztile, total_rows, n_x_rows)
    ys = _experts(xs, sb_expert, sb_rows, sb_first, n_active, n_super, w_gate_up, b_gate_up, w_down, b_down)
    out = _combine(alpha, dest_y, x1, gates, row(ln2_g), row(ln2_b), ys)
    return out.reshape(batch, seq, d)


def kernel(x, mem, w_in, w_spatial, b_spatial, a_ln_g, a_ln_b, w_mem_kv, norm_a_g, norm_b_g, norm_m_g, w_out, ln1_g, ln1_b, w_router, b_router, w_gate_up, b_gate_up, w_down, b_down, ln2_g, ln2_b):
    depth = w_in.shape[0]
    alpha = (2.0 * depth) ** 0.25
    for l in range(depth):
        x = _layer(alpha, x, mem, w_in[l], w_spatial[l], b_spatial[l], a_ln_g[l], a_ln_b[l], w_mem_kv[l],
                   norm_a_g[l], norm_b_g[l], norm_m_g[l], w_out[l], ln1_g[l], ln1_b[l], w_router[l],
                   b_router[l], w_gate_up[l], b_gate_up[l], w_down[l], b_down[l], ln2_g[l], ln2_b[l])
    return x
```
